```python
import math
import jax
import jax.numpy as jnp
from jax import lax
import numpy as np

D_MODEL = 1024
BATCH = 4
SEQ = 4096
DEPTH = 2
DEC_BATCH = 32
DEC_SEQ = 32
PAST_LEN = 1024

CHUNK = 64
HEAD_DIM = 64
MIX_WIDTH = D_MODEL // 2
N_BRANCH = 3
RMS_EPS = 1e-6

A_HEADS = MIX_WIDTH // HEAD_DIM
A_LORA_W = 64
A_LORA_A = 64
A_LORA_G = 128
A_COLS = 3 * MIX_WIDTH + A_LORA_W + A_LORA_A + A_LORA_G
A_GN_EPS = 64e-5
A_DECAY_SCALE = math.exp(-0.5)

B_HEADS = MIX_WIDTH // HEAD_DIM
B_COLS = 3 * MIX_WIDTH
BAND_CHUNKS = 8
BAND = BAND_CHUNKS * CHUNK
MAX_REL = 2 * CHUNK
N_REL = 2 * MAX_REL + 1

C_HEADS = MIX_WIDTH // HEAD_DIM
C_COLS = 4 * MIX_WIDTH + 2 * C_HEADS
CONV_W = 4

IN_COLS = A_COLS + B_COLS + C_COLS

N_EXPERTS = 16
N_GROUPS = 4
EXPERTS_PER_GROUP = N_EXPERTS // N_GROUPS
TOP_K = 2
EXPERT_DIM = D_MODEL // 4

kernel_name = 'hybrid_stream_rwkv7_band_gdn_moe_step'


def rms_norm(x, w):
    xf = x.astype(jnp.float32)
    y = xf * lax.rsqrt(jnp.mean(xf * xf, axis=-1, keepdims=True) + RMS_EPS)
    return (y * w.astype(jnp.float32)).astype(x.dtype)


def l2_normalize(x):
    xf = x.astype(jnp.float32)
    return (xf * lax.rsqrt(jnp.sum(xf * xf, axis=-1, keepdims=True) + RMS_EPS)).astype(x.dtype)


def causal_conv_silu(u, buf, w):
    t = u.shape[1]
    up = jnp.concatenate([buf.astype(u.dtype), u], axis=1)
    out = up[:, 0:t] * w[0]
    for i in range(1, CONV_W):
        out = out + up[:, i:i + t] * w[i]
    return jax.nn.silu(out), up[:, up.shape[1] - (CONV_W - 1):]


def rwkv7_mixer(p, shift_prev, wkv0, mu, w0, w_up, a0, a_up, g_up, k_k, k_a, r_k, gn_w, gn_b):
    bsz, t, _ = p.shape
    p_prev = jnp.concatenate([shift_prev.astype(p.dtype)[:, None], p[:, :-1]], axis=1)
    xs = p + (p_prev - p) * mu
    i0, i1, i2 = MIX_WIDTH, 2 * MIX_WIDTH, 3 * MIX_WIDTH
    r, k, v, wd, ad, gd = jnp.split(xs, [i0, i1, i2, i2 + A_LORA_W, i2 + A_LORA_W + A_LORA_A], axis=-1)
    log_w = -A_DECAY_SCALE * jax.nn.sigmoid((w0 + jnp.tanh(wd) @ w_up).astype(jnp.float32))
    a = jax.nn.sigmoid(a0 + ad @ a_up)
    g = jax.nn.sigmoid(gd) @ g_up
    heads = lambda z: z.reshape(bsz, t, A_HEADS, HEAD_DIM)
    r, k, v, a, log_w = heads(r), heads(k), heads(v), heads(a), heads(log_w)
    kk = l2_normalize(k * k_k.reshape(A_HEADS, HEAD_DIM))
    k = k * (1.0 + (a - 1.0) * k_a.reshape(A_HEADS, HEAD_DIM))

    def step(s, inp):
        r_t, w_t, k_t, v_t, kk_t, b_t = inp
        s_kk = jnp.einsum('bhvk,bhk->bhv', s, kk_t)
        s = s * w_t[:, :, None, :] - s_kk[..., None] * b_t[:, :, None, :] + v_t[..., None] * k_t[:, :, None, :]
        return s, jnp.einsum('bhvk,bhk->bhv', s, r_t)

    seqs = tuple(jnp.moveaxis(z.astype(jnp.float32), 1, 0)
                 for z in (r, jnp.exp(log_w), k, v, kk, kk * a))
    s_final, o = lax.scan(step, wkv0.astype(jnp.float32), seqs)
    o = jnp.moveaxis(o, 0, 1)
    mean = jnp.mean(o, axis=-1, keepdims=True)
    var = jnp.mean(jnp.square(o - mean), axis=-1, keepdims=True)
    o = ((o - mean) * lax.rsqrt(var + A_GN_EPS)).reshape(bsz, t, MIX_WIDTH)
    o = o * gn_w.astype(jnp.float32) + gn_b.astype(jnp.float32)
    bonus = jnp.sum(r * k * r_k, axis=-1, keepdims=True) * v
    o = (o + bonus.reshape(bsz, t, MIX_WIDTH).astype(jnp.float32)) * g.astype(jnp.float32)
    return o.astype(p.dtype), p[:, -1], s_final


def rel_bias(table, dist):
    return table[:, jnp.clip(dist, -MAX_REL, MAX_REL) + MAX_REL].astype(jnp.float32)


def band_attention(q, kb, vb, bias, mask):
    s = jnp.einsum('bnqhd,bnkhd->bnhqk', q, kb).astype(jnp.float32) * (HEAD_DIM ** -0.5) + bias
    if mask is not None:
        s = jnp.where(mask, s, -1e30)
    pr = jax.nn.softmax(s, axis=-1).astype(vb.dtype)
    return jnp.einsum('bnhqk,bnkhd->bnqhd', pr, vb)


def band_prompt(q, k, v, table):
    bsz, t, h, d = q.shape
    nc = t // CHUNK
    nb = (BAND_CHUNKS + 1) * CHUNK
    padc = lambda z: jnp.pad(z, ((0, 0), (BAND, 0), (0, 0), (0, 0))).reshape(bsz, nc + BAND_CHUNKS, CHUNK, h, d)
    idx = jnp.arange(nc)[:, None] + jnp.arange(BAND_CHUNKS + 1)[None, :]
    kb = padc(k)[:, idx].reshape(bsz, nc, nb, h, d)
    vb = padc(v)[:, idx].reshape(bsz, nc, nb, h, d)
    qi = jnp.arange(CHUNK)
    kj = jnp.arange(nb)
    dist = qi[:, None] + BAND - kj[None, :]
    valid = (jnp.arange(nc)[:, None] - BAND_CHUNKS + kj[None, :] // CHUNK) >= 0
    o = band_attention(q.reshape(bsz, nc, CHUNK, h, d), kb, vb, rel_bias(table, dist), valid[:, None, None, :])
    return o.reshape(bsz, t, h * d)


def band_sample(q, k, v, k_past, v_past, table):
    bsz, t, h, d = q.shape
    pl = k_past.shape[1]
    kb = jnp.concatenate([k_past.astype(k.dtype), k], axis=1)
    vb = jnp.concatenate([v_past.astype(v.dtype), v], axis=1)
    dist = jnp.arange(t)[:, None] + pl - jnp.arange(pl + t)[None, :]
    o = band_attention(q[:, None], kb[:, None], vb[:, None], rel_bias(table, dist), None)
    return o.reshape(bsz, t, h * d)


def gated_delta_chunked(q, k, v, log_alpha, beta, s0):
    bsz, t, h, d = q.shape
    nc = -(-t // CHUNK)
    pad = nc * CHUNK - t
    padt = lambda z: jnp.pad(z, ((0, 0), (0, pad)) + ((0, 0),) * (z.ndim - 2))
    ch = lambda z: jnp.moveaxis(padt(z).reshape((bsz, nc, CHUNK, h) + z.shape[3:]), 3, 1)
    q, k, v, b = ch(q), ch(k), ch(v), ch(beta)
    g = jnp.cumsum(ch(log_alpha), axis=-1)
    diff = g[..., :, None] - g[..., None, :]
    tri_s = jnp.tril(jnp.ones((CHUNK, CHUNK), bool), -1)
    tri_i = jnp.tril(jnp.ones((CHUNK, CHUNK), bool))
    dec_s = jnp.where(tri_s, jnp.exp(jnp.where(tri_s, diff, 0.0)), 0.0)
    dec_i = jnp.where(tri_i, jnp.exp(jnp.where(tri_i, diff, 0.0)), 0.0)
    lmat = b[..., :, None] * jnp.einsum('bhntd,bhnjd->bhntj', k, k) * dec_s
    amat = jnp.eye(CHUNK, dtype=jnp.float32) + lmat
    eg = jnp.exp(g)
    rhs = jnp.concatenate([b[..., None] * v, (b * eg)[..., None] * k], axis=-1)
    sol = lax.linalg.triangular_solve(amat, rhs, left_side=True, lower=True, unit_diagonal=True)
    u_c, w_c = sol[..., :d], sol[..., d:]
    attn = jnp.einsum('bhntd,bhnjd->bhntj', q, k) * dec_i
    qg = q * eg[..., None]
    kd = k * jnp.exp(g[..., -1:] - g)[..., None]
    g_last = jnp.exp(g[..., -1])

    def step(s, inp):
        u_i, w_i, a_i, qg_i, kd_i, gl_i = inp
        delta = u_i - jnp.einsum('bhcd,bhde->bhce', w_i, s)
        o = jnp.einsum('bhcd,bhde->bhce', qg_i, s) + jnp.einsum('bhcj,bhje->bhce', a_i, delta)
        s = gl_i[..., None, None] * s + jnp.einsum('bhcd,bhce->bhde', kd_i, delta)
        return s, o

    xs = tuple(jnp.moveaxis(z, 2, 0) for z in (u_c, w_c, attn, qg, kd, g_last))
    s_final, o = lax.scan(step, s0, xs)
    o = jnp.moveaxis(jnp.moveaxis(o, 0, 2), 1, 3).reshape(bsz, nc * CHUNK, h, d)[:, :t]
    return o, s_final


def gdn_mixer(p, conv_buf, s0, conv_w, a_log, dt_bias, norm_w):
    bsz, t, _ = p.shape
    w3, w4 = 3 * MIX_WIDTH, 4 * MIX_WIDTH
    qkv, z, a_in, b_in = jnp.split(p, [w3, w4, w4 + C_HEADS], axis=-1)
    qkv, new_buf = causal_conv_silu(qkv, conv_buf, conv_w)
    q, k, v = jnp.split(qkv, 3, axis=-1)
    heads = lambda y: y.reshape(bsz, t, C_HEADS, HEAD_DIM).astype(jnp.float32)
    q = l2_normalize(heads(q)) * (HEAD_DIM ** -0.5)
    k = l2_normalize(heads(k))
    v = heads(v)
    log_alpha = -jnp.exp(a_log.astype(jnp.float32)) * jax.nn.softplus((a_in + dt_bias).astype(jnp.float32))
    beta = jax.nn.sigmoid(b_in.astype(jnp.float32))
    o, s_final = gated_delta_chunked(q, k, v, log_alpha, beta, s0.astype(jnp.float32))
    o = rms_norm(o, norm_w) * jax.nn.silu(heads(z))
    return o.reshape(bsz, t, MIX_WIDTH).astype(p.dtype), new_buf, s_final


def grouped_moe(h, w_router, router_bias, w_exp_gate, w_exp_up, w_exp_down):
    bsz, t, _ = h.shape
    scores = jax.nn.sigmoid((h @ w_router).astype(jnp.float32))
    sel = scores + router_bias.astype(jnp.float32)
    grp = sel.reshape(bsz, t, N_GROUPS, EXPERTS_PER_GROUP)
    grp_score = jnp.sum(lax.top_k(grp, TOP_K)[0], axis=-1)
    best = jnp.argmax(grp_score, axis=-1)
    in_group = (jnp.arange(N_EXPERTS) // EXPERTS_PER_GROUP) == best[..., None]
    _, idx = lax.top_k(jnp.where(in_group, sel, -jnp.inf), TOP_K)
    w_sel = jnp.take_along_axis(scores, idx, axis=-1)
    w_sel = w_sel / jnp.sum(w_sel, axis=-1, keepdims=True)
    combine = jnp.sum(jax.nn.one_hot(idx, N_EXPERTS, dtype=jnp.float32) * w_sel[..., None], axis=-2)
    hid = jax.nn.silu(jnp.einsum('btd,edf->btef', h, w_exp_gate)) * jnp.einsum('btd,edf->btef', h, w_exp_up)
    hid = hid * combine.astype(h.dtype)[..., None]
    return jnp.einsum('btef,efd->btd', hid, w_exp_down)


def trunk_layer(x, c, shift_prev, wkv0, k_past, v_past, conv_buf, s0,
                w_ada, b_ada, norm_mix_w, norm_ffn_w, w_in,
                rwkv_mu, rwkv_w0, rwkv_w_up, rwkv_a0, rwkv_a_up, rwkv_g_up,
                rwkv_k_k, rwkv_k_a, rwkv_r_k, rwkv_gn_w, rwkv_gn_b,
                band_q_norm, band_k_norm, band_rel_bias,
                gdn_conv_w, gdn_a_log, gdn_dt_bias, gdn_norm_w,
                w_branch, w_gate, b_gate, w_out,
                w_router, router_bias, w_exp_gate, w_exp_up, w_exp_down):
    bsz, t, _ = x.shape
    mod = (jax.nn.silu(c) @ w_ada + b_ada)[:, None, :]
    sh_mix, sc_mix, g_mix, sh_ffn, sc_ffn, g_ffn = jnp.split(mod, 6, axis=-1)
    h = rms_norm(x, norm_mix_w) * (1.0 + sc_mix) + sh_mix
    proj = h @ w_in
    pa, pb, pc = jnp.split(proj, [A_COLS, A_COLS + B_COLS], axis=-1)
    oa, new_shift, new_wkv = rwkv7_mixer(pa, shift_prev, wkv0, rwkv_mu, rwkv_w0, rwkv_w_up, rwkv_a0,
                                         rwkv_a_up, rwkv_g_up, rwkv_k_k, rwkv_k_a, rwkv_r_k,
                                         rwkv_gn_w, rwkv_gn_b)
    q, k, v = jnp.split(pb, 3, axis=-1)
    heads = lambda y: y.reshape(bsz, t, B_HEADS, HEAD_DIM)
    q = rms_norm(heads(q), band_q_norm)
    k = rms_norm(heads(k), band_k_norm)
    v = heads(v)
    if k_past is None:
        ob = band_prompt(q, k, v, band_rel_bias)
        new_k, new_v = k[:, -BAND:], v[:, -BAND:]
    else:
        ob = band_sample(q, k, v, k_past, v_past, band_rel_bias)
        new_k, new_v = k, v
    oc, new_conv, new_s = gdn_mixer(pc, conv_buf, s0, gdn_conv_w, gdn_a_log, gdn_dt_bias, gdn_norm_w)
    branches = jnp.stack([oa, ob.astype(oa.dtype), oc], axis=2)
    proj_b = jnp.einsum('btnw,nwd->btnd', branches, w_branch)
    gates = jax.nn.sigmoid(h @ w_gate + b_gate).reshape(bsz, t, N_BRANCH, D_MODEL)
    mix = jnp.sum(gates * proj_b, axis=2) @ w_out
    x = x + g_mix * mix
    h2 = rms_norm(x, norm_ffn_w) * (1.0 + sc_ffn) + sh_ffn
    x = x + g_ffn * grouped_moe(h2, w_router, router_bias, w_exp_gate, w_exp_up, w_exp_down)
    return x, (new_shift, new_wkv, new_k, new_v, new_conv, new_s)


def setup_inputs(seed: int = 0) -> dict:
    key = jax.random.key(seed)
    ks = iter(jax.random.split(key, 64))
    nrm = lambda shape, scale: scale * jax.random.normal(next(ks), shape, jnp.float32)
    band_past = min(BAND, PAST_LEN)
    L = DEPTH
    dt = jnp.exp(jax.random.uniform(next(ks), (L, C_HEADS), jnp.float32, math.log(1e-3), math.log(1e-1)))
    return {
        'x_prompt': nrm((BATCH, SEQ, D_MODEL), 1.0),
        'x_sample': nrm((DEC_BATCH, DEC_SEQ, D_MODEL), 1.0),
        'c_prompt': nrm((BATCH, D_MODEL), 1.0),
        'c_sample': nrm((DEC_BATCH, D_MODEL), 1.0),
        'state_rwkv_shift': nrm((L, DEC_BATCH, A_COLS), 1.0),
        'state_rwkv_wkv': nrm((L, DEC_BATCH, A_HEADS, HEAD_DIM, HEAD_DIM), 0.5),
        'cache_band_k': nrm((L, DEC_BATCH, band_past, B_HEADS, HEAD_DIM), 1.0),
        'cache_band_v': nrm((L, DEC_BATCH, band_past, B_HEADS, HEAD_DIM), 1.0),
        'state_gdn_conv': nrm((L, DEC_BATCH, CONV_W - 1, 3 * MIX_WIDTH), 1.0),
        'state_gdn_S': nrm((L, DEC_BATCH, C_HEADS, HEAD_DIM, HEAD_DIM), 0.1),
        'w_ada': nrm((L, D_MODEL, 6 * D_MODEL), 0.5 * D_MODEL ** -0.5),
        'b_ada': nrm((L, 6 * D_MODEL), 0.02),
        'norm_mix_w': 1.0 + nrm((L, D_MODEL), 0.02),
        'norm_ffn_w': 1.0 + nrm((L, D_MODEL), 0.02),
        'w_in': nrm((L, D_MODEL, IN_COLS), D_MODEL ** -0.5),
        'rwkv_mu': jax.random.uniform(next(ks), (L, A_COLS), jnp.float32),
        'rwkv_w0': -1.0 + nrm((L, MIX_WIDTH), 1.0),
        'rwkv_w_up': nrm((L, A_LORA_W, MIX_WIDTH), 0.1),
        'rwkv_a0': nrm((L, MIX_WIDTH), 0.5),
        'rwkv_a_up': nrm((L, A_LORA_A, MIX_WIDTH), 0.1),
        'rwkv_g_up': nrm((L, A_LORA_G, MIX_WIDTH), A_LORA_G ** -0.5),
        'rwkv_k_k': 0.85 + nrm((L, MIX_WIDTH), 0.05),
        'rwkv_k_a': 1.0 + nrm((L, MIX_WIDTH), 0.05),
        'rwkv_r_k': nrm((L, A_HEADS, HEAD_DIM), 0.1),
        'rwkv_gn_w': 1.0 + nrm((L, MIX_WIDTH), 0.02),
        'rwkv_gn_b': nrm((L, MIX_WIDTH), 0.02),
        'band_q_norm': 1.0 + nrm((L, HEAD_DIM), 0.02),
        'band_k_norm': 1.0 + nrm((L, HEAD_DIM), 0.02),
        'band_rel_bias': nrm((L, B_HEADS, N_REL), 0.2),
        'gdn_conv_w': nrm((L, CONV_W, 3 * MIX_WIDTH), CONV_W ** -0.5),
        'gdn_a_log': jnp.log(jax.random.uniform(next(ks), (L, C_HEADS), jnp.float32, 1.0, 16.0)),
        'gdn_dt_bias': dt + jnp.log(-jnp.expm1(-dt)),
        'gdn_norm_w': 1.0 + nrm((L, HEAD_DIM), 0.02),
        'w_branch': nrm((L, N_BRANCH, MIX_WIDTH, D_MODEL), MIX_WIDTH ** -0.5),
        'w_gate': nrm((L, D_MODEL, N_BRANCH * D_MODEL), D_MODEL ** -0.5),
        'b_gate': nrm((L, N_BRANCH * D_MODEL), 0.02),
        'w_out': nrm((L, D_MODEL, D_MODEL), D_MODEL ** -0.5),
        'w_router': nrm((D_MODEL, N_EXPERTS), D_MODEL ** -0.5),
        'router_bias': nrm((N_EXPERTS,), 0.01),
        'w_exp_gate': nrm((L, N_EXPERTS, D_MODEL, EXPERT_DIM), D_MODEL ** -0.5),
        'w_exp_up': nrm((L, N_EXPERTS, D_MODEL, EXPERT_DIM), D_MODEL ** -0.5),
        'w_exp_down': nrm((L, N_EXPERTS, EXPERT_DIM, D_MODEL), EXPERT_DIM ** -0.5),
    }


def reference(x_prompt, x_sample, c_prompt, c_sample,
              state_rwkv_shift, state_rwkv_wkv, cache_band_k, cache_band_v, state_gdn_conv, state_gdn_S,
              w_ada, b_ada, norm_mix_w, norm_ffn_w, w_in,
              rwkv_mu, rwkv_w0, rwkv_w_up, rwkv_a0, rwkv_a_up, rwkv_g_up,
              rwkv_k_k, rwkv_k_a, rwkv_r_k, rwkv_gn_w, rwkv_gn_b,
              band_q_norm, band_k_norm, band_rel_bias,
              gdn_conv_w, gdn_a_log, gdn_dt_bias, gdn_norm_w,
              w_branch, w_gate, b_gate, w_out,
              w_router, router_bias, w_exp_gate, w_exp_up, w_exp_down):
    def layer(x, c, st, l):
        return trunk_layer(x, c, *st,
                           w_ada[l], b_ada[l], norm_mix_w[l], norm_ffn_w[l], w_in[l],
                           rwkv_mu[l], rwkv_w0[l], rwkv_w_up[l], rwkv_a0[l], rwkv_a_up[l], rwkv_g_up[l],
                           rwkv_k_k[l], rwkv_k_a[l], rwkv_r_k[l], rwkv_gn_w[l], rwkv_gn_b[l],
                           band_q_norm[l], band_k_norm[l], band_rel_bias[l],
                           gdn_conv_w[l], gdn_a_log[l], gdn_dt_bias[l], gdn_norm_w[l],
                           w_branch[l], w_gate[l], b_gate[l], w_out[l],
                           w_router, router_bias, w_exp_gate[l], w_exp_up[l], w_exp_down[l])

    bsz = x_prompt.shape[0]
    zero_shift = jnp.zeros((bsz, A_COLS), x_prompt.dtype)
    zero_wkv = jnp.zeros((bsz, A_HEADS, HEAD_DIM, HEAD_DIM), jnp.float32)
    zero_conv = jnp.zeros((bsz, CONV_W - 1, 3 * MIX_WIDTH), x_prompt.dtype)
    zero_s = jnp.zeros((bsz, C_HEADS, HEAD_DIM, HEAD_DIM), jnp.float32)

    xp, xs = x_prompt, x_sample
    new_p, new_s = [], []
    for l in range(DEPTH):
        xp, st_p = layer(xp, c_prompt, (zero_shift, zero_wkv, None, None, zero_conv, zero_s), l)
        xs, st_s = layer(xs, c_sample, (state_rwkv_shift[l], state_rwkv_wkv[l], cache_band_k[l],
                                        cache_band_v[l], state_gdn_conv[l], state_gdn_S[l]), l)
        new_p.append(st_p)
        new_s.append(st_s)
    p_shift, p_wkv, p_k, p_v, p_conv, p_S = [jnp.stack(z, axis=0) for z in zip(*new_p)]
    s_shift, s_wkv, s_k, s_v, s_conv, s_S = [jnp.stack(z, axis=0) for z in zip(*new_s)]
    return (xp, xs, p_shift, p_wkv, p_k, p_v, p_conv, p_S, s_shift, s_wkv, s_k, s_v, s_conv, s_S)
```

```python
import functools
import math

import jax
import jax.numpy as jnp
from jax import lax
from jax.experimental import pallas as pl
from jax.experimental.pallas import tpu as pltpu

F32 = jnp.float32
BF16 = jnp.bfloat16
HIGHEST = lax.Precision.HIGHEST

D_MODEL = 1024
MIX = 512
HEAD_DIM = 64
N_HEADS = MIX // HEAD_DIM
LANES = 128
N_PAIRS = MIX // LANES
SUBLANES = 8
CHUNK = 64
A_COLS = 3 * MIX + 64 + 64 + 128
B_COLS = 3 * MIX
C_MAIN = 4 * MIX
CONV_W = 4
BAND = 8 * CHUNK
MAX_REL = 2 * CHUNK
N_EXPERTS = 16
N_GROUPS = 4
EXPERT_DIM = D_MODEL // 4
RMS_EPS = 1e-6
A_GN_EPS = 64e-5
A_DECAY_SCALE = math.exp(-0.5)
VMEM_LIMIT_BYTES = 56 * 1024 * 1024


def _params(*sem):
    return pltpu.CompilerParams(dimension_semantics=sem, vmem_limit_bytes=VMEM_LIMIT_BYTES)


def _dot(a, b, precision=None):
    return lax.dot_general(a, b, (((1,), (0,)), ((), ())), precision=precision, preferred_element_type=F32)


def _dot_nt(a, b, precision=None):
    return lax.dot_general(a, b, (((1,), (1,)), ((), ())), precision=precision, preferred_element_type=F32)


def _bdot(a, b):
    return _dot(a.astype(BF16), b.astype(BF16))


def _bdot_nt(a, b):
    return _dot_nt(a.astype(BF16), b.astype(BF16))


def _hdot(a, b):
    return _dot(a, b, HIGHEST)


def _sigmoid(x):
    return 1.0 / (1.0 + jnp.exp(-x))


def _silu(x):
    return x * _sigmoid(x)


def _softplus(x):
    return jnp.maximum(x, 0.0) + jnp.log1p(jnp.exp(-jnp.abs(x)))


def _iota(shape, dim):
    return lax.broadcasted_iota(jnp.int32, shape, dim)


def _head_ones():
    return (_iota((LANES, LANES), 0) // HEAD_DIM == _iota((LANES, LANES), 1) // HEAD_DIM).astype(F32)


def _head_sum(x, ones):
    parts = [_hdot(x[:, g * LANES:(g + 1) * LANES], ones) for g in range(x.shape[1] // LANES)]
    return parts[0] if len(parts) == 1 else jnp.concatenate(parts, axis=1)


def _stack(x):
    lane = _iota(x.shape, 1)
    return jnp.concatenate([jnp.where(lane < HEAD_DIM, x, 0.0), jnp.where(lane >= HEAD_DIM, x, 0.0)], axis=0)


def _unstack(xs):
    c = xs.shape[0] // 2
    return xs[:c] + xs[c:]


def _tri_masks(n, c):
    row, col = _iota((n, n), 0), _iota((n, n), 1)
    same = row // c == col // c
    return same & (col < row), same & (col <= row)


def _unit_lower_inverse(a, c):
    n = a.shape[0]
    row, col = _iota((n, n), 0), _iota((n, n), 1)
    eye = (row == col).astype(F32)
    same = lambda s: row // s == col // s
    a0 = jnp.where(same(SUBLANES), a, 0.0)
    a2 = _hdot(a0, a0)
    a4 = _hdot(a2, a2)
    x = _hdot(_hdot(eye - a0, eye + a2), eye + a4)
    s = SUBLANES
    while s < c:
        off = jnp.where(same(2 * s) & jnp.logical_not(same(s)), a, 0.0)
        x = x - _hdot(_hdot(x, off), x)
        s *= 2
    return x


def _adaln_kernel(c_ref, w_ref, b_ref, o_ref):
    o_ref[0] = _hdot(_silu(c_ref[...]), w_ref[0]) + b_ref[0]


def _adaln(c, w_ada, b_ada):
    depth, bsz, tn = w_ada.shape[0], c.shape[0], 768
    return pl.pallas_call(
        _adaln_kernel,
        grid=(depth, 6 * D_MODEL // tn),
        in_specs=[pl.BlockSpec((bsz, D_MODEL), lambda l, j: (0, 0)),
                  pl.BlockSpec((1, D_MODEL, tn), lambda l, j: (l, 0, j)),
                  pl.BlockSpec((1, 1, tn), lambda l, j: (l, 0, j))],
        out_specs=pl.BlockSpec((1, bsz, tn), lambda l, j: (l, 0, j)),
        out_shape=jax.ShapeDtypeStruct((depth, bsz, 6 * D_MODEL), F32),
        compiler_params=_params("arbitrary", "arbitrary"),
        name="adaln",
    )(c, w_ada, b_ada.reshape(depth, 1, 6 * D_MODEL))


def _norm_mod(x, norm_w, scale, shift):
    y = x * lax.rsqrt(jnp.mean(x * x, axis=-1, keepdims=True) + RMS_EPS)
    return y * norm_w * (1.0 + scale) + shift


def _inproj_kernel(x_ref, sh_ref, sc_ref, nw_ref, wa_ref, wb_ref, wc_ref, wab_ref, qn_ref, kn_ref,
                   pa_ref, q_ref, k_ref, v_ref, pc_ref, pab_ref):
    h = _norm_mod(x_ref[...], nw_ref[...], sc_ref[0], sh_ref[0]).astype(BF16)
    pa_ref[...] = _dot(h, wa_ref[...])
    pb = _dot(h, wb_ref[...])
    ones = _head_ones()

    def head_rms(y, w):
        return y * lax.rsqrt(_head_sum(y * y, ones) * (1.0 / HEAD_DIM) + RMS_EPS) * w

    q_ref[...] = head_rms(pb[:, :MIX], qn_ref[...])
    k_ref[...] = head_rms(pb[:, MIX:2 * MIX], kn_ref[...])
    v_ref[...] = pb[:, 2 * MIX:]
    pc_ref[...] = _dot(h, wc_ref[...])
    pab_ref[...] = _dot(h, wab_ref[...])


def _mod_spec(mod, tm, tiles_per_group):
    rows = mod.shape[1]
    return pl.BlockSpec((1, rows, D_MODEL), lambda i: (i // tiles_per_group, 0, 0))


def _const_spec(shape):
    zeros = (0,) * len(shape)
    return pl.BlockSpec(shape, lambda *_: zeros, pipeline_mode=pl.Buffered(1))


def _inproj(x, shift, scale, norm_w, wa, wb, wc, wab, q_norm, k_norm, tm, tiles_per_group):
    n = x.shape[0]
    row = lambda w: pl.BlockSpec((tm, w), lambda i: (i, 0))
    widths = (A_COLS, MIX, MIX, MIX, C_MAIN, LANES)
    return pl.pallas_call(
        _inproj_kernel,
        grid=(n // tm,),
        in_specs=[row(D_MODEL), _mod_spec(shift, tm, tiles_per_group), _mod_spec(scale, tm, tiles_per_group),
                  _const_spec((1, D_MODEL)), _const_spec(wa.shape), _const_spec(wb.shape), _const_spec(wc.shape),
                  _const_spec(wab.shape), _const_spec((1, MIX)), _const_spec((1, MIX))],
        out_specs=[row(w) for w in widths],
        out_shape=[jax.ShapeDtypeStruct((n, w), F32) for w in widths],
        compiler_params=_params("arbitrary"),
        name="inproj",
    )(x, shift, scale, norm_w, wa, wb, wc, wab, q_norm, k_norm)


def _rwkv_kernel(p_ref, shift0_ref, st0_ref, mu_ref, w0_ref, a0_ref, wwa_ref, gup_ref, kk_ref, ka_ref, rk_ref,
                 gnw_ref, gnb_ref, o_ref, st_out_ref, prev_scr, st_scr, *, c):
    n = 2 * c
    ci = pl.program_id(1)

    @pl.when(ci == 0)
    def _():
        prev_scr[...] = shift0_ref[0]
        st_scr[...] = st0_ref[0]

    p = p_ref[0]
    row = _iota((c, 1), 0)
    p_prev = jnp.where(row == 0, prev_scr[...], pltpu.roll(p, 1, axis=0))
    prev_scr[...] = p[c - 1:c, :]
    xs = p + (p_prev - p) * mu_ref[...]
    r, k, v = xs[:, :MIX], xs[:, MIX:2 * MIX], xs[:, 2 * MIX:3 * MIX]
    lora_in = xs[:, 3 * MIX:3 * MIX + LANES]
    gd = xs[:, 3 * MIX + LANES:]
    lane = _iota((1, LANES), 1)
    lora = _bdot(jnp.where(lane < HEAD_DIM, jnp.tanh(lora_in), lora_in), wwa_ref[...])
    log_w = -A_DECAY_SCALE * _sigmoid(w0_ref[...] + lora[:, :MIX])
    a = _sigmoid(a0_ref[...] + lora[:, MIX:])
    g = _bdot(_sigmoid(gd), gup_ref[...])
    ones = _head_ones()
    kk_raw = k * kk_ref[...]
    kk = kk_raw * lax.rsqrt(_head_sum(kk_raw * kk_raw, ones) + RMS_EPS)
    k = k * (1.0 + (a - 1.0) * ka_ref[...])
    b = kk * a
    tril = (_iota((c, c), 1) <= _iota((c, c), 0)).astype(F32)
    cum = _hdot(tril, log_w)
    cum_last = cum[c - 1:c, :]
    e_cum, e_neg = jnp.exp(cum), jnp.exp(-cum)
    e_prev, e_tail = jnp.exp(cum - log_w), jnp.exp(cum_last - cum)
    kk_t, r_t = kk * e_prev, r * e_cum
    k_h, b_h = k * e_neg, b * e_neg
    k_d, b_d = k * e_tail, b * e_tail
    p_last = jnp.exp(cum_last)
    strict, incl = _tri_masks(n, c)

    outs = []
    for j in range(N_PAIRS):
        sl = slice(j * LANES, (j + 1) * LANES)
        kk_s, r_s, v_s = _stack(kk_t[:, sl]), _stack(r_t[:, sl]), _stack(v[:, sl])
        amat = _dot_nt(jnp.concatenate([kk_s, r_s], axis=0),
                       jnp.concatenate([_stack(k_h[:, sl]), _stack(b_h[:, sl])], axis=0), HIGHEST)
        a_kk = jnp.where(strict, amat[:n, :n], 0.0)
        a_bk = jnp.where(strict, amat[:n, n:], 0.0)
        a_rk = jnp.where(incl, amat[n:, :n], 0.0)
        a_rb = jnp.where(incl, amat[n:, n:], 0.0)
        t_inv = _unit_lower_inverse(a_bk, c)
        av = _hdot(jnp.concatenate([a_kk, a_rk], axis=0), v_s)
        wu = _hdot(t_inv, jnp.concatenate([kk_s, av[:n]], axis=1))
        st = st_scr[j]
        zo = _hdot(jnp.concatenate([wu[:, :LANES], r_s], axis=0), st)
        z = wu[:, LANES:] + zo[:n]
        outs.append(_unstack(zo[n:] + av[n:] - _hdot(a_rb, z)))
        decay_col = jnp.transpose(jnp.broadcast_to(p_last[:, sl], (LANES, LANES)))
        st_scr[j] = (decay_col * st + _hdot(jnp.transpose(_stack(k_d[:, sl])), v_s)
                     - _hdot(jnp.transpose(_stack(b_d[:, sl])), z))
    o = jnp.concatenate(outs, axis=1)

    mean = _head_sum(o, ones) * (1.0 / HEAD_DIM)
    cen = o - mean
    var = _head_sum(cen * cen, ones) * (1.0 / HEAD_DIM)
    o = cen * lax.rsqrt(var + A_GN_EPS) * gnw_ref[...] + gnb_ref[...]
    bonus = _head_sum(r * k * rk_ref[...], ones) * v
    o_ref[0] = (o + bonus) * g
    st_out_ref[0] = st_scr[...]


def _rwkv(pa, shift0, st0, mu, w0, a0, wwa, gup, k_k, k_a, r_k, gn_w, gn_b, c):
    bsz, t, _ = pa.shape
    vec = lambda w: _const_spec((1, w))
    state = pl.BlockSpec((1, N_PAIRS, LANES, LANES), lambda b, i: (b, 0, 0, 0))
    return pl.pallas_call(
        functools.partial(_rwkv_kernel, c=c),
        grid=(bsz, t // c),
        in_specs=[pl.BlockSpec((1, c, A_COLS), lambda b, i: (b, i, 0)),
                  pl.BlockSpec((1, 1, A_COLS), lambda b, i: (b, 0, 0)), state,
                  vec(A_COLS), vec(MIX), vec(MIX), _const_spec(wwa.shape), _const_spec(gup.shape),
                  vec(MIX), vec(MIX), vec(MIX), vec(MIX), vec(MIX)],
        out_specs=[pl.BlockSpec((1, c, MIX), lambda b, i: (b, i, 0)), state],
        out_shape=[jax.ShapeDtypeStruct((bsz, t, MIX), F32),
                   jax.ShapeDtypeStruct((bsz, N_PAIRS, LANES, LANES), F32)],
        scratch_shapes=[pltpu.VMEM((1, A_COLS), F32), pltpu.VMEM((N_PAIRS, LANES, LANES), F32)],
        compiler_params=_params("arbitrary", "arbitrary"),
        name="rwkv",
    )(pa, shift0, st0, mu, w0, a0, wwa, gup, k_k, k_a, r_k, gn_w, gn_b)


def _gdn_kernel(pc_ref, pab_ref, conv0_ref, st0_ref, convw_ref, alog_ref, dtb_ref, nw_ref, eab_ref,
                o_ref, st_out_ref, ext_scr, st_scr, *, c):
    n = 2 * c
    w3 = 3 * MIX
    ci = pl.program_id(1)

    @pl.when(ci == 0)
    def _():
        ext_scr[0:SUBLANES, :] = conv0_ref[0]
        st_scr[...] = st0_ref[0]

    ext_scr[SUBLANES:SUBLANES + c, :] = pc_ref[0, :, :w3]
    conv = ext_scr[SUBLANES:SUBLANES + c, :] * convw_ref[CONV_W - 1:CONV_W, :]
    for s in range(1, CONV_W):
        conv = conv + ext_scr[SUBLANES - s:SUBLANES - s + c, :] * convw_ref[CONV_W - 1 - s:CONV_W - s, :]
    ext_scr[0:SUBLANES, :] = ext_scr[c:c + SUBLANES, :]
    qkv = _silu(conv)
    ones = _head_ones()
    l2n = lambda y: y * lax.rsqrt(_head_sum(y * y, ones) + RMS_EPS)
    q = l2n(qkv[:, :MIX]) * (HEAD_DIM ** -0.5)
    k = l2n(qkv[:, MIX:2 * MIX])
    v = qkv[:, 2 * MIX:]
    zgate = pc_ref[0, :, w3:]
    ab = _hdot(pab_ref[0], eab_ref[...])
    log_alpha = -jnp.exp(alog_ref[...]) * _softplus(ab[:, :MIX] + dtb_ref[...])
    beta = _sigmoid(ab[:, MIX:])
    tril = (_iota((c, c), 1) <= _iota((c, c), 0)).astype(F32)
    gcum = _hdot(tril, log_alpha)
    g_last = gcum[c - 1:c, :]
    eg = jnp.exp(gcum)
    bk = beta * k
    rhs_u, rhs_w = beta * v, bk * eg
    qg = q * eg
    kd = k * jnp.exp(g_last - gcum)
    strict, incl = _tri_masks(n, c)

    outs = []
    for j in range(N_PAIRS):
        sl = slice(j * LANES, (j + 1) * LANES)
        g_s = _stack(gcum[:, sl])
        g_col = g_s + pltpu.roll(g_s, HEAD_DIM, axis=1)
        g_row = jnp.transpose(g_col)[:n, :]
        diff = g_col[:, :n] - g_row
        dec_s = jnp.where(strict, jnp.exp(jnp.where(strict, diff, 0.0)), 0.0)
        dec_i = jnp.where(incl, jnp.exp(jnp.where(incl, diff, 0.0)), 0.0)
        k_s = _stack(k[:, sl])
        kmat = _dot_nt(jnp.concatenate([_stack(bk[:, sl]), _stack(q[:, sl])], axis=0), k_s, HIGHEST)
        lmat = kmat[:n] * dec_s
        attn = kmat[n:] * dec_i
        t_inv = _unit_lower_inverse(lmat, c)
        sol = _hdot(t_inv, jnp.concatenate([_stack(rhs_u[:, sl]), _stack(rhs_w[:, sl])], axis=1))
        st = st_scr[j]
        wq = _hdot(jnp.concatenate([sol[:, LANES:], _stack(qg[:, sl])], axis=0), st)
        delta = sol[:, :LANES] - wq[:n]
        outs.append(_unstack(wq[n:] + _hdot(attn, delta)))
        decay_col = jnp.transpose(jnp.broadcast_to(jnp.exp(g_last[:, sl]), (LANES, LANES)))
        st_scr[j] = decay_col * st + _hdot(jnp.transpose(_stack(kd[:, sl])), delta)
    o = jnp.concatenate(outs, axis=1)

    o = o * lax.rsqrt(_head_sum(o * o, ones) * (1.0 / HEAD_DIM) + RMS_EPS) * nw_ref[...]
    o_ref[0] = o * _silu(zgate)
    st_out_ref[0] = st_scr[...]


def _gdn(pc, pab, conv0, st0, conv_w, a_log, dt_bias, norm_w, eab, c):
    bsz, t, _ = pc.shape
    vec = lambda w: _const_spec((1, w))
    state = pl.BlockSpec((1, N_PAIRS, LANES, LANES), lambda b, i: (b, 0, 0, 0))
    return pl.pallas_call(
        functools.partial(_gdn_kernel, c=c),
        grid=(bsz, t // c),
        in_specs=[pl.BlockSpec((1, c, C_MAIN), lambda b, i: (b, i, 0)),
                  pl.BlockSpec((1, c, LANES), lambda b, i: (b, i, 0)),
                  pl.BlockSpec((1, SUBLANES, 3 * MIX), lambda b, i: (b, 0, 0)), state,
                  _const_spec((CONV_W, 3 * MIX)), vec(MIX), vec(MIX), vec(MIX), _const_spec(eab.shape)],
        out_specs=[pl.BlockSpec((1, c, MIX), lambda b, i: (b, i, 0)), state],
        out_shape=[jax.ShapeDtypeStruct((bsz, t, MIX), F32),
                   jax.ShapeDtypeStruct((bsz, N_PAIRS, LANES, LANES), F32)],
        scratch_shapes=[pltpu.VMEM((c + SUBLANES, 3 * MIX), F32), pltpu.VMEM((N_PAIRS, LANES, LANES), F32)],
        compiler_params=_params("arbitrary", "arbitrary"),
        name="gdn",
    )(pc, pab, conv0, st0, conv_w, a_log, dt_bias, norm_w, eab)


def _band_kernel(q_ref, k_ref, v_ref, bias_ref, o_ref, *, cq, wn, masked):
    ci = pl.program_id(1)
    start = pl.multiple_of(ci * cq, cq)
    q = q_ref[0]
    kw = k_ref[0, pl.ds(start, wn), :]
    vw = v_ref[0, pl.ds(start, wn), :]
    lane = _iota((1, LANES), 1)
    if masked:
        valid = _iota((1, wn), 1) >= BAND - ci * cq
    outs = []
    for j in range(N_PAIRS):
        sl = slice(j * LANES, (j + 1) * LANES)
        qp, kp, vp = q[:, sl], kw[:, sl], vw[:, sl]
        acc = jnp.zeros((cq, LANES), F32)
        for hh in range(2):
            m = (lane < HEAD_DIM) if hh == 0 else (lane >= HEAD_DIM)
            s = _dot_nt(jnp.where(m, qp, 0.0).astype(BF16), kp) * (HEAD_DIM ** -0.5) + bias_ref[2 * j + hh]
            if masked:
                s = jnp.where(valid, s, -1e30)
            e = jnp.exp(s - jnp.max(s, axis=-1, keepdims=True))
            pv = _dot(e.astype(BF16), jnp.where(m, vp, jnp.zeros_like(vp)))
            acc = acc + pv / jnp.sum(e, axis=-1, keepdims=True)
        outs.append(acc)
    o_ref[0] = jnp.concatenate(outs, axis=1)


def _band(q, k_ext, v_ext, bias, cq, masked):
    bsz, t, _ = q.shape
    text = k_ext.shape[1]
    wn = text - t + cq
    ext = pl.BlockSpec((1, text, MIX), lambda b, i: (b, 0, 0))
    return pl.pallas_call(
        functools.partial(_band_kernel, cq=cq, wn=wn, masked=masked),
        grid=(bsz, t // cq),
        in_specs=[pl.BlockSpec((1, cq, MIX), lambda b, i: (b, i, 0)), ext, ext, _const_spec(bias.shape)],
        out_specs=pl.BlockSpec((1, cq, MIX), lambda b, i: (b, i, 0)),
        out_shape=jax.ShapeDtypeStruct((bsz, t, MIX), F32),
        compiler_params=_params("arbitrary", "arbitrary"),
        name="band",
    )(q, k_ext, v_ext, bias)


def _merge_kernel(x_ref, oa_ref, ob_ref, oc_ref, sh_ref, sc_ref, gm_ref, sh2_ref, sc2_ref, nw_ref, nw2_ref,
                  wgate_ref, bgate_ref, wbr_ref, wout_ref, wrt_ref, rb_ref,
                  x1_ref, h2_ref, comb_ref):
    x = x_ref[...]
    h = _norm_mod(x, nw_ref[...], sc_ref[0], sh_ref[0]).astype(BF16)
    gates = _sigmoid(_dot(h, wgate_ref[...]) + bgate_ref[...])
    mixed = None
    for i, o_ref in enumerate((oa_ref, ob_ref, oc_ref)):
        term = gates[:, i * D_MODEL:(i + 1) * D_MODEL] * _bdot(o_ref[...], wbr_ref[i])
        mixed = term if mixed is None else mixed + term
    x1 = x + gm_ref[0] * _bdot(mixed, wout_ref[...])
    x1_ref[...] = x1
    h2 = _norm_mod(x1, nw2_ref[...], sc2_ref[0], sh2_ref[0])
    h2_ref[...] = h2.astype(BF16)

    scores = _sigmoid(_dot_nt(wrt_ref[...], h2, HIGHEST))
    sel = scores + rb_ref[...]
    tm = scores.shape[1]
    per = N_EXPERTS // N_GROUPS
    best_val, best = None, None
    for g in range(N_GROUPS):
        rows = [sel[g * per + i:g * per + i + 1, :] for i in range(per)]
        top2 = None
        for i in range(per):
            for i2 in range(i + 1, per):
                pair = rows[i] + rows[i2]
                top2 = pair if top2 is None else jnp.maximum(top2, pair)
        if g == 0:
            best_val, best = top2, jnp.zeros((1, tm), jnp.int32)
        else:
            better = top2 > best_val
            best = jnp.where(better, g, best)
            best_val = jnp.where(better, top2, best_val)
    eidx = _iota((N_EXPERTS, tm), 0)
    cand = jnp.where(eidx // per == best, sel, -jnp.inf)
    m1 = jnp.max(cand, axis=0, keepdims=True)
    i1 = jnp.min(jnp.where(cand == m1, eidx, N_EXPERTS), axis=0, keepdims=True)
    cand2 = jnp.where(eidx == i1, -jnp.inf, cand)
    m2 = jnp.max(cand2, axis=0, keepdims=True)
    i2 = jnp.min(jnp.where(cand2 == m2, eidx, N_EXPERTS), axis=0, keepdims=True)
    w1 = jnp.sum(jnp.where(eidx == i1, scores, 0.0), axis=0, keepdims=True)
    w2 = jnp.sum(jnp.where(eidx == i2, scores, 0.0), axis=0, keepdims=True)
    den = w1 + w2
    comb_ref[...] = jnp.where(eidx == i1, w1 / den, 0.0) + jnp.where(eidx == i2, w2 / den, 0.0)


def _merge(x, oa, ob, oc, mods, norm_w, norm2_w, w_gate, b_gate, w_branch, w_out, w_router_t, router_bias,
           tm, tiles_per_group):
    n = x.shape[0]
    row = lambda w: pl.BlockSpec((tm, w), lambda i: (i, 0))
    mod_specs = [_mod_spec(m, tm, tiles_per_group) for m in mods]
    return pl.pallas_call(
        _merge_kernel,
        grid=(n // tm,),
        in_specs=[row(D_MODEL), row(MIX), row(MIX), row(MIX)] + mod_specs
                 + [_const_spec((1, D_MODEL)), _const_spec((1, D_MODEL)), _const_spec(w_gate.shape),
                    _const_spec(b_gate.shape), _const_spec(w_branch.shape), _const_spec(w_out.shape),
                    _const_spec(w_router_t.shape), _const_spec(router_bias.shape)],
        out_specs=[row(D_MODEL), row(D_MODEL), pl.BlockSpec((N_EXPERTS, tm), lambda i: (0, i))],
        out_shape=[jax.ShapeDtypeStruct((n, D_MODEL), F32), jax.ShapeDtypeStruct((n, D_MODEL), BF16),
                   jax.ShapeDtypeStruct((N_EXPERTS, n), F32)],
        compiler_params=_params("arbitrary"),
        name="merge",
    )(x, oa, ob, oc, *mods, norm_w, norm2_w, w_gate, b_gate, w_branch, w_out, w_router_t, router_bias)


def _moe_kernel(x1_ref, h2_ref, comb_ref, gf_ref, wg_ref, wu_ref, wd_ref, o_ref):
    h2 = h2_ref[...]
    comb = jnp.transpose(comb_ref[...])
    acc = jnp.zeros(o_ref.shape, F32)
    for e in range(N_EXPERTS):
        hid = _silu(_dot(h2, wg_ref[e])) * _dot(h2, wu_ref[e]) * comb[:, e:e + 1]
        acc = acc + _dot(hid.astype(BF16), wd_ref[e])
    o_ref[...] = x1_ref[...] + gf_ref[0] * acc


def _moe(x1, h2, comb, g_ffn, wg, wu, wd, tm, tiles_per_group):
    n = x1.shape[0]
    row = lambda w: pl.BlockSpec((tm, w), lambda i: (i, 0))
    return pl.pallas_call(
        _moe_kernel,
        grid=(n // tm,),
        in_specs=[row(D_MODEL), row(D_MODEL), pl.BlockSpec((N_EXPERTS, tm), lambda i: (0, i)),
                  _mod_spec(g_ffn, tm, tiles_per_group),
                  _const_spec(wg.shape), _const_spec(wu.shape), _const_spec(wd.shape)],
        out_specs=row(D_MODEL),
        out_shape=jax.ShapeDtypeStruct((n, D_MODEL), F32),
        compiler_params=_params("arbitrary"),
        name="moe",
    )(x1, h2, comb, g_ffn, wg, wu, wd)


def _pair_state(s):
    bsz = s.shape[0]
    s = s.reshape(bsz, N_PAIRS, 2, HEAD_DIM, HEAD_DIM)
    z = jnp.zeros_like(s[:, :, 0])
    top = jnp.concatenate([s[:, :, 0], z], axis=-1)
    bot = jnp.concatenate([z, s[:, :, 1]], axis=-1)
    return jnp.concatenate([top, bot], axis=-2)


def _unpair_state(s):
    bsz = s.shape[0]
    return jnp.stack([s[:, :, :HEAD_DIM, :HEAD_DIM], s[:, :, HEAD_DIM:, HEAD_DIM:]], axis=2).reshape(
        bsz, N_HEADS, HEAD_DIM, HEAD_DIM)


def _rel_bias(table, cq, wn):
    dist = jnp.arange(cq)[:, None] + BAND - jnp.arange(wn)[None, :]
    return table[:, jnp.clip(dist, -MAX_REL, MAX_REL) + MAX_REL].astype(F32)


def _layer(x, mod, state, wts, tm):
    bsz, t, _ = x.shape
    n = bsz * t
    c = min(CHUNK, t)
    shift0, wkv0, k_past, v_past, conv0, s0 = state
    if t % tm == 0:
        tiles_per_group = t // tm
        mods = [m.reshape(bsz, 1, D_MODEL) for m in jnp.split(mod, 6, axis=-1)]
    else:
        tiles_per_group = 1
        mods = [jnp.repeat(m, t, axis=0).reshape(n // tm, tm, D_MODEL) for m in jnp.split(mod, 6, axis=-1)]
    sh_mix, sc_mix, g_mix, sh_ffn, sc_ffn, g_ffn = mods
    xf = x.reshape(n, D_MODEL)

    pa, q, k, v, pc, pab = _inproj(xf, sh_mix, sc_mix, wts["norm_mix_w"], wts["wa"], wts["wb"], wts["wc"],
                                   wts["wab"], wts["q_norm"], wts["k_norm"], tm, tiles_per_group)
    pa = pa.reshape(bsz, t, A_COLS)
    q, k, v = (z.reshape(bsz, t, MIX) for z in (q, k, v))
    pc = pc.reshape(bsz, t, C_MAIN)

    oa, wkv = _rwkv(pa, shift0.reshape(bsz, 1, A_COLS), _pair_state(jnp.swapaxes(wkv0, -1, -2)),
                    wts["mu"], wts["w0"], wts["a0"], wts["wwa"], wts["g_up"], wts["k_k"], wts["k_a"], wts["r_k"],
                    wts["gn_w"], wts["gn_b"], c)
    new_wkv = jnp.swapaxes(_unpair_state(wkv), -1, -2)

    if k_past is None:
        pad = lambda z: jnp.pad(z.astype(BF16), ((0, 0), (BAND, 0), (0, 0)))
        ob = _band(q, pad(k), pad(v), _rel_bias(wts["rel_bias"], CHUNK, BAND + CHUNK), CHUNK, True)
        new_k, new_v = k[:, -BAND:], v[:, -BAND:]
    else:
        past = k_past.shape[1]
        cat = lambda zp, z: jnp.concatenate([zp.reshape(bsz, past, MIX), z], axis=1).astype(BF16)
        ob = _band(q, cat(k_past, k), cat(v_past, v), _rel_bias(wts["rel_bias"], t, past + t), t, False)
        new_k, new_v = k, v

    conv_pad = jnp.pad(conv0, ((0, 0), (SUBLANES - (CONV_W - 1), 0), (0, 0)))
    oc, s_new = _gdn(pc, pab.reshape(bsz, t, LANES), conv_pad, _pair_state(s0), wts["conv_w"], wts["a_log"],
                     wts["dt_bias"], wts["gdn_norm_w"], wts["eab"], c)
    new_conv = jnp.concatenate([conv0, pc[:, :, :3 * MIX]], axis=1)[:, -(CONV_W - 1):]

    x1, h2, comb = _merge(xf, oa.reshape(n, MIX), ob.reshape(n, MIX), oc.reshape(n, MIX),
                          [sh_mix, sc_mix, g_mix, sh_ffn, sc_ffn], wts["norm_mix_w"], wts["norm_ffn_w"],
                          wts["w_gate"], wts["b_gate"], wts["w_branch"], wts["w_out"], wts["w_router_t"],
                          wts["router_bias"], tm, tiles_per_group)
    x2 = _moe(x1, h2, comb, g_ffn, wts["wg"], wts["wu"], wts["wd"], tm, tiles_per_group)
    heads = lambda z: z.reshape(bsz, -1, N_HEADS, HEAD_DIM)
    return x2.reshape(bsz, t, D_MODEL), (pa[:, -1], new_wkv, heads(new_k), heads(new_v), new_conv,
                                         _unpair_state(s_new))


def _prepare_layer(l, w_in, norm_mix_w, norm_ffn_w, rwkv_mu, rwkv_w0, rwkv_w_up, rwkv_a0, rwkv_a_up, rwkv_g_up,
                   rwkv_k_k, rwkv_k_a, rwkv_r_k, rwkv_gn_w, rwkv_gn_b, band_q_norm, band_k_norm, band_rel_bias,
                   gdn_conv_w, gdn_a_log, gdn_dt_bias, gdn_norm_w, w_branch, w_gate, b_gate, w_out,
                   w_router, router_bias, w_exp_gate, w_exp_up, w_exp_down):
    row = lambda z: z.reshape(1, -1).astype(F32)
    per_head = lambda z: jnp.repeat(z, HEAD_DIM).reshape(1, MIX)
    win = w_in[l].astype(BF16)
    c0 = A_COLS + B_COLS
    wab = jnp.pad(win[:, c0 + C_MAIN:], ((0, 0), (0, LANES - 2 * N_HEADS)))
    zeros = jnp.zeros((HEAD_DIM, MIX), F32)
    wwa = jnp.concatenate([jnp.concatenate([rwkv_w_up[l], zeros], axis=1),
                           jnp.concatenate([zeros, rwkv_a_up[l]], axis=1)], axis=0)
    head_of_lane = jnp.arange(MIX) // HEAD_DIM
    src = jnp.arange(LANES)[:, None]
    eab = jnp.concatenate([(src == head_of_lane[None, :]), (src == N_HEADS + head_of_lane[None, :])],
                          axis=1).astype(F32)
    return dict(
        norm_mix_w=row(norm_mix_w[l]), norm_ffn_w=row(norm_ffn_w[l]),
        wa=win[:, :A_COLS], wb=win[:, A_COLS:c0], wc=win[:, c0:c0 + C_MAIN], wab=wab,
        q_norm=row(jnp.tile(band_q_norm[l], N_HEADS)), k_norm=row(jnp.tile(band_k_norm[l], N_HEADS)),
        mu=row(rwkv_mu[l]), w0=row(rwkv_w0[l]), a0=row(rwkv_a0[l]), wwa=wwa, g_up=rwkv_g_up[l],
        k_k=row(rwkv_k_k[l]), k_a=row(rwkv_k_a[l]), r_k=row(rwkv_r_k[l]), gn_w=row(rwkv_gn_w[l]),
        gn_b=row(rwkv_gn_b[l]), rel_bias=band_rel_bias[l],
        conv_w=gdn_conv_w[l], a_log=per_head(gdn_a_log[l]), dt_bias=per_head(gdn_dt_bias[l]),
        gdn_norm_w=row(jnp.tile(gdn_norm_w[l], N_HEADS)), eab=eab,
        w_gate=w_gate[l].astype(BF16), b_gate=row(b_gate[l]), w_branch=w_branch[l].astype(BF16),
        w_out=w_out[l].astype(BF16), w_router_t=jnp.transpose(w_router), router_bias=router_bias.reshape(-1, 1),
        wg=w_exp_gate[l].astype(BF16), wu=w_exp_up[l].astype(BF16), wd=w_exp_down[l].astype(BF16),
    )


def kernel(x_prompt, x_sample, c_prompt, c_sample, state_rwkv_shift, state_rwkv_wkv, cache_band_k, cache_band_v, state_gdn_conv, state_gdn_S, w_ada, b_ada, norm_mix_w, norm_ffn_w, w_in, rwkv_mu, rwkv_w0, rwkv_w_up, rwkv_a0, rwkv_a_up, rwkv_g_up, rwkv_k_k, rwkv_k_a, rwkv_r_k, rwkv_gn_w, rwkv_gn_b, band_q_norm, band_k_norm, band_rel_bias, gdn_conv_w, gdn_a_log, gdn_dt_bias, gdn_norm_w, w_branch, w_gate, b_gate, w_out, w_router, router_bias, w_exp_gate, w_exp_up, w_exp_down):
    depth = w_ada.shape[0]
    bsz = x_prompt.shape[0]
    mod_p = _adaln(c_prompt, w_ada, b_ada)
    mod_s = _adaln(c_sample, w_ada, b_ada)
    zero_state = (jnp.zeros((bsz, A_COLS), F32), jnp.zeros((bsz, N_HEADS, HEAD_DIM, HEAD_DIM), F32), None, None,
                  jnp.zeros((bsz, CONV_W - 1, 3 * MIX), F32), jnp.zeros((bsz, N_HEADS, HEAD_DIM, HEAD_DIM), F32))
    tm = 256
    xp, xs = x_prompt, x_sample
    new_p, new_s = [], []
    for l in range(depth):
        wts = _prepare_layer(l, w_in, norm_mix_w, norm_ffn_w, rwkv_mu, rwkv_w0, rwkv_w_up, rwkv_a0, rwkv_a_up,
                             rwkv_g_up, rwkv_k_k, rwkv_k_a, rwkv_r_k, rwkv_gn_w, rwkv_gn_b, band_q_norm,
                             band_k_norm, band_rel_bias, gdn_conv_w, gdn_a_log, gdn_dt_bias, gdn_norm_w,
                             w_branch, w_gate, b_gate, w_out, w_router, router_bias, w_exp_gate, w_exp_up,
                             w_exp_down)
        xp, st_p = _layer(xp, mod_p[l], zero_state, wts, tm)
        xs, st_s = _layer(xs, mod_s[l], (state_rwkv_shift[l], state_rwkv_wkv[l], cache_band_k[l], cache_band_v[l],
                                         state_gdn_conv[l], state_gdn_S[l]), wts, tm)
        new_p.append(st_p)
        new_s.append(st_s)
    p_out = [jnp.stack(z, axis=0) for z in zip(*new_p)]
    s_out = [jnp.stack(z, axis=0) for z in zip(*new_s)]
    return (xp, xs, *p_out, *s_out)
```

```python
import functools
import math

import jax
import jax.numpy as jnp
from jax import lax
from jax.experimental import pallas as pl
from jax.experimental.pallas import tpu as pltpu

F32 = jnp.float32
BF16 = jnp.bfloat16
HIGHEST = lax.Precision.HIGHEST

D_MODEL = 1024
MIX = 512
HEAD_DIM = 64
N_HEADS = MIX // HEAD_DIM
LANES = 128
N_PAIRS = MIX // LANES
SUBLANES = 8
CHUNK = 64
A_COLS = 3 * MIX + 64 + 64 + 128
B_COLS = 3 * MIX
C_MAIN = 4 * MIX
CONV_W = 4
BAND = 8 * CHUNK
MAX_REL = 2 * CHUNK
N_EXPERTS = 16
N_GROUPS = 4
EXPERT_DIM = D_MODEL // 4
RMS_EPS = 1e-6
A_GN_EPS = 64e-5
A_DECAY_SCALE = math.exp(-0.5)
VMEM_LIMIT_BYTES = 56 * 1024 * 1024


def _params(*sem):
    return pltpu.CompilerParams(dimension_semantics=sem, vmem_limit_bytes=VMEM_LIMIT_BYTES)


def _dot(a, b, precision=None):
    return lax.dot_general(a, b, (((1,), (0,)), ((), ())), precision=precision, preferred_element_type=F32)


def _dot_nt(a, b, precision=None):
    return lax.dot_general(a, b, (((1,), (1,)), ((), ())), precision=precision, preferred_element_type=F32)


def _bdot(a, b):
    return _dot(a.astype(BF16), b.astype(BF16))


def _bdot_nt(a, b):
    return _dot_nt(a.astype(BF16), b.astype(BF16))


def _hdot(a, b):
    return _dot(a, b, HIGHEST)


def _split(a):
    hi = a.astype(BF16)
    return hi, (a - hi.astype(F32)).astype(BF16)


def _mm(a, b, passes, nt=False):
    dot = _dot_nt if nt else _dot
    if passes == 6:
        return dot(a, b, HIGHEST)
    if passes == 1:
        return dot(a.astype(BF16), b.astype(BF16))
    a_hi, a_lo = _split(a)
    b_hi, b_lo = _split(b)
    return dot(a_hi, b_hi) + (dot(a_hi, b_lo) + dot(a_lo, b_hi))


P_AMAT, P_INV, P_SOLVE, P_STATE = 1, 1, 1, 1


def _sigmoid(x):
    return 1.0 / (1.0 + jnp.exp(-x))


def _silu(x):
    return x * _sigmoid(x)


def _softplus(x):
    return jnp.maximum(x, 0.0) + jnp.log1p(jnp.exp(-jnp.abs(x)))


def _iota(shape, dim):
    return lax.broadcasted_iota(jnp.int32, shape, dim)


def _head_ones():
    return (_iota((LANES, LANES), 0) // HEAD_DIM == _iota((LANES, LANES), 1) // HEAD_DIM).astype(F32)


def _head_sum(x, ones):
    parts = [_hdot(x[:, g * LANES:(g + 1) * LANES], ones) for g in range(x.shape[1] // LANES)]
    return parts[0] if len(parts) == 1 else jnp.concatenate(parts, axis=1)


def _stack(x):
    lane = _iota(x.shape, 1)
    return jnp.concatenate([jnp.where(lane < HEAD_DIM, x, 0.0), jnp.where(lane >= HEAD_DIM, x, 0.0)], axis=0)


def _unstack(xs):
    c = xs.shape[0] // 2
    return xs[:c] + xs[c:]


def _tri_masks(n, c):
    row, col = _iota((n, n), 0), _iota((n, n), 1)
    same = row // c == col // c
    return same & (col < row), same & (col <= row)


def _unit_lower_inverse(a, c):
    n = a.shape[0]
    row, col = _iota((n, n), 0), _iota((n, n), 1)
    eye = (row == col).astype(F32)
    same = lambda s: row // s == col // s
    a0 = jnp.where(same(SUBLANES), a, 0.0)
    a2 = _mm(a0, a0, P_INV)
    a4 = _mm(a2, a2, P_INV)
    x = _mm(_mm(eye - a0, eye + a2, P_INV), eye + a4, P_INV)
    s = SUBLANES
    while s < c:
        off = jnp.where(same(2 * s) & jnp.logical_not(same(s)), a, 0.0)
        x = x - _mm(_mm(x, off, P_INV), x, P_INV)
        s *= 2
    return x


def _adaln_kernel(c_ref, w_ref, b_ref, o_ref):
    o_ref[0] = _hdot(_silu(c_ref[...]), w_ref[0]) + b_ref[0]


def _adaln(c, w_ada, b_ada):
    depth, bsz, tn = w_ada.shape[0], c.shape[0], 768
    return pl.pallas_call(
        _adaln_kernel,
        grid=(depth, 6 * D_MODEL // tn),
        in_specs=[pl.BlockSpec((bsz, D_MODEL), lambda l, j: (0, 0)),
                  pl.BlockSpec((1, D_MODEL, tn), lambda l, j: (l, 0, j)),
                  pl.BlockSpec((1, 1, tn), lambda l, j: (l, 0, j))],
        out_specs=pl.BlockSpec((1, bsz, tn), lambda l, j: (l, 0, j)),
        out_shape=jax.ShapeDtypeStruct((depth, bsz, 6 * D_MODEL), F32),
        compiler_params=_params("arbitrary", "arbitrary"),
        name="adaln",
    )(c, w_ada, b_ada.reshape(depth, 1, 6 * D_MODEL))


def _norm_mod(x, norm_w, scale, shift):
    y = x * lax.rsqrt(jnp.mean(x * x, axis=-1, keepdims=True) + RMS_EPS)
    return y * norm_w * (1.0 + scale) + shift


def _inproj_kernel(x_ref, sh_ref, sc_ref, nw_ref, wa_ref, wb_ref, wc_ref, wab_ref, qn_ref, kn_ref,
                   pa_ref, q_ref, k_ref, v_ref, pc_ref, pab_ref):
    h = _norm_mod(x_ref[...], nw_ref[...], sc_ref[0], sh_ref[0]).astype(BF16)
    pa_ref[...] = _dot(h, wa_ref[...])
    pb = _dot(h, wb_ref[...])
    ones = _head_ones()

    def head_rms(y, w):
        return y * lax.rsqrt(_head_sum(y * y, ones) * (1.0 / HEAD_DIM) + RMS_EPS) * w

    q_ref[...] = head_rms(pb[:, :MIX], qn_ref[...])
    k_ref[...] = head_rms(pb[:, MIX:2 * MIX], kn_ref[...])
    v_ref[...] = pb[:, 2 * MIX:]
    pc_ref[...] = _dot(h, wc_ref[...])
    pab_ref[...] = _dot(h, wab_ref[...])


def _mod_spec(mod, tm, tiles_per_group):
    rows = mod.shape[1]
    return pl.BlockSpec((1, rows, D_MODEL), lambda i: (i // tiles_per_group, 0, 0))


def _const_spec(shape):
    zeros = (0,) * len(shape)
    return pl.BlockSpec(shape, lambda *_: zeros, pipeline_mode=pl.Buffered(1))


def _inproj(x, shift, scale, norm_w, wa, wb, wc, wab, q_norm, k_norm, tm, tiles_per_group):
    n = x.shape[0]
    row = lambda w: pl.BlockSpec((tm, w), lambda i: (i, 0))
    widths = (A_COLS, MIX, MIX, MIX, C_MAIN, LANES)
    return pl.pallas_call(
        _inproj_kernel,
        grid=(n // tm,),
        in_specs=[row(D_MODEL), _mod_spec(shift, tm, tiles_per_group), _mod_spec(scale, tm, tiles_per_group),
                  _const_spec((1, D_MODEL)), _const_spec(wa.shape), _const_spec(wb.shape), _const_spec(wc.shape),
                  _const_spec(wab.shape), _const_spec((1, MIX)), _const_spec((1, MIX))],
        out_specs=[row(w) for w in widths],
        out_shape=[jax.ShapeDtypeStruct((n, w), F32) for w in widths],
        compiler_params=_params("arbitrary"),
        name="inproj",
    )(x, shift, scale, norm_w, wa, wb, wc, wab, q_norm, k_norm)


def _rwkv_kernel(p_ref, shift0_ref, st0_ref, mu_ref, w0_ref, a0_ref, wwa_ref, gup_ref, kk_ref, ka_ref, rk_ref,
                 gnw_ref, gnb_ref, o_ref, st_out_ref, prev_scr, st_scr, *, c):
    n = 2 * c
    ci = pl.program_id(1)

    @pl.when(ci == 0)
    def _():
        prev_scr[...] = shift0_ref[0]
        st_scr[...] = st0_ref[0]

    p = p_ref[0]
    row = _iota((c, 1), 0)
    p_prev = jnp.where(row == 0, prev_scr[...], pltpu.roll(p, 1, axis=0))
    prev_scr[...] = p[c - 1:c, :]
    xs = p + (p_prev - p) * mu_ref[...]
    r, k, v = xs[:, :MIX], xs[:, MIX:2 * MIX], xs[:, 2 * MIX:3 * MIX]
    lora_in = xs[:, 3 * MIX:3 * MIX + LANES]
    gd = xs[:, 3 * MIX + LANES:]
    lane = _iota((1, LANES), 1)
    lora = _bdot(jnp.where(lane < HEAD_DIM, jnp.tanh(lora_in), lora_in), wwa_ref[...])
    log_w = -A_DECAY_SCALE * _sigmoid(w0_ref[...] + lora[:, :MIX])
    a = _sigmoid(a0_ref[...] + lora[:, MIX:])
    g = _bdot(_sigmoid(gd), gup_ref[...])
    ones = _head_ones()
    kk_raw = k * kk_ref[...]
    kk = kk_raw * lax.rsqrt(_head_sum(kk_raw * kk_raw, ones) + RMS_EPS)
    k = k * (1.0 + (a - 1.0) * ka_ref[...])
    b = kk * a
    tril = (_iota((c, c), 1) <= _iota((c, c), 0)).astype(F32)
    cum = _hdot(tril, log_w)
    cum_last = cum[c - 1:c, :]
    e_cum, e_neg = jnp.exp(cum), jnp.exp(-cum)
    e_prev, e_tail = jnp.exp(cum - log_w), jnp.exp(cum_last - cum)
    kk_t, r_t = kk * e_prev, r * e_cum
    k_h, b_h = k * e_neg, b * e_neg
    k_d, b_d = k * e_tail, b * e_tail
    p_last = jnp.exp(cum_last)
    strict, incl = _tri_masks(n, c)

    outs = []
    for j in range(N_PAIRS):
        sl = slice(j * LANES, (j + 1) * LANES)
        kk_s, r_s, v_s = _stack(kk_t[:, sl]), _stack(r_t[:, sl]), _stack(v[:, sl])
        amat = _mm(jnp.concatenate([kk_s, r_s], axis=0),
                   jnp.concatenate([_stack(k_h[:, sl]), _stack(b_h[:, sl])], axis=0), P_AMAT, nt=True)
        a_kk = jnp.where(strict, amat[:n, :n], 0.0)
        a_bk = jnp.where(strict, amat[:n, n:], 0.0)
        a_rk = jnp.where(incl, amat[n:, :n], 0.0)
        a_rb = jnp.where(incl, amat[n:, n:], 0.0)
        t_inv = _unit_lower_inverse(a_bk, c)
        av = _mm(jnp.concatenate([a_kk, a_rk], axis=0), v_s, P_SOLVE)
        wu = _mm(t_inv, jnp.concatenate([kk_s, av[:n]], axis=1), P_SOLVE)
        st = st_scr[j]
        zo = _mm(jnp.concatenate([wu[:, :LANES], r_s], axis=0), st, P_STATE)
        z = wu[:, LANES:] + zo[:n]
        outs.append(_unstack(zo[n:] + av[n:] - _mm(a_rb, z, P_STATE)))
        decay_col = jnp.transpose(jnp.broadcast_to(p_last[:, sl], (LANES, LANES)))
        st_scr[j] = (decay_col * st + _mm(jnp.transpose(_stack(k_d[:, sl])), v_s, P_STATE)
                     - _mm(jnp.transpose(_stack(b_d[:, sl])), z, P_STATE))
    o = jnp.concatenate(outs, axis=1)

    mean = _head_sum(o, ones) * (1.0 / HEAD_DIM)
    cen = o - mean
    var = _head_sum(cen * cen, ones) * (1.0 / HEAD_DIM)
    o = cen * lax.rsqrt(var + A_GN_EPS) * gnw_ref[...] + gnb_ref[...]
    bonus = _head_sum(r * k * rk_ref[...], ones) * v
    o_ref[0] = (o + bonus) * g
    st_out_ref[0] = st_scr[...]


def _rwkv(pa, shift0, st0, mu, w0, a0, wwa, gup, k_k, k_a, r_k, gn_w, gn_b, c):
    bsz, t, _ = pa.shape
    vec = lambda w: _const_spec((1, w))
    state = pl.BlockSpec((1, N_PAIRS, LANES, LANES), lambda b, i: (b, 0, 0, 0))
    return pl.pallas_call(
        functools.partial(_rwkv_kernel, c=c),
        grid=(bsz, t // c),
        in_specs=[pl.BlockSpec((1, c, A_COLS), lambda b, i: (b, i, 0)),
                  pl.BlockSpec((1, 1, A_COLS), lambda b, i: (b, 0, 0)), state,
                  vec(A_COLS), vec(MIX), vec(MIX), _const_spec(wwa.shape), _const_spec(gup.shape),
                  vec(MIX), vec(MIX), vec(MIX), vec(MIX), vec(MIX)],
        out_specs=[pl.BlockSpec((1, c, MIX), lambda b, i: (b, i, 0)), state],
        out_shape=[jax.ShapeDtypeStruct((bsz, t, MIX), F32),
                   jax.ShapeDtypeStruct((bsz, N_PAIRS, LANES, LANES), F32)],
        scratch_shapes=[pltpu.VMEM((1, A_COLS), F32), pltpu.VMEM((N_PAIRS, LANES, LANES), F32)],
        compiler_params=_params("arbitrary", "arbitrary"),
        name="rwkv",
    )(pa, shift0, st0, mu, w0, a0, wwa, gup, k_k, k_a, r_k, gn_w, gn_b)


def _gdn_kernel(pc_ref, pab_ref, conv0_ref, st0_ref, convw_ref, alog_ref, dtb_ref, nw_ref, eab_ref,
                o_ref, st_out_ref, ext_scr, st_scr, *, c):
    n = 2 * c
    w3 = 3 * MIX
    ci = pl.program_id(1)

    @pl.when(ci == 0)
    def _():
        ext_scr[0:SUBLANES, :] = conv0_ref[0]
        st_scr[...] = st0_ref[0]

    ext_scr[SUBLANES:SUBLANES + c, :] = pc_ref[0, :, :w3]
    conv = ext_scr[SUBLANES:SUBLANES + c, :] * convw_ref[CONV_W - 1:CONV_W, :]
    for s in range(1, CONV_W):
        conv = conv + ext_scr[SUBLANES - s:SUBLANES - s + c, :] * convw_ref[CONV_W - 1 - s:CONV_W - s, :]
    ext_scr[0:SUBLANES, :] = ext_scr[c:c + SUBLANES, :]
    qkv = _silu(conv)
    ones = _head_ones()
    l2n = lambda y: y * lax.rsqrt(_head_sum(y * y, ones) + RMS_EPS)
    q = l2n(qkv[:, :MIX]) * (HEAD_DIM ** -0.5)
    k = l2n(qkv[:, MIX:2 * MIX])
    v = qkv[:, 2 * MIX:]
    zgate = pc_ref[0, :, w3:]
    ab = _hdot(pab_ref[0], eab_ref[...])
    log_alpha = -jnp.exp(alog_ref[...]) * _softplus(ab[:, :MIX] + dtb_ref[...])
    beta = _sigmoid(ab[:, MIX:])
    tril = (_iota((c, c), 1) <= _iota((c, c), 0)).astype(F32)
    gcum = _hdot(tril, log_alpha)
    g_last = gcum[c - 1:c, :]
    eg = jnp.exp(gcum)
    bk = beta * k
    rhs_u, rhs_w = beta * v, bk * eg
    qg = q * eg
    kd = k * jnp.exp(g_last - gcum)
    strict, incl = _tri_masks(n, c)

    outs = []
    for j in range(N_PAIRS):
        sl = slice(j * LANES, (j + 1) * LANES)
        g_s = _stack(gcum[:, sl])
        g_col = g_s + pltpu.roll(g_s, HEAD_DIM, axis=1)
        g_row = jnp.transpose(g_col)[:n, :]
        diff = g_col[:, :n] - g_row
        dec_s = jnp.where(strict, jnp.exp(jnp.where(strict, diff, 0.0)), 0.0)
        dec_i = jnp.where(incl, jnp.exp(jnp.where(incl, diff, 0.0)), 0.0)
        k_s = _stack(k[:, sl])
        kmat = _mm(jnp.concatenate([_stack(bk[:, sl]), _stack(q[:, sl])], axis=0), k_s, P_AMAT, nt=True)
        lmat = kmat[:n] * dec_s
        attn = kmat[n:] * dec_i
        t_inv = _unit_lower_inverse(lmat, c)
        sol = _mm(t_inv, jnp.concatenate([_stack(rhs_u[:, sl]), _stack(rhs_w[:, sl])], axis=1), P_SOLVE)
        st = st_scr[j]
        wq = _mm(jnp.concatenate([sol[:, LANES:], _stack(qg[:, sl])], axis=0), st, P_STATE)
        delta = sol[:, :LANES] - wq[:n]
        outs.append(_unstack(wq[n:] + _mm(attn, delta, P_STATE)))
        decay_col = jnp.transpose(jnp.broadcast_to(jnp.exp(g_last[:, sl]), (LANES, LANES)))
        st_scr[j] = decay_col * st + _mm(jnp.transpose(_stack(kd[:, sl])), delta, P_STATE)
    o = jnp.concatenate(outs, axis=1)

    o = o * lax.rsqrt(_head_sum(o * o, ones) * (1.0 / HEAD_DIM) + RMS_EPS) * nw_ref[...]
    o_ref[0] = o * _silu(zgate)
    st_out_ref[0] = st_scr[...]


def _gdn(pc, pab, conv0, st0, conv_w, a_log, dt_bias, norm_w, eab, c):
    bsz, t, _ = pc.shape
    vec = lambda w: _const_spec((1, w))
    state = pl.BlockSpec((1, N_PAIRS, LANES, LANES), lambda b, i: (b, 0, 0, 0))
    return pl.pallas_call(
        functools.partial(_gdn_kernel, c=c),
        grid=(bsz, t // c),
        in_specs=[pl.BlockSpec((1, c, C_MAIN), lambda b, i: (b, i, 0)),
                  pl.BlockSpec((1, c, LANES), lambda b, i: (b, i, 0)),
                  pl.BlockSpec((1, SUBLANES, 3 * MIX), lambda b, i: (b, 0, 0)), state,
                  _const_spec((CONV_W, 3 * MIX)), vec(MIX), vec(MIX), vec(MIX), _const_spec(eab.shape)],
        out_specs=[pl.BlockSpec((1, c, MIX), lambda b, i: (b, i, 0)), state],
        out_shape=[jax.ShapeDtypeStruct((bsz, t, MIX), F32),
                   jax.ShapeDtypeStruct((bsz, N_PAIRS, LANES, LANES), F32)],
        scratch_shapes=[pltpu.VMEM((c + SUBLANES, 3 * MIX), F32), pltpu.VMEM((N_PAIRS, LANES, LANES), F32)],
        compiler_params=_params("arbitrary", "arbitrary"),
        name="gdn",
    )(pc, pab, conv0, st0, conv_w, a_log, dt_bias, norm_w, eab)


def _band_kernel(q_ref, k_ref, v_ref, bias_ref, o_ref, *, cq, wn, masked):
    ci = pl.program_id(1)
    start = pl.multiple_of(ci * cq, cq)
    q = q_ref[0]
    kw = k_ref[0, pl.ds(start, wn), :]
    vw = v_ref[0, pl.ds(start, wn), :]
    lane = _iota((1, LANES), 1)
    if masked:
        valid = _iota((1, wn), 1) >= BAND - ci * cq
    outs = []
    for j in range(N_PAIRS):
        sl = slice(j * LANES, (j + 1) * LANES)
        qp, kp, vp = q[:, sl], kw[:, sl], vw[:, sl]
        acc = jnp.zeros((cq, LANES), F32)
        for hh in range(2):
            m = (lane < HEAD_DIM) if hh == 0 else (lane >= HEAD_DIM)
            s = _dot_nt(jnp.where(m, qp, 0.0).astype(BF16), kp) * (HEAD_DIM ** -0.5) + bias_ref[2 * j + hh]
            if masked:
                s = jnp.where(valid, s, -1e30)
            e = jnp.exp(s - jnp.max(s, axis=-1, keepdims=True))
            pv = _dot(e.astype(BF16), jnp.where(m, vp, jnp.zeros_like(vp)))
            acc = acc + pv / jnp.sum(e, axis=-1, keepdims=True)
        outs.append(acc)
    o_ref[0] = jnp.concatenate(outs, axis=1)


def _band(q, k_ext, v_ext, bias, cq, masked):
    bsz, t, _ = q.shape
    text = k_ext.shape[1]
    wn = text - t + cq
    ext = pl.BlockSpec((1, text, MIX), lambda b, i: (b, 0, 0))
    return pl.pallas_call(
        functools.partial(_band_kernel, cq=cq, wn=wn, masked=masked),
        grid=(bsz, t // cq),
        in_specs=[pl.BlockSpec((1, cq, MIX), lambda b, i: (b, i, 0)), ext, ext, _const_spec(bias.shape)],
        out_specs=pl.BlockSpec((1, cq, MIX), lambda b, i: (b, i, 0)),
        out_shape=jax.ShapeDtypeStruct((bsz, t, MIX), F32),
        compiler_params=_params("arbitrary", "arbitrary"),
        name="band",
    )(q, k_ext, v_ext, bias)


def _merge_kernel(x_ref, oa_ref, ob_ref, oc_ref, sh_ref, sc_ref, gm_ref, sh2_ref, sc2_ref, nw_ref, nw2_ref,
                  wgate_ref, bgate_ref, wbr_ref, wout_ref, wrt_ref, rb_ref,
                  x1_ref, h2_ref, comb_ref):
    x = x_ref[...]
    h = _norm_mod(x, nw_ref[...], sc_ref[0], sh_ref[0]).astype(BF16)
    gates = _sigmoid(_dot(h, wgate_ref[...]) + bgate_ref[...])
    mixed = None
    for i, o_ref in enumerate((oa_ref, ob_ref, oc_ref)):
        term = gates[:, i * D_MODEL:(i + 1) * D_MODEL] * _bdot(o_ref[...], wbr_ref[i])
        mixed = term if mixed is None else mixed + term
    x1 = x + gm_ref[0] * _bdot(mixed, wout_ref[...])
    x1_ref[...] = x1
    h2 = _norm_mod(x1, nw2_ref[...], sc2_ref[0], sh2_ref[0])
    h2_ref[...] = h2.astype(BF16)

    scores = _sigmoid(_dot_nt(wrt_ref[...], h2, HIGHEST))
    sel = scores + rb_ref[...]
    tm = scores.shape[1]
    per = N_EXPERTS // N_GROUPS
    best_val, best = None, None
    for g in range(N_GROUPS):
        rows = [sel[g * per + i:g * per + i + 1, :] for i in range(per)]
        top2 = None
        for i in range(per):
            for i2 in range(i + 1, per):
                pair = rows[i] + rows[i2]
                top2 = pair if top2 is None else jnp.maximum(top2, pair)
        if g == 0:
            best_val, best = top2, jnp.zeros((1, tm), jnp.int32)
        else:
            better = top2 > best_val
            best = jnp.where(better, g, best)
            best_val = jnp.where(better, top2, best_val)
    eidx = _iota((N_EXPERTS, tm), 0)
    cand = jnp.where(eidx // per == best, sel, -jnp.inf)
    m1 = jnp.max(cand, axis=0, keepdims=True)
    i1 = jnp.min(jnp.where(cand == m1, eidx, N_EXPERTS), axis=0, keepdims=True)
    cand2 = jnp.where(eidx == i1, -jnp.inf, cand)
    m2 = jnp.max(cand2, axis=0, keepdims=True)
    i2 = jnp.min(jnp.where(cand2 == m2, eidx, N_EXPERTS), axis=0, keepdims=True)
    w1 = jnp.sum(jnp.where(eidx == i1, scores, 0.0), axis=0, keepdims=True)
    w2 = jnp.sum(jnp.where(eidx == i2, scores, 0.0), axis=0, keepdims=True)
    den = w1 + w2
    comb_ref[...] = jnp.where(eidx == i1, w1 / den, 0.0) + jnp.where(eidx == i2, w2 / den, 0.0)


def _merge(x, oa, ob, oc, mods, norm_w, norm2_w, w_gate, b_gate, w_branch, w_out, w_router_t, router_bias,
           tm, tiles_per_group):
    n = x.shape[0]
    row = lambda w: pl.BlockSpec((tm, w), lambda i: (i, 0))
    mod_specs = [_mod_spec(m, tm, tiles_per_group) for m in mods]
    return pl.pallas_call(
        _merge_kernel,
        grid=(n // tm,),
        in_specs=[row(D_MODEL), row(MIX), row(MIX), row(MIX)] + mod_specs
                 + [_const_spec((1, D_MODEL)), _const_spec((1, D_MODEL)), _const_spec(w_gate.shape),
                    _const_spec(b_gate.shape), _const_spec(w_branch.shape), _const_spec(w_out.shape),
                    _const_spec(w_router_t.shape), _const_spec(router_bias.shape)],
        out_specs=[row(D_MODEL), row(D_MODEL), pl.BlockSpec((N_EXPERTS, tm), lambda i: (0, i))],
        out_shape=[jax.ShapeDtypeStruct((n, D_MODEL), F32), jax.ShapeDtypeStruct((n, D_MODEL), BF16),
                   jax.ShapeDtypeStruct((N_EXPERTS, n), F32)],
        compiler_params=_params("arbitrary"),
        name="merge",
    )(x, oa, ob, oc, *mods, norm_w, norm2_w, w_gate, b_gate, w_branch, w_out, w_router_t, router_bias)


def _moe_kernel(x1_ref, h2_ref, comb_ref, gf_ref, wg_ref, wu_ref, wd_ref, o_ref):
    h2 = h2_ref[...]
    comb = jnp.transpose(comb_ref[...])
    acc = jnp.zeros(o_ref.shape, F32)
    for e in range(N_EXPERTS):
        hid = _silu(_dot(h2, wg_ref[e])) * _dot(h2, wu_ref[e]) * comb[:, e:e + 1]
        acc = acc + _dot(hid.astype(BF16), wd_ref[e])
    o_ref[...] = x1_ref[...] + gf_ref[0] * acc


def _moe(x1, h2, comb, g_ffn, wg, wu, wd, tm, tiles_per_group):
    n = x1.shape[0]
    row = lambda w: pl.BlockSpec((tm, w), lambda i: (i, 0))
    return pl.pallas_call(
        _moe_kernel,
        grid=(n // tm,),
        in_specs=[row(D_MODEL), row(D_MODEL), pl.BlockSpec((N_EXPERTS, tm), lambda i: (0, i)),
                  _mod_spec(g_ffn, tm, tiles_per_group),
                  _const_spec(wg.shape), _const_spec(wu.shape), _const_spec(wd.shape)],
        out_specs=row(D_MODEL),
        out_shape=jax.ShapeDtypeStruct((n, D_MODEL), F32),
        compiler_params=_params("arbitrary"),
        name="moe",
    )(x1, h2, comb, g_ffn, wg, wu, wd)


def _pair_state(s):
    bsz = s.shape[0]
    s = s.reshape(bsz, N_PAIRS, 2, HEAD_DIM, HEAD_DIM)
    z = jnp.zeros_like(s[:, :, 0])
    top = jnp.concatenate([s[:, :, 0], z], axis=-1)
    bot = jnp.concatenate([z, s[:, :, 1]], axis=-1)
    return jnp.concatenate([top, bot], axis=-2)


def _unpair_state(s):
    bsz = s.shape[0]
    return jnp.stack([s[:, :, :HEAD_DIM, :HEAD_DIM], s[:, :, HEAD_DIM:, HEAD_DIM:]], axis=2).reshape(
        bsz, N_HEADS, HEAD_DIM, HEAD_DIM)


def _rel_bias(table, cq, wn):
    dist = jnp.arange(cq)[:, None] + BAND - jnp.arange(wn)[None, :]
    return table[:, jnp.clip(dist, -MAX_REL, MAX_REL) + MAX_REL].astype(F32)


def _layer(x, mod, state, wts, tm):
    bsz, t, _ = x.shape
    n = bsz * t
    c = min(CHUNK, t)
    shift0, wkv0, k_past, v_past, conv0, s0 = state
    if t % tm == 0:
        tiles_per_group = t // tm
        mods = [m.reshape(bsz, 1, D_MODEL) for m in jnp.split(mod, 6, axis=-1)]
    else:
        tiles_per_group = 1
        mods = [jnp.repeat(m, t, axis=0).reshape(n // tm, tm, D_MODEL) for m in jnp.split(mod, 6, axis=-1)]
    sh_mix, sc_mix, g_mix, sh_ffn, sc_ffn, g_ffn = mods
    xf = x.reshape(n, D_MODEL)

    pa, q, k, v, pc, pab = _inproj(xf, sh_mix, sc_mix, wts["norm_mix_w"], wts["wa"], wts["wb"], wts["wc"],
                                   wts["wab"], wts["q_norm"], wts["k_norm"], tm, tiles_per_group)
    pa = pa.reshape(bsz, t, A_COLS)
    q, k, v = (z.reshape(bsz, t, MIX) for z in (q, k, v))
    pc = pc.reshape(bsz, t, C_MAIN)

    oa, wkv = _rwkv(pa, shift0.reshape(bsz, 1, A_COLS), _pair_state(jnp.swapaxes(wkv0, -1, -2)),
                    wts["mu"], wts["w0"], wts["a0"], wts["wwa"], wts["g_up"], wts["k_k"], wts["k_a"], wts["r_k"],
                    wts["gn_w"], wts["gn_b"], c)
    new_wkv = jnp.swapaxes(_unpair_state(wkv), -1, -2)

    if k_past is None:
        pad = lambda z: jnp.pad(z.astype(BF16), ((0, 0), (BAND, 0), (0, 0)))
        ob = _band(q, pad(k), pad(v), _rel_bias(wts["rel_bias"], CHUNK, BAND + CHUNK), CHUNK, True)
        new_k, new_v = k[:, -BAND:], v[:, -BAND:]
    else:
        past = k_past.shape[1]
        cat = lambda zp, z: jnp.concatenate([zp.reshape(bsz, past, MIX), z], axis=1).astype(BF16)
        ob = _band(q, cat(k_past, k), cat(v_past, v), _rel_bias(wts["rel_bias"], t, past + t), t, False)
        new_k, new_v = k, v

    conv_pad = jnp.pad(conv0, ((0, 0), (SUBLANES - (CONV_W - 1), 0), (0, 0)))
    oc, s_new = _gdn(pc, pab.reshape(bsz, t, LANES), conv_pad, _pair_state(s0), wts["conv_w"], wts["a_log"],
                     wts["dt_bias"], wts["gdn_norm_w"], wts["eab"], c)
    new_conv = jnp.concatenate([conv0, pc[:, :, :3 * MIX]], axis=1)[:, -(CONV_W - 1):]

    x1, h2, comb = _merge(xf, oa.reshape(n, MIX), ob.reshape(n, MIX), oc.reshape(n, MIX),
                          [sh_mix, sc_mix, g_mix, sh_ffn, sc_ffn], wts["norm_mix_w"], wts["norm_ffn_w"],
                          wts["w_gate"], wts["b_gate"], wts["w_branch"], wts["w_out"], wts["w_router_t"],
                          wts["router_bias"], tm, tiles_per_group)
    x2 = _moe(x1, h2, comb, g_ffn, wts["wg"], wts["wu"], wts["wd"], tm, tiles_per_group)
    heads = lambda z: z.reshape(bsz, -1, N_HEADS, HEAD_DIM)
    return x2.reshape(bsz, t, D_MODEL), (pa[:, -1], new_wkv, heads(new_k), heads(new_v), new_conv,
                                         _unpair_state(s_new))


def _prepare_layer(l, w_in, norm_mix_w, norm_ffn_w, rwkv_mu, rwkv_w0, rwkv_w_up, rwkv_a0, rwkv_a_up, rwkv_g_up,
                   rwkv_k_k, rwkv_k_a, rwkv_r_k, rwkv_gn_w, rwkv_gn_b, band_q_norm, band_k_norm, band_rel_bias,
                   gdn_conv_w, gdn_a_log, gdn_dt_bias, gdn_norm_w, w_branch, w_gate, b_gate, w_out,
                   w_router, router_bias, w_exp_gate, w_exp_up, w_exp_down):
    row = lambda z: z.reshape(1, -1).astype(F32)
    per_head = lambda z: jnp.repeat(z, HEAD_DIM).reshape(1, MIX)
    win = w_in[l].astype(BF16)
    c0 = A_COLS + B_COLS
    wab = jnp.pad(win[:, c0 + C_MAIN:], ((0, 0), (0, LANES - 2 * N_HEADS)))
    zeros = jnp.zeros((HEAD_DIM, MIX), F32)
    wwa = jnp.concatenate([jnp.concatenate([rwkv_w_up[l], zeros], axis=1),
                           jnp.concatenate([zeros, rwkv_a_up[l]], axis=1)], axis=0)
    head_of_lane = jnp.arange(MIX) // HEAD_DIM
    src = jnp.arange(LANES)[:, None]
    eab = jnp.concatenate([(src == head_of_lane[None, :]), (src == N_HEADS + head_of_lane[None, :])],
                          axis=1).astype(F32)
    return dict(
        norm_mix_w=row(norm_mix_w[l]), norm_ffn_w=row(norm_ffn_w[l]),
        wa=win[:, :A_COLS], wb=win[:, A_COLS:c0], wc=win[:, c0:c0 + C_MAIN], wab=wab,
        q_norm=row(jnp.tile(band_q_norm[l], N_HEADS)), k_norm=row(jnp.tile(band_k_norm[l], N_HEADS)),
        mu=row(rwkv_mu[l]), w0=row(rwkv_w0[l]), a0=row(rwkv_a0[l]), wwa=wwa, g_up=rwkv_g_up[l],
        k_k=row(rwkv_k_k[l]), k_a=row(rwkv_k_a[l]), r_k=row(rwkv_r_k[l]), gn_w=row(rwkv_gn_w[l]),
        gn_b=row(rwkv_gn_b[l]), rel_bias=band_rel_bias[l],
        conv_w=gdn_conv_w[l], a_log=per_head(gdn_a_log[l]), dt_bias=per_head(gdn_dt_bias[l]),
        gdn_norm_w=row(jnp.tile(gdn_norm_w[l], N_HEADS)), eab=eab,
        w_gate=w_gate[l].astype(BF16), b_gate=row(b_gate[l]), w_branch=w_branch[l].astype(BF16),
        w_out=w_out[l].astype(BF16), w_router_t=jnp.transpose(w_router), router_bias=router_bias.reshape(-1, 1),
        wg=w_exp_gate[l].astype(BF16), wu=w_exp_up[l].astype(BF16), wd=w_exp_down[l].astype(BF16),
    )


def kernel(x_prompt, x_sample, c_prompt, c_sample, state_rwkv_shift, state_rwkv_wkv, cache_band_k, cache_band_v, state_gdn_conv, state_gdn_S, w_ada, b_ada, norm_mix_w, norm_ffn_w, w_in, rwkv_mu, rwkv_w0, rwkv_w_up, rwkv_a0, rwkv_a_up, rwkv_g_up, rwkv_k_k, rwkv_k_a, rwkv_r_k, rwkv_gn_w, rwkv_gn_b, band_q_norm, band_k_norm, band_rel_bias, gdn_conv_w, gdn_a_log, gdn_dt_bias, gdn_norm_w, w_branch, w_gate, b_gate, w_out, w_router, router_bias, w_exp_gate, w_exp_up, w_exp_down):
    depth = w_ada.shape[0]
    bsz = x_prompt.shape[0]
    mod_p = _adaln(c_prompt, w_ada, b_ada)
    mod_s = _adaln(c_sample, w_ada, b_ada)
    zero_state = (jnp.zeros((bsz, A_COLS), F32), jnp.zeros((bsz, N_HEADS, HEAD_DIM, HEAD_DIM), F32), None, None,
                  jnp.zeros((bsz, CONV_W - 1, 3 * MIX), F32), jnp.zeros((bsz, N_HEADS, HEAD_DIM, HEAD_DIM), F32))
    tm = 256
    xp, xs = x_prompt, x_sample
    new_p, new_s = [], []
    for l in range(depth):
        wts = _prepare_layer(l, w_in, norm_mix_w, norm_ffn_w, rwkv_mu, rwkv_w0, rwkv_w_up, rwkv_a0, rwkv_a_up,
                             rwkv_g_up, rwkv_k_k, rwkv_k_a, rwkv_r_k, rwkv_gn_w, rwkv_gn_b, band_q_norm,
                             band_k_norm, band_rel_bias, gdn_conv_w, gdn_a_log, gdn_dt_bias, gdn_norm_w,
                             w_branch, w_gate, b_gate, w_out, w_router, router_bias, w_exp_gate, w_exp_up,
                             w_exp_down)
        xp, st_p = _layer(xp, mod_p[l], zero_state, wts, tm)
        xs, st_s = _layer(xs, mod_s[l], (state_rwkv_shift[l], state_rwkv_wkv[l], cache_band_k[l], cache_band_v[l],
                                         state_gdn_conv[l], state_gdn_S[l]), wts, tm)
        new_p.append(st_p)
        new_s.append(st_s)
    p_out = [jnp.stack(z, axis=0) for z in zip(*new_p)]
    s_out = [jnp.stack(z, axis=0) for z in zip(*new_s)]
    return (xp, xs, *p_out, *s_out)
```

```python
import functools
import math

import jax
import jax.numpy as jnp
from jax import lax
from jax.experimental import pallas as pl
from jax.experimental.pallas import tpu as pltpu

F32 = jnp.float32
BF16 = jnp.bfloat16
HIGHEST = lax.Precision.HIGHEST

D_MODEL = 1024
MIX = 512
HEAD_DIM = 64
N_HEADS = MIX // HEAD_DIM
LANES = 128
N_PAIRS = MIX // LANES
SUBLANES = 8
CHUNK = 64
MIXER_ROWS = 256
A_COLS = 3 * MIX + 64 + 64 + 128
B_COLS = 3 * MIX
C_MAIN = 4 * MIX
CONV_W = 4
BAND = 8 * CHUNK
MAX_REL = 2 * CHUNK
N_EXPERTS = 16
N_GROUPS = 4
EXPERT_DIM = D_MODEL // 4
RMS_EPS = 1e-6
A_GN_EPS = 64e-5
A_DECAY_SCALE = math.exp(-0.5)
VMEM_LIMIT_BYTES = 56 * 1024 * 1024


def _params(*sem):
    return pltpu.CompilerParams(dimension_semantics=sem, vmem_limit_bytes=VMEM_LIMIT_BYTES)


def _dot(a, b, precision=None):
    return lax.dot_general(a, b, (((1,), (0,)), ((), ())), precision=precision, preferred_element_type=F32)


def _dot_nt(a, b, precision=None):
    return lax.dot_general(a, b, (((1,), (1,)), ((), ())), precision=precision, preferred_element_type=F32)


def _bdot(a, b):
    return _dot(a.astype(BF16), b.astype(BF16))


def _bdot_nt(a, b):
    return _dot_nt(a.astype(BF16), b.astype(BF16))


def _hdot(a, b):
    return _dot(a, b, HIGHEST)


def _bmm(a, b):
    return lax.dot_general(a.astype(BF16), b.astype(BF16), (((2,), (1,)), ((0,), (0,))),
                           preferred_element_type=F32)


def _bmm_nt(a, b):
    return lax.dot_general(a.astype(BF16), b.astype(BF16), (((2,), (2,)), ((0,), (0,))),
                           preferred_element_type=F32)


def _sigmoid(x):
    return 1.0 / (1.0 + jnp.exp(-x))


def _silu(x):
    return x * _sigmoid(x)


def _softplus(x):
    return jnp.maximum(x, 0.0) + jnp.log1p(jnp.exp(-jnp.abs(x)))


def _iota(shape, dim):
    return lax.broadcasted_iota(jnp.int32, shape, dim)


def _head_ones():
    return (_iota((LANES, LANES), 0) // HEAD_DIM == _iota((LANES, LANES), 1) // HEAD_DIM).astype(F32)


def _head_sum(x, ones):
    parts = [_hdot(x[:, g * LANES:(g + 1) * LANES], ones) for g in range(x.shape[1] // LANES)]
    return parts[0] if len(parts) == 1 else jnp.concatenate(parts, axis=1)


def _stack(x):
    lane = _iota(x.shape, 1)
    return jnp.concatenate([jnp.where(lane < HEAD_DIM, x, 0.0), jnp.where(lane >= HEAD_DIM, x, 0.0)], axis=0)


def _group(x, c):
    return jnp.stack([_stack(x[u * c:(u + 1) * c, j * LANES:(j + 1) * LANES])
                      for u in range(x.shape[0] // c) for j in range(N_PAIRS)])


def _ungroup(xs, c):
    return jnp.concatenate(
        [jnp.concatenate([xs[u * N_PAIRS + j, :c] + xs[u * N_PAIRS + j, c:] for j in range(N_PAIRS)], axis=1)
         for u in range(xs.shape[0] // N_PAIRS)], axis=0)


def _transpose_tiles(xs):
    return jnp.stack([jnp.transpose(xs[g]) for g in range(xs.shape[0])])


def _chunk_last(x, c):
    return jnp.concatenate([jnp.broadcast_to(x[(u + 1) * c - 1:(u + 1) * c], (c, x.shape[1]))
                            for u in range(x.shape[0] // c)], axis=0)


def _column_scale(row_values, c):
    tiles = []
    for u in range(row_values.shape[0] // c):
        last = row_values[(u + 1) * c - 1:(u + 1) * c]
        for j in range(N_PAIRS):
            tiles.append(jnp.transpose(jnp.broadcast_to(last[:, j * LANES:(j + 1) * LANES], (LANES, LANES))))
    return jnp.stack(tiles)


def _chunk_tril(rows, c):
    row, col = _iota((rows, rows), 0), _iota((rows, rows), 1)
    return ((col <= row) & (row // c == col // c)).astype(F32)


def _tri_masks(n, c):
    row, col = _iota((n, n), 0), _iota((n, n), 1)
    same = row // c == col // c
    return same & (col < row), same & (col <= row)


def _unit_lower_inverse(a, c):
    n = a.shape[1]
    row, col = _iota((n, n), 0), _iota((n, n), 1)
    eye = (row == col).astype(F32)
    same = lambda s: row // s == col // s
    a0 = jnp.where(same(SUBLANES), a, 0.0)
    a2 = _bmm(a0, a0)
    a4 = _bmm(a2, a2)
    x = _bmm(_bmm(eye - a0, eye + a2), eye + a4)
    s = SUBLANES
    while s < c:
        off = jnp.where(same(2 * s) & jnp.logical_not(same(s)), a, 0.0)
        x = x - _bmm(_bmm(x, off), x)
        s *= 2
    return x


def _adaln_kernel(c_ref, w_ref, b_ref, o_ref):
    o_ref[0] = _hdot(_silu(c_ref[...]), w_ref[0]) + b_ref[0]


def _adaln(c, w_ada, b_ada):
    depth, bsz, tn = w_ada.shape[0], c.shape[0], 768
    return pl.pallas_call(
        _adaln_kernel,
        grid=(depth, 6 * D_MODEL // tn),
        in_specs=[pl.BlockSpec((bsz, D_MODEL), lambda l, j: (0, 0)),
                  pl.BlockSpec((1, D_MODEL, tn), lambda l, j: (l, 0, j)),
                  pl.BlockSpec((1, 1, tn), lambda l, j: (l, 0, j))],
        out_specs=pl.BlockSpec((1, bsz, tn), lambda l, j: (l, 0, j)),
        out_shape=jax.ShapeDtypeStruct((depth, bsz, 6 * D_MODEL), F32),
        compiler_params=_params("arbitrary", "arbitrary"),
        name="adaln",
    )(c, w_ada, b_ada.reshape(depth, 1, 6 * D_MODEL))


def _norm_mod(x, norm_w, scale, shift):
    y = x * lax.rsqrt(jnp.mean(x * x, axis=-1, keepdims=True) + RMS_EPS)
    return y * norm_w * (1.0 + scale) + shift


def _inproj_kernel(x_ref, sh_ref, sc_ref, nw_ref, wa_ref, wb_ref, wc_ref, wab_ref, qn_ref, kn_ref,
                   pa_ref, q_ref, k_ref, v_ref, pc_ref, pab_ref):
    h = _norm_mod(x_ref[...], nw_ref[...], sc_ref[0], sh_ref[0]).astype(BF16)
    pa_ref[...] = _dot(h, wa_ref[...])
    pb = _dot(h, wb_ref[...])
    ones = _head_ones()

    def head_rms(y, w):
        return y * lax.rsqrt(_head_sum(y * y, ones) * (1.0 / HEAD_DIM) + RMS_EPS) * w

    q_ref[...] = head_rms(pb[:, :MIX], qn_ref[...])
    k_ref[...] = head_rms(pb[:, MIX:2 * MIX], kn_ref[...])
    v_ref[...] = pb[:, 2 * MIX:]
    pc_ref[...] = _dot(h, wc_ref[...])
    pab_ref[...] = _dot(h, wab_ref[...])


def _mod_spec(mod, tm, tiles_per_group):
    rows = mod.shape[1]
    return pl.BlockSpec((1, rows, D_MODEL), lambda i: (i // tiles_per_group, 0, 0))


def _const_spec(shape):
    zeros = (0,) * len(shape)
    return pl.BlockSpec(shape, lambda *_: zeros, pipeline_mode=pl.Buffered(1))


def _inproj(x, shift, scale, norm_w, wa, wb, wc, wab, q_norm, k_norm, tm, tiles_per_group):
    n = x.shape[0]
    row = lambda w: pl.BlockSpec((tm, w), lambda i: (i, 0))
    widths = (A_COLS, MIX, MIX, MIX, C_MAIN, LANES)
    return pl.pallas_call(
        _inproj_kernel,
        grid=(n // tm,),
        in_specs=[row(D_MODEL), _mod_spec(shift, tm, tiles_per_group), _mod_spec(scale, tm, tiles_per_group),
                  _const_spec((1, D_MODEL)), _const_spec(wa.shape), _const_spec(wb.shape), _const_spec(wc.shape),
                  _const_spec(wab.shape), _const_spec((1, MIX)), _const_spec((1, MIX))],
        out_specs=[row(w) for w in widths],
        out_shape=[jax.ShapeDtypeStruct((n, w), F32) for w in widths],
        compiler_params=_params("arbitrary"),
        name="inproj",
    )(x, shift, scale, norm_w, wa, wb, wc, wab, q_norm, k_norm)


def _rwkv_kernel(p_ref, shift0_ref, st0_ref, mu_ref, w0_ref, a0_ref, wwa_ref, gup_ref, kk_ref, ka_ref, rk_ref,
                 gnw_ref, gnb_ref, o_ref, st_out_ref, ext_scr, st_scr, *, c):
    bb, tt, _ = p_ref.shape
    rows, nch, n = bb * tt, tt // c, 2 * c
    assert bb == 1 or nch == 1
    ci = pl.program_id(1)

    @pl.when(ci == 0)
    def _():
        ext_scr[:, SUBLANES - 1:SUBLANES, :] = shift0_ref[...]
        st_scr[...] = st0_ref[...].reshape(bb * N_PAIRS, LANES, LANES)

    ext_scr[:, SUBLANES:SUBLANES + tt, :] = p_ref[...]
    p = p_ref[...].reshape(rows, A_COLS)
    p_prev = ext_scr[:, SUBLANES - 1:SUBLANES - 1 + tt, :].reshape(rows, A_COLS)
    ext_scr[:, 0:SUBLANES, :] = ext_scr[:, tt:tt + SUBLANES, :]
    xs = p + (p_prev - p) * mu_ref[...]
    r, k, v = xs[:, :MIX], xs[:, MIX:2 * MIX], xs[:, 2 * MIX:3 * MIX]
    lora_in = xs[:, 3 * MIX:3 * MIX + LANES]
    gd = xs[:, 3 * MIX + LANES:]
    lane = _iota((1, LANES), 1)
    lora = _bdot(jnp.where(lane < HEAD_DIM, jnp.tanh(lora_in), lora_in), wwa_ref[...])
    log_w = -A_DECAY_SCALE * _sigmoid(w0_ref[...] + lora[:, :MIX])
    a = _sigmoid(a0_ref[...] + lora[:, MIX:])
    g = _bdot(_sigmoid(gd), gup_ref[...])
    ones = _head_ones()
    kk_raw = k * kk_ref[...]
    kk = kk_raw * lax.rsqrt(_head_sum(kk_raw * kk_raw, ones) + RMS_EPS)
    k = k * (1.0 + (a - 1.0) * ka_ref[...])
    b = kk * a
    cum = _hdot(_chunk_tril(rows, c), log_w)
    cum_last = _chunk_last(cum, c)
    e_cum, e_neg = jnp.exp(cum), jnp.exp(-cum)
    e_prev, e_tail = jnp.exp(cum - log_w), jnp.exp(cum_last - cum)
    strict, incl = _tri_masks(n, c)

    kk_g, r_g, v_g = _group(kk * e_prev, c), _group(r * e_cum, c), _group(v, c)
    amat = _bmm_nt(jnp.concatenate([kk_g, r_g], axis=1),
                   jnp.concatenate([_group(k * e_neg, c), _group(b * e_neg, c)], axis=1))
    a_kk = jnp.where(strict, amat[:, :n, :n], 0.0)
    a_bk = jnp.where(strict, amat[:, :n, n:], 0.0)
    a_rk = jnp.where(incl, amat[:, n:, :n], 0.0)
    a_rb = jnp.where(incl, amat[:, n:, n:], 0.0)
    t_inv = _unit_lower_inverse(a_bk, c)
    av = _bmm(jnp.concatenate([a_kk, a_rk], axis=1), v_g)
    wu = _bmm(t_inv, jnp.concatenate([kk_g, av[:, :n]], axis=2))
    kd_v = _bmm(_transpose_tiles(_group(k * e_tail, c)), v_g)
    bd_t = _transpose_tiles(_group(b * e_tail, c))
    decay = _column_scale(e_cum, c)

    st = st_scr[...]
    per_step = bb * N_PAIRS
    o_tiles = []
    for ch in range(nch):
        sel = slice(ch * per_step, (ch + 1) * per_step)
        zo = _bmm(jnp.concatenate([wu[sel, :, :LANES], r_g[sel]], axis=1), st)
        z = wu[sel, :, LANES:] + zo[:, :n]
        o_tiles.append(zo[:, n:] + av[sel, n:] - _bmm(a_rb[sel], z))
        st = decay[sel] * st + kd_v[sel] - _bmm(bd_t[sel], z)
    st_scr[...] = st
    o = _ungroup(o_tiles[0] if nch == 1 else jnp.concatenate(o_tiles, axis=0), c)

    mean = _head_sum(o, ones) * (1.0 / HEAD_DIM)
    cen = o - mean
    var = _head_sum(cen * cen, ones) * (1.0 / HEAD_DIM)
    o = cen * lax.rsqrt(var + A_GN_EPS) * gnw_ref[...] + gnb_ref[...]
    bonus = _head_sum(r * k * rk_ref[...], ones) * v
    o_ref[...] = ((o + bonus) * g).reshape(bb, tt, MIX)
    st_out_ref[...] = st.reshape(bb, N_PAIRS, LANES, LANES)


def _mixer_tiling(bsz, t, c):
    if t > c:
        return 1, min(t, MIXER_ROWS)
    return min(bsz, MIXER_ROWS // t), t


def _rwkv(pa, shift0, st0, mu, w0, a0, wwa, gup, k_k, k_a, r_k, gn_w, gn_b, c):
    bsz, t, _ = pa.shape
    bb, tt = _mixer_tiling(bsz, t, c)
    vec = lambda w: _const_spec((1, w))
    state = pl.BlockSpec((bb, N_PAIRS, LANES, LANES), lambda b, i: (b, 0, 0, 0))
    return pl.pallas_call(
        functools.partial(_rwkv_kernel, c=c),
        grid=(bsz // bb, t // tt),
        in_specs=[pl.BlockSpec((bb, tt, A_COLS), lambda b, i: (b, i, 0)),
                  pl.BlockSpec((bb, 1, A_COLS), lambda b, i: (b, 0, 0)), state,
                  vec(A_COLS), vec(MIX), vec(MIX), _const_spec(wwa.shape), _const_spec(gup.shape),
                  vec(MIX), vec(MIX), vec(MIX), vec(MIX), vec(MIX)],
        out_specs=[pl.BlockSpec((bb, tt, MIX), lambda b, i: (b, i, 0)), state],
        out_shape=[jax.ShapeDtypeStruct((bsz, t, MIX), F32),
                   jax.ShapeDtypeStruct((bsz, N_PAIRS, LANES, LANES), F32)],
        scratch_shapes=[pltpu.VMEM((bb, tt + SUBLANES, A_COLS), F32),
                        pltpu.VMEM((bb * N_PAIRS, LANES, LANES), F32)],
        compiler_params=_params("arbitrary", "arbitrary"),
        name="rwkv",
    )(pa, shift0, st0, mu, w0, a0, wwa, gup, k_k, k_a, r_k, gn_w, gn_b)


def _gdn_kernel(pc_ref, pab_ref, conv0_ref, st0_ref, convw_ref, alog_ref, dtb_ref, nw_ref, eab_ref,
                o_ref, st_out_ref, ext_scr, st_scr, *, c):
    bb, tt, _ = pc_ref.shape
    rows, nch, n = bb * tt, tt // c, 2 * c
    assert bb == 1 or nch == 1
    w3 = 3 * MIX
    ci = pl.program_id(1)

    @pl.when(ci == 0)
    def _():
        ext_scr[:, 0:SUBLANES, :] = conv0_ref[...]
        st_scr[...] = st0_ref[...].reshape(bb * N_PAIRS, LANES, LANES)

    ext_scr[:, SUBLANES:SUBLANES + tt, :] = pc_ref[:, :, :w3]
    conv = ext_scr[:, SUBLANES:SUBLANES + tt, :] * convw_ref[CONV_W - 1:CONV_W, :]
    for s in range(1, CONV_W):
        conv = conv + ext_scr[:, SUBLANES - s:SUBLANES - s + tt, :] * convw_ref[CONV_W - 1 - s:CONV_W - s, :]
    ext_scr[:, 0:SUBLANES, :] = ext_scr[:, tt:tt + SUBLANES, :]
    qkv = _silu(conv.reshape(rows, w3))
    ones = _head_ones()
    l2n = lambda y: y * lax.rsqrt(_head_sum(y * y, ones) + RMS_EPS)
    q = l2n(qkv[:, :MIX]) * (HEAD_DIM ** -0.5)
    k = l2n(qkv[:, MIX:2 * MIX])
    v = qkv[:, 2 * MIX:]
    zgate = pc_ref[:, :, w3:].reshape(rows, MIX)
    ab = _hdot(pab_ref[...].reshape(rows, LANES), eab_ref[...])
    log_alpha = -jnp.exp(alog_ref[...]) * _softplus(ab[:, :MIX] + dtb_ref[...])
    beta = _sigmoid(ab[:, MIX:])
    gcum = _hdot(_chunk_tril(rows, c), log_alpha)
    eg = jnp.exp(gcum)
    bk = beta * k
    strict, incl = _tri_masks(n, c)

    g_tiles = _group(gcum, c)
    g_col = jnp.stack([g_tiles[i] + pltpu.roll(g_tiles[i], HEAD_DIM, axis=1)
                       for i in range(g_tiles.shape[0])])
    g_row = jnp.stack([jnp.transpose(g_col[i])[:n, :] for i in range(g_col.shape[0])])
    diff = g_col[:, :, :n] - g_row
    dec_s = jnp.where(strict, jnp.exp(jnp.where(strict, diff, 0.0)), 0.0)
    dec_i = jnp.where(incl, jnp.exp(jnp.where(incl, diff, 0.0)), 0.0)
    qg_g = _group(q * eg, c)
    kmat = _bmm_nt(jnp.concatenate([_group(bk, c), _group(q, c)], axis=1), _group(k, c))
    attn = kmat[:, n:] * dec_i
    t_inv = _unit_lower_inverse(kmat[:, :n] * dec_s, c)
    sol = _bmm(t_inv, jnp.concatenate([_group(beta * v, c), _group(bk * eg, c)], axis=2))
    kd_t = _transpose_tiles(_group(k * jnp.exp(_chunk_last(gcum, c) - gcum), c))
    decay = _column_scale(eg, c)

    st = st_scr[...]
    per_step = bb * N_PAIRS
    o_tiles = []
    for ch in range(nch):
        sel = slice(ch * per_step, (ch + 1) * per_step)
        wq = _bmm(jnp.concatenate([sol[sel, :, LANES:], qg_g[sel]], axis=1), st)
        delta = sol[sel, :, :LANES] - wq[:, :n]
        o_tiles.append(wq[:, n:] + _bmm(attn[sel], delta))
        st = decay[sel] * st + _bmm(kd_t[sel], delta)
    st_scr[...] = st
    o = _ungroup(o_tiles[0] if nch == 1 else jnp.concatenate(o_tiles, axis=0), c)

    o = o * lax.rsqrt(_head_sum(o * o, ones) * (1.0 / HEAD_DIM) + RMS_EPS) * nw_ref[...]
    o_ref[...] = (o * _silu(zgate)).reshape(bb, tt, MIX)
    st_out_ref[...] = st.reshape(bb, N_PAIRS, LANES, LANES)


def _gdn(pc, pab, conv0, st0, conv_w, a_log, dt_bias, norm_w, eab, c):
    bsz, t, _ = pc.shape
    bb, tt = _mixer_tiling(bsz, t, c)
    vec = lambda w: _const_spec((1, w))
    state = pl.BlockSpec((bb, N_PAIRS, LANES, LANES), lambda b, i: (b, 0, 0, 0))
    return pl.pallas_call(
        functools.partial(_gdn_kernel, c=c),
        grid=(bsz // bb, t // tt),
        in_specs=[pl.BlockSpec((bb, tt, C_MAIN), lambda b, i: (b, i, 0)),
                  pl.BlockSpec((bb, tt, LANES), lambda b, i: (b, i, 0)),
                  pl.BlockSpec((bb, SUBLANES, 3 * MIX), lambda b, i: (b, 0, 0)), state,
                  _const_spec((CONV_W, 3 * MIX)), vec(MIX), vec(MIX), vec(MIX), _const_spec(eab.shape)],
        out_specs=[pl.BlockSpec((bb, tt, MIX), lambda b, i: (b, i, 0)), state],
        out_shape=[jax.ShapeDtypeStruct((bsz, t, MIX), F32),
                   jax.ShapeDtypeStruct((bsz, N_PAIRS, LANES, LANES), F32)],
        scratch_shapes=[pltpu.VMEM((bb, tt + SUBLANES, 3 * MIX), F32),
                        pltpu.VMEM((bb * N_PAIRS, LANES, LANES), F32)],
        compiler_params=_params("arbitrary", "arbitrary"),
        name="gdn",
    )(pc, pab, conv0, st0, conv_w, a_log, dt_bias, norm_w, eab)


def _band_kernel(q_ref, k_ref, v_ref, bias_ref, o_ref, *, cq, wn, masked):
    ci = pl.program_id(1)
    start = pl.multiple_of(ci * cq, cq)
    q = q_ref[0]
    kw = k_ref[0, pl.ds(start, wn), :]
    vw = v_ref[0, pl.ds(start, wn), :]
    lane = _iota((1, LANES), 1)
    if masked:
        valid = _iota((1, wn), 1) >= BAND - ci * cq
    outs = []
    for j in range(N_PAIRS):
        sl = slice(j * LANES, (j + 1) * LANES)
        qp, kp, vp = q[:, sl], kw[:, sl], vw[:, sl]
        acc = jnp.zeros((cq, LANES), F32)
        for hh in range(2):
            m = (lane < HEAD_DIM) if hh == 0 else (lane >= HEAD_DIM)
            s = _dot_nt(jnp.where(m, qp, 0.0).astype(BF16), kp) * (HEAD_DIM ** -0.5) + bias_ref[2 * j + hh]
            if masked:
                s = jnp.where(valid, s, -1e30)
            e = jnp.exp(s - jnp.max(s, axis=-1, keepdims=True))
            pv = _dot(e.astype(BF16), jnp.where(m, vp, jnp.zeros_like(vp)))
            acc = acc + pv / jnp.sum(e, axis=-1, keepdims=True)
        outs.append(acc)
    o_ref[0] = jnp.concatenate(outs, axis=1)


def _band(q, k_ext, v_ext, bias, cq, masked):
    bsz, t, _ = q.shape
    text = k_ext.shape[1]
    wn = text - t + cq
    ext = pl.BlockSpec((1, text, MIX), lambda b, i: (b, 0, 0))
    return pl.pallas_call(
        functools.partial(_band_kernel, cq=cq, wn=wn, masked=masked),
        grid=(bsz, t // cq),
        in_specs=[pl.BlockSpec((1, cq, MIX), lambda b, i: (b, i, 0)), ext, ext, _const_spec(bias.shape)],
        out_specs=pl.BlockSpec((1, cq, MIX), lambda b, i: (b, i, 0)),
        out_shape=jax.ShapeDtypeStruct((bsz, t, MIX), F32),
        compiler_params=_params("arbitrary", "arbitrary"),
        name="band",
    )(q, k_ext, v_ext, bias)


def _merge_kernel(x_ref, oa_ref, ob_ref, oc_ref, sh_ref, sc_ref, gm_ref, sh2_ref, sc2_ref, nw_ref, nw2_ref,
                  wgate_ref, bgate_ref, wbr_ref, wout_ref, wrt_ref, rb_ref,
                  x1_ref, h2_ref, comb_ref):
    x = x_ref[...]
    h = _norm_mod(x, nw_ref[...], sc_ref[0], sh_ref[0]).astype(BF16)
    gates = _sigmoid(_dot(h, wgate_ref[...]) + bgate_ref[...])
    mixed = None
    for i, o_ref in enumerate((oa_ref, ob_ref, oc_ref)):
        term = gates[:, i * D_MODEL:(i + 1) * D_MODEL] * _bdot(o_ref[...], wbr_ref[i])
        mixed = term if mixed is None else mixed + term
    x1 = x + gm_ref[0] * _bdot(mixed, wout_ref[...])
    x1_ref[...] = x1
    h2 = _norm_mod(x1, nw2_ref[...], sc2_ref[0], sh2_ref[0])
    h2_ref[...] = h2.astype(BF16)

    scores = _sigmoid(_dot_nt(wrt_ref[...], h2, HIGHEST))
    sel = scores + rb_ref[...]
    tm = scores.shape[1]
    per = N_EXPERTS // N_GROUPS
    best_val, best = None, None
    for g in range(N_GROUPS):
        rows = [sel[g * per + i:g * per + i + 1, :] for i in range(per)]
        top2 = None
        for i in range(per):
            for i2 in range(i + 1, per):
                pair = rows[i] + rows[i2]
                top2 = pair if top2 is None else jnp.maximum(top2, pair)
        if g == 0:
            best_val, best = top2, jnp.zeros((1, tm), jnp.int32)
        else:
            better = top2 > best_val
            best = jnp.where(better, g, best)
            best_val = jnp.where(better, top2, best_val)
    eidx = _iota((N_EXPERTS, tm), 0)
    cand = jnp.where(eidx // per == best, sel, -jnp.inf)
    m1 = jnp.max(cand, axis=0, keepdims=True)
    i1 = jnp.min(jnp.where(cand == m1, eidx, N_EXPERTS), axis=0, keepdims=True)
    cand2 = jnp.where(eidx == i1, -jnp.inf, cand)
    m2 = jnp.max(cand2, axis=0, keepdims=True)
    i2 = jnp.min(jnp.where(cand2 == m2, eidx, N_EXPERTS), axis=0, keepdims=True)
    w1 = jnp.sum(jnp.where(eidx == i1, scores, 0.0), axis=0, keepdims=True)
    w2 = jnp.sum(jnp.where(eidx == i2, scores, 0.0), axis=0, keepdims=True)
    den = w1 + w2
    comb_ref[...] = jnp.where(eidx == i1, w1 / den, 0.0) + jnp.where(eidx == i2, w2 / den, 0.0)


def _merge(x, oa, ob, oc, mods, norm_w, norm2_w, w_gate, b_gate, w_branch, w_out, w_router_t, router_bias,
           tm, tiles_per_group):
    n = x.shape[0]
    row = lambda w: pl.BlockSpec((tm, w), lambda i: (i, 0))
    mod_specs = [_mod_spec(m, tm, tiles_per_group) for m in mods]
    return pl.pallas_call(
        _merge_kernel,
        grid=(n // tm,),
        in_specs=[row(D_MODEL), row(MIX), row(MIX), row(MIX)] + mod_specs
                 + [_const_spec((1, D_MODEL)), _const_spec((1, D_MODEL)), _const_spec(w_gate.shape),
                    _const_spec(b_gate.shape), _const_spec(w_branch.shape), _const_spec(w_out.shape),
                    _const_spec(w_router_t.shape), _const_spec(router_bias.shape)],
        out_specs=[row(D_MODEL), row(D_MODEL), pl.BlockSpec((N_EXPERTS, tm), lambda i: (0, i))],
        out_shape=[jax.ShapeDtypeStruct((n, D_MODEL), F32), jax.ShapeDtypeStruct((n, D_MODEL), BF16),
                   jax.ShapeDtypeStruct((N_EXPERTS, n), F32)],
        compiler_params=_params("arbitrary"),
        name="merge",
    )(x, oa, ob, oc, *mods, norm_w, norm2_w, w_gate, b_gate, w_branch, w_out, w_router_t, router_bias)


def _moe_kernel(x1_ref, h2_ref, comb_ref, gf_ref, wg_ref, wu_ref, wd_ref, o_ref):
    h2 = h2_ref[...]
    comb = jnp.transpose(comb_ref[...])
    acc = jnp.zeros(o_ref.shape, F32)
    for e in range(N_EXPERTS):
        hid = _silu(_dot(h2, wg_ref[e])) * _dot(h2, wu_ref[e]) * comb[:, e:e + 1]
        acc = acc + _dot(hid.astype(BF16), wd_ref[e])
    o_ref[...] = x1_ref[...] + gf_ref[0] * acc


def _moe(x1, h2, comb, g_ffn, wg, wu, wd, tm, tiles_per_group):
    n = x1.shape[0]
    row = lambda w: pl.BlockSpec((tm, w), lambda i: (i, 0))
    return pl.pallas_call(
        _moe_kernel,
        grid=(n // tm,),
        in_specs=[row(D_MODEL), row(D_MODEL), pl.BlockSpec((N_EXPERTS, tm), lambda i: (0, i)),
                  _mod_spec(g_ffn, tm, tiles_per_group),
                  _const_spec(wg.shape), _const_spec(wu.shape), _const_spec(wd.shape)],
        out_specs=row(D_MODEL),
        out_shape=jax.ShapeDtypeStruct((n, D_MODEL), F32),
        compiler_params=_params("arbitrary"),
        name="moe",
    )(x1, h2, comb, g_ffn, wg, wu, wd)


def _pair_state(s):
    bsz = s.shape[0]
    s = s.reshape(bsz, N_PAIRS, 2, HEAD_DIM, HEAD_DIM)
    z = jnp.zeros_like(s[:, :, 0])
    top = jnp.concatenate([s[:, :, 0], z], axis=-1)
    bot = jnp.concatenate([z, s[:, :, 1]], axis=-1)
    return jnp.concatenate([top, bot], axis=-2)


def _unpair_state(s):
    bsz = s.shape[0]
    return jnp.stack([s[:, :, :HEAD_DIM, :HEAD_DIM], s[:, :, HEAD_DIM:, HEAD_DIM:]], axis=2).reshape(
        bsz, N_HEADS, HEAD_DIM, HEAD_DIM)


def _rel_bias(table, cq, wn):
    offs = jnp.arange(-(cq - 1), wn)
    diag = table[:, jnp.clip(BAND - offs, -MAX_REL, MAX_REL) + MAX_REL].astype(F32)
    return jnp.stack([diag[:, cq - 1 - q:cq - 1 - q + wn] for q in range(cq)], axis=1)


def _layer(x, mod, state, wts, tm):
    bsz, t, _ = x.shape
    n = bsz * t
    c = min(CHUNK, t)
    shift0, wkv0, k_past, v_past, conv0, s0 = state
    if t % tm == 0:
        tiles_per_group = t // tm
        mods = [m.reshape(bsz, 1, D_MODEL) for m in jnp.split(mod, 6, axis=-1)]
    else:
        tiles_per_group = 1
        mods = [jnp.repeat(m, t, axis=0).reshape(n // tm, tm, D_MODEL) for m in jnp.split(mod, 6, axis=-1)]
    sh_mix, sc_mix, g_mix, sh_ffn, sc_ffn, g_ffn = mods
    xf = x.reshape(n, D_MODEL)

    pa, q, k, v, pc, pab = _inproj(xf, sh_mix, sc_mix, wts["norm_mix_w"], wts["wa"], wts["wb"], wts["wc"],
                                   wts["wab"], wts["q_norm"], wts["k_norm"], tm, tiles_per_group)
    pa = pa.reshape(bsz, t, A_COLS)
    q, k, v = (z.reshape(bsz, t, MIX) for z in (q, k, v))
    pc = pc.reshape(bsz, t, C_MAIN)

    oa, wkv = _rwkv(pa, shift0.reshape(bsz, 1, A_COLS), _pair_state(jnp.swapaxes(wkv0, -1, -2)),
                    wts["mu"], wts["w0"], wts["a0"], wts["wwa"], wts["g_up"], wts["k_k"], wts["k_a"], wts["r_k"],
                    wts["gn_w"], wts["gn_b"], c)
    new_wkv = jnp.swapaxes(_unpair_state(wkv), -1, -2)

    if k_past is None:
        pad = lambda z: jnp.pad(z.astype(BF16), ((0, 0), (BAND, 0), (0, 0)))
        ob = _band(q, pad(k), pad(v), _rel_bias(wts["rel_bias"], CHUNK, BAND + CHUNK), CHUNK, True)
        new_k, new_v = k[:, -BAND:], v[:, -BAND:]
    else:
        past = k_past.shape[1]
        cat = lambda zp, z: jnp.concatenate([zp.reshape(bsz, past, MIX), z], axis=1).astype(BF16)
        ob = _band(q, cat(k_past, k), cat(v_past, v), _rel_bias(wts["rel_bias"], t, past + t), t, False)
        new_k, new_v = k, v

    conv_pad = jnp.pad(conv0, ((0, 0), (SUBLANES - (CONV_W - 1), 0), (0, 0)))
    oc, s_new = _gdn(pc, pab.reshape(bsz, t, LANES), conv_pad, _pair_state(s0), wts["conv_w"], wts["a_log"],
                     wts["dt_bias"], wts["gdn_norm_w"], wts["eab"], c)
    new_conv = jnp.concatenate([conv0, pc[:, :, :3 * MIX]], axis=1)[:, -(CONV_W - 1):]

    x1, h2, comb = _merge(xf, oa.reshape(n, MIX), ob.reshape(n, MIX), oc.reshape(n, MIX),
                          [sh_mix, sc_mix, g_mix, sh_ffn, sc_ffn], wts["norm_mix_w"], wts["norm_ffn_w"],
                          wts["w_gate"], wts["b_gate"], wts["w_branch"], wts["w_out"], wts["w_router_t"],
                          wts["router_bias"], tm, tiles_per_group)
    x2 = _moe(x1, h2, comb, g_ffn, wts["wg"], wts["wu"], wts["wd"], tm, tiles_per_group)
    heads = lambda z: z.reshape(bsz, -1, N_HEADS, HEAD_DIM)
    return x2.reshape(bsz, t, D_MODEL), (pa[:, -1], new_wkv, heads(new_k), heads(new_v), new_conv,
                                         _unpair_state(s_new))


def _prepare_layer(l, w_in, norm_mix_w, norm_ffn_w, rwkv_mu, rwkv_w0, rwkv_w_up, rwkv_a0, rwkv_a_up, rwkv_g_up,
                   rwkv_k_k, rwkv_k_a, rwkv_r_k, rwkv_gn_w, rwkv_gn_b, band_q_norm, band_k_norm, band_rel_bias,
                   gdn_conv_w, gdn_a_log, gdn_dt_bias, gdn_norm_w, w_branch, w_gate, b_gate, w_out,
                   w_router, router_bias, w_exp_gate, w_exp_up, w_exp_down):
    row = lambda z: z.reshape(1, -1).astype(F32)
    per_head = lambda z: jnp.repeat(z, HEAD_DIM).reshape(1, MIX)
    win = w_in[l].astype(BF16)
    c0 = A_COLS + B_COLS
    wab = jnp.pad(win[:, c0 + C_MAIN:], ((0, 0), (0, LANES - 2 * N_HEADS)))
    zeros = jnp.zeros((HEAD_DIM, MIX), F32)
    wwa = jnp.concatenate([jnp.concatenate([rwkv_w_up[l], zeros], axis=1),
                           jnp.concatenate([zeros, rwkv_a_up[l]], axis=1)], axis=0)
    head_of_lane = jnp.arange(MIX) // HEAD_DIM
    src = jnp.arange(LANES)[:, None]
    eab = jnp.concatenate([(src == head_of_lane[None, :]), (src == N_HEADS + head_of_lane[None, :])],
                          axis=1).astype(F32)
    return dict(
        norm_mix_w=row(norm_mix_w[l]), norm_ffn_w=row(norm_ffn_w[l]),
        wa=win[:, :A_COLS], wb=win[:, A_COLS:c0], wc=win[:, c0:c0 + C_MAIN], wab=wab,
        q_norm=row(jnp.tile(band_q_norm[l], N_HEADS)), k_norm=row(jnp.tile(band_k_norm[l], N_HEADS)),
        mu=row(rwkv_mu[l]), w0=row(rwkv_w0[l]), a0=row(rwkv_a0[l]), wwa=wwa, g_up=rwkv_g_up[l],
        k_k=row(rwkv_k_k[l]), k_a=row(rwkv_k_a[l]), r_k=row(rwkv_r_k[l]), gn_w=row(rwkv_gn_w[l]),
        gn_b=row(rwkv_gn_b[l]), rel_bias=band_rel_bias[l],
        conv_w=gdn_conv_w[l], a_log=per_head(gdn_a_log[l]), dt_bias=per_head(gdn_dt_bias[l]),
        gdn_norm_w=row(jnp.tile(gdn_norm_w[l], N_HEADS)), eab=eab,
        w_gate=w_gate[l].astype(BF16), b_gate=row(b_gate[l]), w_branch=w_branch[l].astype(BF16),
        w_out=w_out[l].astype(BF16), w_router_t=jnp.transpose(w_router), router_bias=router_bias.reshape(-1, 1),
        wg=w_exp_gate[l].astype(BF16), wu=w_exp_up[l].astype(BF16), wd=w_exp_down[l].astype(BF16),
    )


def kernel(x_prompt, x_sample, c_prompt, c_sample, state_rwkv_shift, state_rwkv_wkv, cache_band_k, cache_band_v, state_gdn_conv, state_gdn_S, w_ada, b_ada, norm_mix_w, norm_ffn_w, w_in, rwkv_mu, rwkv_w0, rwkv_w_up, rwkv_a0, rwkv_a_up, rwkv_g_up, rwkv_k_k, rwkv_k_a, rwkv_r_k, rwkv_gn_w, rwkv_gn_b, band_q_norm, band_k_norm, band_rel_bias, gdn_conv_w, gdn_a_log, gdn_dt_bias, gdn_norm_w, w_branch, w_gate, b_gate, w_out, w_router, router_bias, w_exp_gate, w_exp_up, w_exp_down):
    depth = w_ada.shape[0]
    bsz = x_prompt.shape[0]
    mod_p = _adaln(c_prompt, w_ada, b_ada)
    mod_s = _adaln(c_sample, w_ada, b_ada)
    zero_state = (jnp.zeros((bsz, A_COLS), F32), jnp.zeros((bsz, N_HEADS, HEAD_DIM, HEAD_DIM), F32), None, None,
                  jnp.zeros((bsz, CONV_W - 1, 3 * MIX), F32), jnp.zeros((bsz, N_HEADS, HEAD_DIM, HEAD_DIM), F32))
    tm = 256
    xp, xs = x_prompt, x_sample
    new_p, new_s = [], []
    for l in range(depth):
        wts = _prepare_layer(l, w_in, norm_mix_w, norm_ffn_w, rwkv_mu, rwkv_w0, rwkv_w_up, rwkv_a0, rwkv_a_up,
                             rwkv_g_up, rwkv_k_k, rwkv_k_a, rwkv_r_k, rwkv_gn_w, rwkv_gn_b, band_q_norm,
                             band_k_norm, band_rel_bias, gdn_conv_w, gdn_a_log, gdn_dt_bias, gdn_norm_w,
                             w_branch, w_gate, b_gate, w_out, w_router, router_bias, w_exp_gate, w_exp_up,
                             w_exp_down)
        xp, st_p = _layer(xp, mod_p[l], zero_state, wts, tm)
        xs, st_s = _layer(xs, mod_s[l], (state_rwkv_shift[l], state_rwkv_wkv[l], cache_band_k[l], cache_band_v[l],
                                         state_gdn_conv[l], state_gdn_S[l]), wts, tm)
        new_p.append(st_p)
        new_s.append(st_s)
    p_out = [jnp.stack(z, axis=0) for z in zip(*new_p)]
    s_out = [jnp.stack(z, axis=0) for z in zip(*new_s)]
    return (xp, xs, *p_out, *s_out)
```

```python
import functools
import math

import jax
import jax.numpy as jnp
from jax import lax
from jax.experimental import pallas as pl
from jax.experimental.pallas import tpu as pltpu

F32 = jnp.float32
BF16 = jnp.bfloat16
HIGHEST = lax.Precision.HIGHEST

D_MODEL = 1024
MIX = 512
HEAD_DIM = 64
N_HEADS = MIX // HEAD_DIM
LANES = 128
N_PAIRS = MIX // LANES
SUBLANES = 8
CHUNK = 64
MIXER_ROWS = 256
BAND_QUERIES = 2 * CHUNK
A_COLS = 3 * MIX + 64 + 64 + 128
B_COLS = 3 * MIX
C_MAIN = 4 * MIX
CONV_W = 4
BAND = 8 * CHUNK
MAX_REL = 2 * CHUNK
N_EXPERTS = 16
N_GROUPS = 4
EXPERT_DIM = D_MODEL // 4
RMS_EPS = 1e-6
A_GN_EPS = 64e-5
A_DECAY_SCALE = math.exp(-0.5)
VMEM_LIMIT_BYTES = 56 * 1024 * 1024


def _params(*sem):
    return pltpu.CompilerParams(dimension_semantics=sem, vmem_limit_bytes=VMEM_LIMIT_BYTES)


def _dot(a, b, precision=None):
    return lax.dot_general(a, b, (((1,), (0,)), ((), ())), precision=precision, preferred_element_type=F32)


def _dot_nt(a, b, precision=None):
    return lax.dot_general(a, b, (((1,), (1,)), ((), ())), precision=precision, preferred_element_type=F32)


def _bdot(a, b):
    return _dot(a.astype(BF16), b.astype(BF16))


def _bdot_nt(a, b):
    return _dot_nt(a.astype(BF16), b.astype(BF16))


def _hdot(a, b):
    return _dot(a, b, HIGHEST)


def _bmm(a, b):
    return lax.dot_general(a.astype(BF16), b.astype(BF16), (((2,), (1,)), ((0,), (0,))),
                           preferred_element_type=F32)


def _bmm_nt(a, b):
    return lax.dot_general(a.astype(BF16), b.astype(BF16), (((2,), (2,)), ((0,), (0,))),
                           preferred_element_type=F32)


def _sigmoid(x):
    return 1.0 / (1.0 + jnp.exp(-x))


def _silu(x):
    return x * _sigmoid(x)


def _softplus(x):
    return jnp.maximum(x, 0.0) + jnp.log1p(jnp.exp(-jnp.abs(x)))


def _iota(shape, dim):
    return lax.broadcasted_iota(jnp.int32, shape, dim)


def _bf16_terms(x, terms):
    parts = []
    for _ in range(terms):
        part = x.astype(BF16)
        parts.append(part)
        x = x - part.astype(F32)
    return parts


def _dot_01(x, m, terms):
    out = None
    for part in _bf16_terms(x, terms):
        term = _dot(part, m)
        out = term if out is None else out + term
    return out


def _dot_01_left(m, x, terms):
    out = None
    for part in _bf16_terms(x, terms):
        term = _dot(m, part)
        out = term if out is None else out + term
    return out


def _head_ones():
    return (_iota((LANES, LANES), 0) // HEAD_DIM == _iota((LANES, LANES), 1) // HEAD_DIM).astype(BF16)


def _head_sum(x, ones):
    parts = [_dot_01(x[:, g * LANES:(g + 1) * LANES], ones, 2) for g in range(x.shape[1] // LANES)]
    return parts[0] if len(parts) == 1 else jnp.concatenate(parts, axis=1)


def _stack(x):
    lane = _iota(x.shape, 1)
    return jnp.concatenate([jnp.where(lane < HEAD_DIM, x, 0.0), jnp.where(lane >= HEAD_DIM, x, 0.0)], axis=0)


def _group(x, c):
    return jnp.stack([_stack(x[u * c:(u + 1) * c, j * LANES:(j + 1) * LANES])
                      for u in range(x.shape[0] // c) for j in range(N_PAIRS)])


def _ungroup(xs, c):
    return jnp.concatenate(
        [jnp.concatenate([xs[u * N_PAIRS + j, :c] + xs[u * N_PAIRS + j, c:] for j in range(N_PAIRS)], axis=1)
         for u in range(xs.shape[0] // N_PAIRS)], axis=0)


def _transpose_tiles(xs):
    return jnp.stack([jnp.transpose(xs[g]) for g in range(xs.shape[0])])


def _chunk_last(x, c):
    return jnp.concatenate([jnp.broadcast_to(x[(u + 1) * c - 1:(u + 1) * c], (c, x.shape[1]))
                            for u in range(x.shape[0] // c)], axis=0)


def _column_scale(row_values, c):
    tiles = []
    for u in range(row_values.shape[0] // c):
        last = row_values[(u + 1) * c - 1:(u + 1) * c]
        for j in range(N_PAIRS):
            tiles.append(jnp.transpose(jnp.broadcast_to(last[:, j * LANES:(j + 1) * LANES], (LANES, LANES))))
    return jnp.stack(tiles)


def _chunk_tril(rows, c):
    row, col = _iota((rows, rows), 0), _iota((rows, rows), 1)
    return ((col <= row) & (row // c == col // c)).astype(BF16)


def _tri_masks(n, c):
    row, col = _iota((n, n), 0), _iota((n, n), 1)
    same = row // c == col // c
    return same & (col < row), same & (col <= row)


def _unit_lower_inverse(a, c):
    n = a.shape[1]
    row, col = _iota((n, n), 0), _iota((n, n), 1)
    eye = (row == col).astype(F32)
    same = lambda s: row // s == col // s
    a0 = jnp.where(same(SUBLANES), a, 0.0)
    a2 = _bmm(a0, a0)
    a4 = _bmm(a2, a2)
    x = _bmm(_bmm(eye - a0, eye + a2), eye + a4)
    s = SUBLANES
    while s < c:
        off = jnp.where(same(2 * s) & jnp.logical_not(same(s)), a, 0.0)
        x = x - _bmm(_bmm(x, off), x)
        s *= 2
    return x


def _adaln_kernel(c_ref, w_ref, b_ref, o_ref):
    o_ref[0] = _hdot(_silu(c_ref[...]), w_ref[0]) + b_ref[0]


def _adaln(c, w_ada, b_ada):
    depth, bsz, tn = w_ada.shape[0], c.shape[0], 768
    return pl.pallas_call(
        _adaln_kernel,
        grid=(depth, 6 * D_MODEL // tn),
        in_specs=[pl.BlockSpec((bsz, D_MODEL), lambda l, j: (0, 0)),
                  pl.BlockSpec((1, D_MODEL, tn), lambda l, j: (l, 0, j)),
                  pl.BlockSpec((1, 1, tn), lambda l, j: (l, 0, j))],
        out_specs=pl.BlockSpec((1, bsz, tn), lambda l, j: (l, 0, j)),
        out_shape=jax.ShapeDtypeStruct((depth, bsz, 6 * D_MODEL), F32),
        compiler_params=_params("arbitrary", "arbitrary"),
        name="adaln",
    )(c, w_ada, b_ada.reshape(depth, 1, 6 * D_MODEL))


def _norm_mod(x, norm_w, scale, shift):
    y = x * lax.rsqrt(jnp.mean(x * x, axis=-1, keepdims=True) + RMS_EPS)
    return y * norm_w * (1.0 + scale) + shift


def _inproj_kernel(x_ref, sh_ref, sc_ref, nw_ref, wa_ref, wb_ref, wc_ref, wab_ref, qn_ref, kn_ref,
                   pa_ref, q_ref, k_ref, v_ref, pc_ref, pab_ref):
    h = _norm_mod(x_ref[...], nw_ref[...], sc_ref[0], sh_ref[0]).astype(BF16)
    pa_ref[...] = _dot(h, wa_ref[...])
    pb = _dot(h, wb_ref[...])
    ones = _head_ones()

    def head_rms(y, w):
        return y * lax.rsqrt(_head_sum(y * y, ones) * (1.0 / HEAD_DIM) + RMS_EPS) * w

    q_ref[...] = head_rms(pb[:, :MIX], qn_ref[...])
    k_ref[...] = head_rms(pb[:, MIX:2 * MIX], kn_ref[...])
    v_ref[...] = pb[:, 2 * MIX:]
    pc_ref[...] = _dot(h, wc_ref[...])
    pab_ref[...] = _dot(h, wab_ref[...])


def _mod_spec(mod, tm, tiles_per_group):
    rows = mod.shape[1]
    return pl.BlockSpec((1, rows, D_MODEL), lambda i: (i // tiles_per_group, 0, 0))


def _const_spec(shape):
    zeros = (0,) * len(shape)
    return pl.BlockSpec(shape, lambda *_: zeros, pipeline_mode=pl.Buffered(1))


def _inproj(x, shift, scale, norm_w, wa, wb, wc, wab, q_norm, k_norm, tm, tiles_per_group):
    n = x.shape[0]
    row = lambda w: pl.BlockSpec((tm, w), lambda i: (i, 0))
    widths = (A_COLS, MIX, MIX, MIX, C_MAIN, LANES)
    return pl.pallas_call(
        _inproj_kernel,
        grid=(n // tm,),
        in_specs=[row(D_MODEL), _mod_spec(shift, tm, tiles_per_group), _mod_spec(scale, tm, tiles_per_group),
                  _const_spec((1, D_MODEL)), _const_spec(wa.shape), _const_spec(wb.shape), _const_spec(wc.shape),
                  _const_spec(wab.shape), _const_spec((1, MIX)), _const_spec((1, MIX))],
        out_specs=[row(w) for w in widths],
        out_shape=[jax.ShapeDtypeStruct((n, w), F32) for w in widths],
        compiler_params=_params("arbitrary"),
        name="inproj",
    )(x, shift, scale, norm_w, wa, wb, wc, wab, q_norm, k_norm)


def _rwkv_kernel(p_ref, shift0_ref, st0_ref, mu_ref, w0_ref, a0_ref, wwa_ref, gup_ref, kk_ref, ka_ref, rk_ref,
                 gnw_ref, gnb_ref, o_ref, st_out_ref, ext_scr, st_scr, *, c):
    bb, tt, _ = p_ref.shape
    rows, nch, n = bb * tt, tt // c, 2 * c
    assert bb == 1 or nch == 1
    ci = pl.program_id(1)

    @pl.when(ci == 0)
    def _():
        ext_scr[:, SUBLANES - 1:SUBLANES, :] = shift0_ref[...]
        st_scr[...] = st0_ref[...].reshape(bb * N_PAIRS, LANES, LANES)

    ext_scr[:, SUBLANES:SUBLANES + tt, :] = p_ref[...]
    p = p_ref[...].reshape(rows, A_COLS)
    p_prev = ext_scr[:, SUBLANES - 1:SUBLANES - 1 + tt, :].reshape(rows, A_COLS)
    ext_scr[:, 0:SUBLANES, :] = ext_scr[:, tt:tt + SUBLANES, :]
    xs = p + (p_prev - p) * mu_ref[...]
    r, k, v = xs[:, :MIX], xs[:, MIX:2 * MIX], xs[:, 2 * MIX:3 * MIX]
    lora_in = xs[:, 3 * MIX:3 * MIX + LANES]
    gd = xs[:, 3 * MIX + LANES:]
    lane = _iota((1, LANES), 1)
    lora = _bdot(jnp.where(lane < HEAD_DIM, jnp.tanh(lora_in), lora_in), wwa_ref[...])
    log_w = -A_DECAY_SCALE * _sigmoid(w0_ref[...] + lora[:, :MIX])
    a = _sigmoid(a0_ref[...] + lora[:, MIX:])
    g = _bdot(_sigmoid(gd), gup_ref[...])
    ones = _head_ones()
    kk_raw = k * kk_ref[...]
    kk = kk_raw * lax.rsqrt(_head_sum(kk_raw * kk_raw, ones) + RMS_EPS)
    k = k * (1.0 + (a - 1.0) * ka_ref[...])
    b = kk * a
    cum = _dot_01_left(_chunk_tril(rows, c), log_w, 3)
    cum_last = _chunk_last(cum, c)
    e_cum, e_neg = jnp.exp(cum), jnp.exp(-cum)
    e_prev, e_tail = jnp.exp(cum - log_w), jnp.exp(cum_last - cum)
    strict, incl = _tri_masks(n, c)

    kk_g, r_g, v_g = _group(kk * e_prev, c), _group(r * e_cum, c), _group(v, c)
    amat = _bmm_nt(jnp.concatenate([kk_g, r_g], axis=1),
                   jnp.concatenate([_group(k * e_neg, c), _group(b * e_neg, c)], axis=1))
    a_kk = jnp.where(strict, amat[:, :n, :n], 0.0)
    a_bk = jnp.where(strict, amat[:, :n, n:], 0.0)
    a_rk = jnp.where(incl, amat[:, n:, :n], 0.0)
    a_rb = jnp.where(incl, amat[:, n:, n:], 0.0)
    t_inv = _unit_lower_inverse(a_bk, c)
    av = _bmm(jnp.concatenate([a_kk, a_rk], axis=1), v_g)
    wu = _bmm(t_inv, jnp.concatenate([kk_g, av[:, :n]], axis=2))
    kd_v = _bmm(_transpose_tiles(_group(k * e_tail, c)), v_g)
    bd_t = _transpose_tiles(_group(b * e_tail, c))
    decay = _column_scale(e_cum, c)

    st = st_scr[...]
    per_step = bb * N_PAIRS
    o_tiles = []
    for ch in range(nch):
        sel = slice(ch * per_step, (ch + 1) * per_step)
        zo = _bmm(jnp.concatenate([wu[sel, :, :LANES], r_g[sel]], axis=1), st)
        z = wu[sel, :, LANES:] + zo[:, :n]
        o_tiles.append(zo[:, n:] + av[sel, n:] - _bmm(a_rb[sel], z))
        st = decay[sel] * st + kd_v[sel] - _bmm(bd_t[sel], z)
    st_scr[...] = st
    o = _ungroup(o_tiles[0] if nch == 1 else jnp.concatenate(o_tiles, axis=0), c)

    mean = _head_sum(o, ones) * (1.0 / HEAD_DIM)
    cen = o - mean
    var = _head_sum(cen * cen, ones) * (1.0 / HEAD_DIM)
    o = cen * lax.rsqrt(var + A_GN_EPS) * gnw_ref[...] + gnb_ref[...]
    bonus = _head_sum(r * k * rk_ref[...], ones) * v
    o_ref[...] = ((o + bonus) * g).reshape(bb, tt, MIX)
    st_out_ref[...] = st.reshape(bb, N_PAIRS, LANES, LANES)


def _mixer_tiling(bsz, t, c):
    if t > c:
        return 1, min(t, MIXER_ROWS)
    return min(bsz, MIXER_ROWS // t), t


def _rwkv(pa, shift0, st0, mu, w0, a0, wwa, gup, k_k, k_a, r_k, gn_w, gn_b, c):
    bsz, t, _ = pa.shape
    bb, tt = _mixer_tiling(bsz, t, c)
    vec = lambda w: _const_spec((1, w))
    state = pl.BlockSpec((bb, N_PAIRS, LANES, LANES), lambda b, i: (b, 0, 0, 0))
    return pl.pallas_call(
        functools.partial(_rwkv_kernel, c=c),
        grid=(bsz // bb, t // tt),
        in_specs=[pl.BlockSpec((bb, tt, A_COLS), lambda b, i: (b, i, 0)),
                  pl.BlockSpec((bb, 1, A_COLS), lambda b, i: (b, 0, 0)), state,
                  vec(A_COLS), vec(MIX), vec(MIX), _const_spec(wwa.shape), _const_spec(gup.shape),
                  vec(MIX), vec(MIX), vec(MIX), vec(MIX), vec(MIX)],
        out_specs=[pl.BlockSpec((bb, tt, MIX), lambda b, i: (b, i, 0)), state],
        out_shape=[jax.ShapeDtypeStruct((bsz, t, MIX), F32),
                   jax.ShapeDtypeStruct((bsz, N_PAIRS, LANES, LANES), F32)],
        scratch_shapes=[pltpu.VMEM((bb, tt + SUBLANES, A_COLS), F32),
                        pltpu.VMEM((bb * N_PAIRS, LANES, LANES), F32)],
        compiler_params=_params("arbitrary", "arbitrary"),
        name="rwkv",
    )(pa, shift0, st0, mu, w0, a0, wwa, gup, k_k, k_a, r_k, gn_w, gn_b)


def _gdn_kernel(pc_ref, pab_ref, conv0_ref, st0_ref, convw_ref, alog_ref, dtb_ref, nw_ref, eab_ref,
                o_ref, st_out_ref, ext_scr, st_scr, *, c):
    bb, tt, _ = pc_ref.shape
    rows, nch, n = bb * tt, tt // c, 2 * c
    assert bb == 1 or nch == 1
    w3 = 3 * MIX
    ci = pl.program_id(1)

    @pl.when(ci == 0)
    def _():
        ext_scr[:, 0:SUBLANES, :] = conv0_ref[...]
        st_scr[...] = st0_ref[...].reshape(bb * N_PAIRS, LANES, LANES)

    ext_scr[:, SUBLANES:SUBLANES + tt, :] = pc_ref[:, :, :w3]
    conv = ext_scr[:, SUBLANES:SUBLANES + tt, :] * convw_ref[CONV_W - 1:CONV_W, :]
    for s in range(1, CONV_W):
        conv = conv + ext_scr[:, SUBLANES - s:SUBLANES - s + tt, :] * convw_ref[CONV_W - 1 - s:CONV_W - s, :]
    ext_scr[:, 0:SUBLANES, :] = ext_scr[:, tt:tt + SUBLANES, :]
    qkv = _silu(conv.reshape(rows, w3))
    ones = _head_ones()
    l2n = lambda y: y * lax.rsqrt(_head_sum(y * y, ones) + RMS_EPS)
    q = l2n(qkv[:, :MIX]) * (HEAD_DIM ** -0.5)
    k = l2n(qkv[:, MIX:2 * MIX])
    v = qkv[:, 2 * MIX:]
    zgate = pc_ref[:, :, w3:].reshape(rows, MIX)
    ab = _dot_01(pab_ref[...].reshape(rows, LANES), eab_ref[...], 3)
    log_alpha = -jnp.exp(alog_ref[...]) * _softplus(ab[:, :MIX] + dtb_ref[...])
    beta = _sigmoid(ab[:, MIX:])
    gcum = _dot_01_left(_chunk_tril(rows, c), log_alpha, 3)
    eg = jnp.exp(gcum)
    bk = beta * k
    strict, incl = _tri_masks(n, c)

    g_tiles = _group(gcum, c)
    g_col = jnp.stack([g_tiles[i] + pltpu.roll(g_tiles[i], HEAD_DIM, axis=1)
                       for i in range(g_tiles.shape[0])])
    g_row = jnp.stack([jnp.transpose(g_col[i])[:n, :] for i in range(g_col.shape[0])])
    diff = g_col[:, :, :n] - g_row
    dec_s = jnp.where(strict, jnp.exp(jnp.where(strict, diff, 0.0)), 0.0)
    dec_i = jnp.where(incl, jnp.exp(jnp.where(incl, diff, 0.0)), 0.0)
    qg_g = _group(q * eg, c)
    kmat = _bmm_nt(jnp.concatenate([_group(bk, c), _group(q, c)], axis=1), _group(k, c))
    attn = kmat[:, n:] * dec_i
    t_inv = _unit_lower_inverse(kmat[:, :n] * dec_s, c)
    sol = _bmm(t_inv, jnp.concatenate([_group(beta * v, c), _group(bk * eg, c)], axis=2))
    kd_t = _transpose_tiles(_group(k * jnp.exp(_chunk_last(gcum, c) - gcum), c))
    decay = _column_scale(eg, c)

    st = st_scr[...]
    per_step = bb * N_PAIRS
    o_tiles = []
    for ch in range(nch):
        sel = slice(ch * per_step, (ch + 1) * per_step)
        wq = _bmm(jnp.concatenate([sol[sel, :, LANES:], qg_g[sel]], axis=1), st)
        delta = sol[sel, :, :LANES] - wq[:, :n]
        o_tiles.append(wq[:, n:] + _bmm(attn[sel], delta))
        st = decay[sel] * st + _bmm(kd_t[sel], delta)
    st_scr[...] = st
    o = _ungroup(o_tiles[0] if nch == 1 else jnp.concatenate(o_tiles, axis=0), c)

    o = o * lax.rsqrt(_head_sum(o * o, ones) * (1.0 / HEAD_DIM) + RMS_EPS) * nw_ref[...]
    o_ref[...] = (o * _silu(zgate)).reshape(bb, tt, MIX)
    st_out_ref[...] = st.reshape(bb, N_PAIRS, LANES, LANES)


def _gdn(pc, pab, conv0, st0, conv_w, a_log, dt_bias, norm_w, eab, c):
    bsz, t, _ = pc.shape
    bb, tt = _mixer_tiling(bsz, t, c)
    vec = lambda w: _const_spec((1, w))
    state = pl.BlockSpec((bb, N_PAIRS, LANES, LANES), lambda b, i: (b, 0, 0, 0))
    return pl.pallas_call(
        functools.partial(_gdn_kernel, c=c),
        grid=(bsz // bb, t // tt),
        in_specs=[pl.BlockSpec((bb, tt, C_MAIN), lambda b, i: (b, i, 0)),
                  pl.BlockSpec((bb, tt, LANES), lambda b, i: (b, i, 0)),
                  pl.BlockSpec((bb, SUBLANES, 3 * MIX), lambda b, i: (b, 0, 0)), state,
                  _const_spec((CONV_W, 3 * MIX)), vec(MIX), vec(MIX), vec(MIX), _const_spec(eab.shape)],
        out_specs=[pl.BlockSpec((bb, tt, MIX), lambda b, i: (b, i, 0)), state],
        out_shape=[jax.ShapeDtypeStruct((bsz, t, MIX), F32),
                   jax.ShapeDtypeStruct((bsz, N_PAIRS, LANES, LANES), F32)],
        scratch_shapes=[pltpu.VMEM((bb, tt + SUBLANES, 3 * MIX), F32),
                        pltpu.VMEM((bb * N_PAIRS, LANES, LANES), F32)],
        compiler_params=_params("arbitrary", "arbitrary"),
        name="gdn",
    )(pc, pab, conv0, st0, conv_w, a_log, dt_bias, norm_w, eab)


def _band_kernel(q_ref, k_ref, v_ref, bias_ref, o_ref, *, wn, banded):
    bb, cq, _ = q_ref.shape
    ci = pl.program_id(1)
    start = pl.multiple_of(ci * cq, cq)
    units = [(b, j) for b in range(bb) for j in range(N_PAIRS)]
    q = q_ref[...] * (HEAD_DIM ** -0.5)
    q_g = jnp.stack([_stack(q[b, :, j * LANES:(j + 1) * LANES]) for b, j in units]).astype(BF16)
    window = lambda ref: jnp.stack([ref[b, pl.ds(start, wn), j * LANES:(j + 1) * LANES] for b, j in units])
    s = _bmm_nt(q_g, window(k_ref))
    s = (s.reshape(bb, N_PAIRS, 2 * cq, wn) + bias_ref[...]).reshape(bb * N_PAIRS, 2 * cq, wn)
    if banded:
        qc = (_iota((2 * cq, wn), 0) % cq) // CHUNK
        kj = _iota((2 * cq, wn), 1)
        kc = kj // CHUNK
        valid = (kc >= qc) & (kc <= qc + BAND // CHUNK) & (kj >= BAND - ci * cq)
        s = jnp.where(valid, s, -1e30)
    e = jnp.exp(s - jnp.max(s, axis=-1, keepdims=True))
    pv = _bmm(e, window(v_ref)) / jnp.sum(e, axis=-1, keepdims=True)
    lane = _iota((cq, LANES), 1)
    o_ref[...] = jnp.stack([
        jnp.concatenate([jnp.where(lane < HEAD_DIM, pv[b * N_PAIRS + j, :cq], pv[b * N_PAIRS + j, cq:])
                         for j in range(N_PAIRS)], axis=1) for b in range(bb)])


def _band(q, k_ext, v_ext, bias, cq, bb, banded):
    bsz, t, _ = q.shape
    text = k_ext.shape[1]
    wn = text - t + cq
    ext = pl.BlockSpec((bb, text, MIX), lambda b, i: (b, 0, 0))
    bias = bias.reshape(N_PAIRS, 2 * cq, wn)
    return pl.pallas_call(
        functools.partial(_band_kernel, wn=wn, banded=banded),
        grid=(bsz // bb, t // cq),
        in_specs=[pl.BlockSpec((bb, cq, MIX), lambda b, i: (b, i, 0)), ext, ext, _const_spec(bias.shape)],
        out_specs=pl.BlockSpec((bb, cq, MIX), lambda b, i: (b, i, 0)),
        out_shape=jax.ShapeDtypeStruct((bsz, t, MIX), F32),
        compiler_params=_params("arbitrary", "arbitrary"),
        name="band",
    )(q, k_ext, v_ext, bias)


def _merge_kernel(x_ref, oa_ref, ob_ref, oc_ref, sh_ref, sc_ref, gm_ref, sh2_ref, sc2_ref, nw_ref, nw2_ref,
                  wgate_ref, bgate_ref, wbr_ref, wout_ref, wrt_ref, rb_ref,
                  x1_ref, h2_ref, comb_ref):
    x = x_ref[...]
    h = _norm_mod(x, nw_ref[...], sc_ref[0], sh_ref[0]).astype(BF16)
    gates = _sigmoid(_dot(h, wgate_ref[...]) + bgate_ref[...])
    mixed = None
    for i, o_ref in enumerate((oa_ref, ob_ref, oc_ref)):
        term = gates[:, i * D_MODEL:(i + 1) * D_MODEL] * _bdot(o_ref[...], wbr_ref[i])
        mixed = term if mixed is None else mixed + term
    x1 = x + gm_ref[0] * _bdot(mixed, wout_ref[...])
    x1_ref[...] = x1
    h2 = _norm_mod(x1, nw2_ref[...], sc2_ref[0], sh2_ref[0])
    h2_ref[...] = h2.astype(BF16)

    scores = _sigmoid(_dot_nt(wrt_ref[...], h2, HIGHEST))
    sel = scores + rb_ref[...]
    tm = scores.shape[1]
    per = N_EXPERTS // N_GROUPS
    best_val, best = None, None
    for g in range(N_GROUPS):
        rows = [sel[g * per + i:g * per + i + 1, :] for i in range(per)]
        top2 = None
        for i in range(per):
            for i2 in range(i + 1, per):
                pair = rows[i] + rows[i2]
                top2 = pair if top2 is None else jnp.maximum(top2, pair)
        if g == 0:
            best_val, best = top2, jnp.zeros((1, tm), jnp.int32)
        else:
            better = top2 > best_val
            best = jnp.where(better, g, best)
            best_val = jnp.where(better, top2, best_val)
    eidx = _iota((N_EXPERTS, tm), 0)
    cand = jnp.where(eidx // per == best, sel, -jnp.inf)
    m1 = jnp.max(cand, axis=0, keepdims=True)
    i1 = jnp.min(jnp.where(cand == m1, eidx, N_EXPERTS), axis=0, keepdims=True)
    cand2 = jnp.where(eidx == i1, -jnp.inf, cand)
    m2 = jnp.max(cand2, axis=0, keepdims=True)
    i2 = jnp.min(jnp.where(cand2 == m2, eidx, N_EXPERTS), axis=0, keepdims=True)
    w1 = jnp.sum(jnp.where(eidx == i1, scores, 0.0), axis=0, keepdims=True)
    w2 = jnp.sum(jnp.where(eidx == i2, scores, 0.0), axis=0, keepdims=True)
    den = w1 + w2
    comb_ref[...] = jnp.where(eidx == i1, w1 / den, 0.0) + jnp.where(eidx == i2, w2 / den, 0.0)


def _merge(x, oa, ob, oc, mods, norm_w, norm2_w, w_gate, b_gate, w_branch, w_out, w_router_t, router_bias,
           tm, tiles_per_group):
    n = x.shape[0]
    row = lambda w: pl.BlockSpec((tm, w), lambda i: (i, 0))
    mod_specs = [_mod_spec(m, tm, tiles_per_group) for m in mods]
    return pl.pallas_call(
        _merge_kernel,
        grid=(n // tm,),
        in_specs=[row(D_MODEL), row(MIX), row(MIX), row(MIX)] + mod_specs
                 + [_const_spec((1, D_MODEL)), _const_spec((1, D_MODEL)), _const_spec(w_gate.shape),
                    _const_spec(b_gate.shape), _const_spec(w_branch.shape), _const_spec(w_out.shape),
                    _const_spec(w_router_t.shape), _const_spec(router_bias.shape)],
        out_specs=[row(D_MODEL), row(D_MODEL), pl.BlockSpec((N_EXPERTS, tm), lambda i: (0, i))],
        out_shape=[jax.ShapeDtypeStruct((n, D_MODEL), F32), jax.ShapeDtypeStruct((n, D_MODEL), BF16),
                   jax.ShapeDtypeStruct((N_EXPERTS, n), F32)],
        compiler_params=_params("arbitrary"),
        name="merge",
    )(x, oa, ob, oc, *mods, norm_w, norm2_w, w_gate, b_gate, w_branch, w_out, w_router_t, router_bias)


def _moe_kernel(x1_ref, h2_ref, comb_ref, gf_ref, wg_ref, wu_ref, wd_ref, o_ref):
    h2 = h2_ref[...]
    comb = jnp.transpose(comb_ref[...])
    acc = jnp.zeros(o_ref.shape, F32)
    for e in range(N_EXPERTS):
        hid = _silu(_dot(h2, wg_ref[e])) * _dot(h2, wu_ref[e]) * comb[:, e:e + 1]
        acc = acc + _dot(hid.astype(BF16), wd_ref[e])
    o_ref[...] = x1_ref[...] + gf_ref[0] * acc


def _moe(x1, h2, comb, g_ffn, wg, wu, wd, tm, tiles_per_group):
    n = x1.shape[0]
    row = lambda w: pl.BlockSpec((tm, w), lambda i: (i, 0))
    return pl.pallas_call(
        _moe_kernel,
        grid=(n // tm,),
        in_specs=[row(D_MODEL), row(D_MODEL), pl.BlockSpec((N_EXPERTS, tm), lambda i: (0, i)),
                  _mod_spec(g_ffn, tm, tiles_per_group),
                  _const_spec(wg.shape), _const_spec(wu.shape), _const_spec(wd.shape)],
        out_specs=row(D_MODEL),
        out_shape=jax.ShapeDtypeStruct((n, D_MODEL), F32),
        compiler_params=_params("arbitrary"),
        name="moe",
    )(x1, h2, comb, g_ffn, wg, wu, wd)


def _pair_state(s):
    bsz = s.shape[0]
    s = s.reshape(bsz, N_PAIRS, 2, HEAD_DIM, HEAD_DIM)
    z = jnp.zeros_like(s[:, :, 0])
    top = jnp.concatenate([s[:, :, 0], z], axis=-1)
    bot = jnp.concatenate([z, s[:, :, 1]], axis=-1)
    return jnp.concatenate([top, bot], axis=-2)


def _unpair_state(s):
    bsz = s.shape[0]
    return jnp.stack([s[:, :, :HEAD_DIM, :HEAD_DIM], s[:, :, HEAD_DIM:, HEAD_DIM:]], axis=2).reshape(
        bsz, N_HEADS, HEAD_DIM, HEAD_DIM)


def _rel_bias(table, cq, wn):
    span = cq - 1 + wn
    offs = jnp.arange(-(cq - 1), wn + 1)
    diag = table[:, jnp.clip(BAND - offs, -MAX_REL, MAX_REL) + MAX_REL].astype(F32)
    skew = jnp.tile(diag, (1, cq))[:, :cq * span].reshape(-1, cq, span)
    return skew[:, :, cq - 1:]


def _layer(x, mod, state, wts, tm):
    bsz, t, _ = x.shape
    n = bsz * t
    c = min(CHUNK, t)
    shift0, wkv0, k_past, v_past, conv0, s0 = state
    if t % tm == 0:
        tiles_per_group = t // tm
        mods = [m.reshape(bsz, 1, D_MODEL) for m in jnp.split(mod, 6, axis=-1)]
    else:
        tiles_per_group = 1
        mods = [jnp.repeat(m, t, axis=0).reshape(n // tm, tm, D_MODEL) for m in jnp.split(mod, 6, axis=-1)]
    sh_mix, sc_mix, g_mix, sh_ffn, sc_ffn, g_ffn = mods
    xf = x.reshape(n, D_MODEL)

    pa, q, k, v, pc, pab = _inproj(xf, sh_mix, sc_mix, wts["norm_mix_w"], wts["wa"], wts["wb"], wts["wc"],
                                   wts["wab"], wts["q_norm"], wts["k_norm"], tm, tiles_per_group)
    pa = pa.reshape(bsz, t, A_COLS)
    q, k, v = (z.reshape(bsz, t, MIX) for z in (q, k, v))
    pc = pc.reshape(bsz, t, C_MAIN)

    oa, wkv = _rwkv(pa, shift0.reshape(bsz, 1, A_COLS), _pair_state(jnp.swapaxes(wkv0, -1, -2)),
                    wts["mu"], wts["w0"], wts["a0"], wts["wwa"], wts["g_up"], wts["k_k"], wts["k_a"], wts["r_k"],
                    wts["gn_w"], wts["gn_b"], c)
    new_wkv = jnp.swapaxes(_unpair_state(wkv), -1, -2)

    if k_past is None:
        pad = lambda z: jnp.pad(z.astype(BF16), ((0, 0), (BAND, 0), (0, 0)))
        ob = _band(q, pad(k), pad(v), _rel_bias(wts["rel_bias"], BAND_QUERIES, BAND + BAND_QUERIES),
                   BAND_QUERIES, 1, True)
        new_k, new_v = k[:, -BAND:], v[:, -BAND:]
    else:
        past = k_past.shape[1]
        cat = lambda zp, z: jnp.concatenate([zp.reshape(bsz, past, MIX), z], axis=1).astype(BF16)
        ob = _band(q, cat(k_past, k), cat(v_past, v), _rel_bias(wts["rel_bias"], t, past + t), t,
                   min(bsz, MIXER_ROWS // t), False)
        new_k, new_v = k, v

    conv_pad = jnp.pad(conv0, ((0, 0), (SUBLANES - (CONV_W - 1), 0), (0, 0)))
    oc, s_new = _gdn(pc, pab.reshape(bsz, t, LANES), conv_pad, _pair_state(s0), wts["conv_w"], wts["a_log"],
                     wts["dt_bias"], wts["gdn_norm_w"], wts["eab"], c)
    new_conv = jnp.concatenate([conv0, pc[:, :, :3 * MIX]], axis=1)[:, -(CONV_W - 1):]

    x1, h2, comb = _merge(xf, oa.reshape(n, MIX), ob.reshape(n, MIX), oc.reshape(n, MIX),
                          [sh_mix, sc_mix, g_mix, sh_ffn, sc_ffn], wts["norm_mix_w"], wts["norm_ffn_w"],
                          wts["w_gate"], wts["b_gate"], wts["w_branch"], wts["w_out"], wts["w_router_t"],
                          wts["router_bias"], tm, tiles_per_group)
    x2 = _moe(x1, h2, comb, g_ffn, wts["wg"], wts["wu"], wts["wd"], tm, tiles_per_group)
    heads = lambda z: z.reshape(bsz, -1, N_HEADS, HEAD_DIM)
    return x2.reshape(bsz, t, D_MODEL), (pa[:, -1], new_wkv, heads(new_k), heads(new_v), new_conv,
                                         _unpair_state(s_new))


def _prepare_layer(l, w_in, norm_mix_w, norm_ffn_w, rwkv_mu, rwkv_w0, rwkv_w_up, rwkv_a0, rwkv_a_up, rwkv_g_up,
                   rwkv_k_k, rwkv_k_a, rwkv_r_k, rwkv_gn_w, rwkv_gn_b, band_q_norm, band_k_norm, band_rel_bias,
                   gdn_conv_w, gdn_a_log, gdn_dt_bias, gdn_norm_w, w_branch, w_gate, b_gate, w_out,
                   w_router, router_bias, w_exp_gate, w_exp_up, w_exp_down):
    row = lambda z: z.reshape(1, -1).astype(F32)
    per_head = lambda z: jnp.repeat(z, HEAD_DIM).reshape(1, MIX)
    win = w_in[l].astype(BF16)
    c0 = A_COLS + B_COLS
    wab = jnp.pad(win[:, c0 + C_MAIN:], ((0, 0), (0, LANES - 2 * N_HEADS)))
    zeros = jnp.zeros((HEAD_DIM, MIX), F32)
    wwa = jnp.concatenate([jnp.concatenate([rwkv_w_up[l], zeros], axis=1),
                           jnp.concatenate([zeros, rwkv_a_up[l]], axis=1)], axis=0)
    head_of_lane = jnp.arange(MIX) // HEAD_DIM
    src = jnp.arange(LANES)[:, None]
    eab = jnp.concatenate([(src == head_of_lane[None, :]), (src == N_HEADS + head_of_lane[None, :])],
                          axis=1).astype(BF16)
    return dict(
        norm_mix_w=row(norm_mix_w[l]), norm_ffn_w=row(norm_ffn_w[l]),
        wa=win[:, :A_COLS], wb=win[:, A_COLS:c0], wc=win[:, c0:c0 + C_MAIN], wab=wab,
        q_norm=row(jnp.tile(band_q_norm[l], N_HEADS)), k_norm=row(jnp.tile(band_k_norm[l], N_HEADS)),
        mu=row(rwkv_mu[l]), w0=row(rwkv_w0[l]), a0=row(rwkv_a0[l]), wwa=wwa, g_up=rwkv_g_up[l],
        k_k=row(rwkv_k_k[l]), k_a=row(rwkv_k_a[l]), r_k=row(rwkv_r_k[l]), gn_w=row(rwkv_gn_w[l]),
        gn_b=row(rwkv_gn_b[l]), rel_bias=band_rel_bias[l],
        conv_w=gdn_conv_w[l], a_log=per_head(gdn_a_log[l]), dt_bias=per_head(gdn_dt_bias[l]),
        gdn_norm_w=row(jnp.tile(gdn_norm_w[l], N_HEADS)), eab=eab,
        w_gate=w_gate[l].astype(BF16), b_gate=row(b_gate[l]), w_branch=w_branch[l].astype(BF16),
        w_out=w_out[l].astype(BF16), w_router_t=jnp.transpose(w_router), router_bias=router_bias.reshape(-1, 1),
        wg=w_exp_gate[l].astype(BF16), wu=w_exp_up[l].astype(BF16), wd=w_exp_down[l].astype(BF16),
    )


def kernel(x_prompt, x_sample, c_prompt, c_sample, state_rwkv_shift, state_rwkv_wkv, cache_band_k, cache_band_v, state_gdn_conv, state_gdn_S, w_ada, b_ada, norm_mix_w, norm_ffn_w, w_in, rwkv_mu, rwkv_w0, rwkv_w_up, rwkv_a0, rwkv_a_up, rwkv_g_up, rwkv_k_k, rwkv_k_a, rwkv_r_k, rwkv_gn_w, rwkv_gn_b, band_q_norm, band_k_norm, band_rel_bias, gdn_conv_w, gdn_a_log, gdn_dt_bias, gdn_norm_w, w_branch, w_gate, b_gate, w_out, w_router, router_bias, w_exp_gate, w_exp_up, w_exp_down):
    depth = w_ada.shape[0]
    bsz = x_prompt.shape[0]
    mod_p = _adaln(c_prompt, w_ada, b_ada)
    mod_s = _adaln(c_sample, w_ada, b_ada)
    zero_state = (jnp.zeros((bsz, A_COLS), F32), jnp.zeros((bsz, N_HEADS, HEAD_DIM, HEAD_DIM), F32), None, None,
                  jnp.zeros((bsz, CONV_W - 1, 3 * MIX), F32), jnp.zeros((bsz, N_HEADS, HEAD_DIM, HEAD_DIM), F32))
    tm = 256
    xp, xs = x_prompt, x_sample
    new_p, new_s = [], []
    for l in range(depth):
        wts = _prepare_layer(l, w_in, norm_mix_w, norm_ffn_w, rwkv_mu, rwkv_w0, rwkv_w_up, rwkv_a0, rwkv_a_up,
                             rwkv_g_up, rwkv_k_k, rwkv_k_a, rwkv_r_k, rwkv_gn_w, rwkv_gn_b, band_q_norm,
                             band_k_norm, band_rel_bias, gdn_conv_w, gdn_a_log, gdn_dt_bias, gdn_norm_w,
                             w_branch, w_gate, b_gate, w_out, w_router, router_bias, w_exp_gate, w_exp_up,
                             w_exp_down)
        xp, st_p = _layer(xp, mod_p[l], zero_state, wts, tm)
        xs, st_s = _layer(xs, mod_s[l], (state_rwkv_shift[l], state_rwkv_wkv[l], cache_band_k[l], cache_band_v[l],
                                         state_gdn_conv[l], state_gdn_S[l]), wts, tm)
        new_p.append(st_p)
        new_s.append(st_s)
    p_out = [jnp.stack(z, axis=0) for z in zip(*new_p)]
    s_out = [jnp.stack(z, axis=0) for z in zip(*new_s)]
    return (xp, xs, *p_out, *s_out)
```

```python
import functools
import math

import jax
import jax.numpy as jnp
from jax import lax
from jax.experimental import pallas as pl
from jax.experimental.pallas import tpu as pltpu

F32 = jnp.float32
BF16 = jnp.bfloat16
HIGHEST = lax.Precision.HIGHEST

D_MODEL = 1024
MIX = 512
HEAD_DIM = 64
N_HEADS = MIX // HEAD_DIM
LANES = 128
N_PAIRS = MIX // LANES
SUBLANES = 8
CHUNK = 64
TOKEN_TILE = 512
MIXER_ROWS = 256
BAND_QUERIES = 2 * CHUNK
A_COLS = 3 * MIX + 64 + 64 + 128
B_COLS = 3 * MIX
C_MAIN = 4 * MIX
CONV_W = 4
BAND = 8 * CHUNK
MAX_REL = 2 * CHUNK
N_EXPERTS = 16
N_GROUPS = 4
EXPERT_DIM = D_MODEL // 4
RMS_EPS = 1e-6
A_GN_EPS = 64e-5
A_DECAY_SCALE = math.exp(-0.5)
VMEM_LIMIT_BYTES = 56 * 1024 * 1024


def _params(*sem):
    return pltpu.CompilerParams(dimension_semantics=sem, vmem_limit_bytes=VMEM_LIMIT_BYTES)


def _dot(a, b, precision=None):
    return lax.dot_general(a, b, (((1,), (0,)), ((), ())), precision=precision, preferred_element_type=F32)


def _dot_nt(a, b, precision=None):
    return lax.dot_general(a, b, (((1,), (1,)), ((), ())), precision=precision, preferred_element_type=F32)


def _bdot(a, b):
    return _dot(a.astype(BF16), b.astype(BF16))


def _bdot_nt(a, b):
    return _dot_nt(a.astype(BF16), b.astype(BF16))


def _hdot(a, b):
    return _dot(a, b, HIGHEST)


def _bmm(a, b):
    return lax.dot_general(a.astype(BF16), b.astype(BF16), (((2,), (1,)), ((0,), (0,))),
                           preferred_element_type=F32)


def _bmm_nt(a, b):
    return lax.dot_general(a.astype(BF16), b.astype(BF16), (((2,), (2,)), ((0,), (0,))),
                           preferred_element_type=F32)


def _sigmoid(x):
    return 1.0 / (1.0 + jnp.exp(-x))


def _silu(x):
    return x * _sigmoid(x)


def _softplus(x):
    return jnp.maximum(x, 0.0) + jnp.log1p(jnp.exp(-jnp.abs(x)))


def _iota(shape, dim):
    return lax.broadcasted_iota(jnp.int32, shape, dim)


def _bf16_terms(x, terms):
    parts = []
    for _ in range(terms):
        part = x.astype(BF16)
        parts.append(part)
        x = x - part.astype(F32)
    return parts


def _dot_01(x, m, terms):
    out = None
    for part in _bf16_terms(x, terms):
        term = _dot(part, m)
        out = term if out is None else out + term
    return out


def _dot_01_left(m, x, terms):
    out = None
    for part in _bf16_terms(x, terms):
        term = _dot(m, part)
        out = term if out is None else out + term
    return out


def _head_ones():
    return (_iota((LANES, LANES), 0) // HEAD_DIM == _iota((LANES, LANES), 1) // HEAD_DIM).astype(BF16)


def _head_sum(x, ones):
    parts = [_dot_01(x[:, g * LANES:(g + 1) * LANES], ones, 2) for g in range(x.shape[1] // LANES)]
    return parts[0] if len(parts) == 1 else jnp.concatenate(parts, axis=1)


def _stack(x):
    lane = _iota(x.shape, 1)
    return jnp.concatenate([jnp.where(lane < HEAD_DIM, x, 0.0), jnp.where(lane >= HEAD_DIM, x, 0.0)], axis=0)


def _group(x, c):
    return jnp.stack([_stack(x[u * c:(u + 1) * c, j * LANES:(j + 1) * LANES])
                      for u in range(x.shape[0] // c) for j in range(N_PAIRS)])


def _ungroup(xs, c):
    return jnp.concatenate(
        [jnp.concatenate([xs[u * N_PAIRS + j, :c] + xs[u * N_PAIRS + j, c:] for j in range(N_PAIRS)], axis=1)
         for u in range(xs.shape[0] // N_PAIRS)], axis=0)


def _transpose_tiles(xs):
    return jnp.stack([jnp.transpose(xs[g]) for g in range(xs.shape[0])])


def _chunk_last(x, c):
    return jnp.concatenate([jnp.broadcast_to(x[(u + 1) * c - 1:(u + 1) * c], (c, x.shape[1]))
                            for u in range(x.shape[0] // c)], axis=0)


def _column_scale(row_values, c):
    tiles = []
    for u in range(row_values.shape[0] // c):
        last = row_values[(u + 1) * c - 1:(u + 1) * c]
        for j in range(N_PAIRS):
            tiles.append(jnp.transpose(jnp.broadcast_to(last[:, j * LANES:(j + 1) * LANES], (LANES, LANES))))
    return jnp.stack(tiles)


def _chunk_tril(rows, c):
    row, col = _iota((rows, rows), 0), _iota((rows, rows), 1)
    return ((col <= row) & (row // c == col // c)).astype(BF16)


def _tri_masks(n, c):
    row, col = _iota((n, n), 0), _iota((n, n), 1)
    same = row // c == col // c
    return same & (col < row), same & (col <= row)


def _unit_lower_inverse(a, c):
    n = a.shape[1]
    row, col = _iota((n, n), 0), _iota((n, n), 1)
    eye = (row == col).astype(F32)
    same = lambda s: row // s == col // s
    a0 = jnp.where(same(SUBLANES), a, 0.0)
    a2 = _bmm(a0, a0)
    a4 = _bmm(a2, a2)
    x = _bmm(_bmm(eye - a0, eye + a2), eye + a4)
    s = SUBLANES
    while s < c:
        off = jnp.where(same(2 * s) & jnp.logical_not(same(s)), a, 0.0)
        x = x - _bmm(_bmm(x, off), x)
        s *= 2
    return x


def _adaln_kernel(c_ref, w_ref, b_ref, o_ref):
    o_ref[0] = _hdot(_silu(c_ref[...]), w_ref[0]) + b_ref[0]


def _adaln(c, w_ada, b_ada):
    depth, bsz, tn = w_ada.shape[0], c.shape[0], 768
    return pl.pallas_call(
        _adaln_kernel,
        grid=(depth, 6 * D_MODEL // tn),
        in_specs=[pl.BlockSpec((bsz, D_MODEL), lambda l, j: (0, 0)),
                  pl.BlockSpec((1, D_MODEL, tn), lambda l, j: (l, 0, j)),
                  pl.BlockSpec((1, 1, tn), lambda l, j: (l, 0, j))],
        out_specs=pl.BlockSpec((1, bsz, tn), lambda l, j: (l, 0, j)),
        out_shape=jax.ShapeDtypeStruct((depth, bsz, 6 * D_MODEL), F32),
        compiler_params=_params("arbitrary", "arbitrary"),
        name="adaln",
    )(c, w_ada, b_ada.reshape(depth, 1, 6 * D_MODEL))


def _norm_mod(x, norm_w, scale, shift):
    y = x * lax.rsqrt(jnp.mean(x * x, axis=-1, keepdims=True) + RMS_EPS)
    return y * norm_w * (1.0 + scale) + shift


def _inproj_kernel(x_ref, sh_ref, sc_ref, nw_ref, wa_ref, wb_ref, wc_ref, wab_ref, qn_ref, kn_ref,
                   pa_ref, q_ref, k_ref, v_ref, pc_ref, pab_ref):
    h = _norm_mod(x_ref[...], nw_ref[...], sc_ref[0], sh_ref[0]).astype(BF16)
    pa_ref[...] = _dot(h, wa_ref[...])
    pb = _dot(h, wb_ref[...])
    ones = _head_ones()

    def head_rms(y, w):
        return y * lax.rsqrt(_head_sum(y * y, ones) * (1.0 / HEAD_DIM) + RMS_EPS) * w

    q_ref[...] = head_rms(pb[:, :MIX], qn_ref[...])
    k_ref[...] = head_rms(pb[:, MIX:2 * MIX], kn_ref[...])
    v_ref[...] = pb[:, 2 * MIX:]
    pc_ref[...] = _dot(h, wc_ref[...])
    pab_ref[...] = _dot(h, wab_ref[...])


def _mod_spec(mod, tm, tiles_per_group):
    rows = mod.shape[1]
    return pl.BlockSpec((1, rows, D_MODEL), lambda i: (i // tiles_per_group, 0, 0))


def _const_spec(shape):
    zeros = (0,) * len(shape)
    return pl.BlockSpec(shape, lambda *_: zeros, pipeline_mode=pl.Buffered(1))


def _inproj(x, shift, scale, norm_w, wa, wb, wc, wab, q_norm, k_norm, tm, tiles_per_group):
    n = x.shape[0]
    row = lambda w: pl.BlockSpec((tm, w), lambda i: (i, 0))
    widths = (A_COLS, MIX, MIX, MIX, C_MAIN, LANES)
    return pl.pallas_call(
        _inproj_kernel,
        grid=(n // tm,),
        in_specs=[row(D_MODEL), _mod_spec(shift, tm, tiles_per_group), _mod_spec(scale, tm, tiles_per_group),
                  _const_spec((1, D_MODEL)), _const_spec(wa.shape), _const_spec(wb.shape), _const_spec(wc.shape),
                  _const_spec(wab.shape), _const_spec((1, MIX)), _const_spec((1, MIX))],
        out_specs=[row(w) for w in widths],
        out_shape=[jax.ShapeDtypeStruct((n, w), F32) for w in widths],
        compiler_params=_params("arbitrary"),
        name="inproj",
    )(x, shift, scale, norm_w, wa, wb, wc, wab, q_norm, k_norm)


def _rwkv_kernel(p_ref, shift0_ref, st0_ref, mu_ref, w0_ref, a0_ref, wwa_ref, gup_ref, kk_ref, ka_ref, rk_ref,
                 gnw_ref, gnb_ref, o_ref, st_out_ref, ext_scr, st_scr, *, c):
    bb, tt, _ = p_ref.shape
    rows, nch, n = bb * tt, tt // c, 2 * c
    assert bb == 1 or nch == 1
    ci = pl.program_id(1)

    @pl.when(ci == 0)
    def _():
        ext_scr[:, SUBLANES - 1:SUBLANES, :] = shift0_ref[...]
        st_scr[...] = st0_ref[...].reshape(bb * N_PAIRS, LANES, LANES)

    ext_scr[:, SUBLANES:SUBLANES + tt, :] = p_ref[...]
    p = p_ref[...].reshape(rows, A_COLS)
    p_prev = ext_scr[:, SUBLANES - 1:SUBLANES - 1 + tt, :].reshape(rows, A_COLS)
    ext_scr[:, 0:SUBLANES, :] = ext_scr[:, tt:tt + SUBLANES, :]
    xs = p + (p_prev - p) * mu_ref[...]
    r, k, v = xs[:, :MIX], xs[:, MIX:2 * MIX], xs[:, 2 * MIX:3 * MIX]
    lora_in = xs[:, 3 * MIX:3 * MIX + LANES]
    gd = xs[:, 3 * MIX + LANES:]
    lane = _iota((1, LANES), 1)
    lora = _bdot(jnp.where(lane < HEAD_DIM, jnp.tanh(lora_in), lora_in), wwa_ref[...])
    log_w = -A_DECAY_SCALE * _sigmoid(w0_ref[...] + lora[:, :MIX])
    a = _sigmoid(a0_ref[...] + lora[:, MIX:])
    g = _bdot(_sigmoid(gd), gup_ref[...])
    ones = _head_ones()
    kk_raw = k * kk_ref[...]
    kk = kk_raw * lax.rsqrt(_head_sum(kk_raw * kk_raw, ones) + RMS_EPS)
    k = k * (1.0 + (a - 1.0) * ka_ref[...])
    b = kk * a
    cum = _dot_01_left(_chunk_tril(rows, c), log_w, 3)
    cum_last = _chunk_last(cum, c)
    e_cum, e_neg = jnp.exp(cum), jnp.exp(-cum)
    e_prev, e_tail = jnp.exp(cum - log_w), jnp.exp(cum_last - cum)
    strict, incl = _tri_masks(n, c)

    kk_g, r_g, v_g = _group(kk * e_prev, c), _group(r * e_cum, c), _group(v, c)
    amat = _bmm_nt(jnp.concatenate([kk_g, r_g], axis=1),
                   jnp.concatenate([_group(k * e_neg, c), _group(b * e_neg, c)], axis=1))
    a_kk = jnp.where(strict, amat[:, :n, :n], 0.0)
    a_bk = jnp.where(strict, amat[:, :n, n:], 0.0)
    a_rk = jnp.where(incl, amat[:, n:, :n], 0.0)
    a_rb = jnp.where(incl, amat[:, n:, n:], 0.0)
    t_inv = _unit_lower_inverse(a_bk, c)
    av = _bmm(jnp.concatenate([a_kk, a_rk], axis=1), v_g)
    wu = _bmm(t_inv, jnp.concatenate([kk_g, av[:, :n]], axis=2))
    kd_v = _bmm(_transpose_tiles(_group(k * e_tail, c)), v_g)
    bd_t = _transpose_tiles(_group(b * e_tail, c))
    decay = _column_scale(e_cum, c)

    st = st_scr[...]
    per_step = bb * N_PAIRS
    o_tiles = []
    for ch in range(nch):
        sel = slice(ch * per_step, (ch + 1) * per_step)
        zo = _bmm(jnp.concatenate([wu[sel, :, :LANES], r_g[sel]], axis=1), st)
        z = wu[sel, :, LANES:] + zo[:, :n]
        o_tiles.append(zo[:, n:] + av[sel, n:] - _bmm(a_rb[sel], z))
        st = decay[sel] * st + kd_v[sel] - _bmm(bd_t[sel], z)
    st_scr[...] = st
    o = _ungroup(o_tiles[0] if nch == 1 else jnp.concatenate(o_tiles, axis=0), c)

    mean = _head_sum(o, ones) * (1.0 / HEAD_DIM)
    cen = o - mean
    var = _head_sum(cen * cen, ones) * (1.0 / HEAD_DIM)
    o = cen * lax.rsqrt(var + A_GN_EPS) * gnw_ref[...] + gnb_ref[...]
    bonus = _head_sum(r * k * rk_ref[...], ones) * v
    o_ref[...] = ((o + bonus) * g).reshape(bb, tt, MIX)
    st_out_ref[...] = st.reshape(bb, N_PAIRS, LANES, LANES)


def _mixer_tiling(bsz, t, c):
    if t > c:
        return 1, min(t, MIXER_ROWS)
    return min(bsz, MIXER_ROWS // t), t


def _rwkv(pa, shift0, st0, mu, w0, a0, wwa, gup, k_k, k_a, r_k, gn_w, gn_b, c):
    bsz, t, _ = pa.shape
    bb, tt = _mixer_tiling(bsz, t, c)
    vec = lambda w: _const_spec((1, w))
    state = pl.BlockSpec((bb, N_PAIRS, LANES, LANES), lambda b, i: (b, 0, 0, 0))
    return pl.pallas_call(
        functools.partial(_rwkv_kernel, c=c),
        grid=(bsz // bb, t // tt),
        in_specs=[pl.BlockSpec((bb, tt, A_COLS), lambda b, i: (b, i, 0)),
                  pl.BlockSpec((bb, 1, A_COLS), lambda b, i: (b, 0, 0)), state,
                  vec(A_COLS), vec(MIX), vec(MIX), _const_spec(wwa.shape), _const_spec(gup.shape),
                  vec(MIX), vec(MIX), vec(MIX), vec(MIX), vec(MIX)],
        out_specs=[pl.BlockSpec((bb, tt, MIX), lambda b, i: (b, i, 0)), state],
        out_shape=[jax.ShapeDtypeStruct((bsz, t, MIX), F32),
                   jax.ShapeDtypeStruct((bsz, N_PAIRS, LANES, LANES), F32)],
        scratch_shapes=[pltpu.VMEM((bb, tt + SUBLANES, A_COLS), F32),
                        pltpu.VMEM((bb * N_PAIRS, LANES, LANES), F32)],
        compiler_params=_params("arbitrary", "arbitrary"),
        name="rwkv",
    )(pa, shift0, st0, mu, w0, a0, wwa, gup, k_k, k_a, r_k, gn_w, gn_b)


def _gdn_kernel(pc_ref, pab_ref, conv0_ref, st0_ref, convw_ref, alog_ref, dtb_ref, nw_ref, eab_ref,
                o_ref, st_out_ref, ext_scr, st_scr, *, c):
    bb, tt, _ = pc_ref.shape
    rows, nch, n = bb * tt, tt // c, 2 * c
    assert bb == 1 or nch == 1
    w3 = 3 * MIX
    ci = pl.program_id(1)

    @pl.when(ci == 0)
    def _():
        ext_scr[:, 0:SUBLANES, :] = conv0_ref[...]
        st_scr[...] = st0_ref[...].reshape(bb * N_PAIRS, LANES, LANES)

    ext_scr[:, SUBLANES:SUBLANES + tt, :] = pc_ref[:, :, :w3]
    conv = ext_scr[:, SUBLANES:SUBLANES + tt, :] * convw_ref[CONV_W - 1:CONV_W, :]
    for s in range(1, CONV_W):
        conv = conv + ext_scr[:, SUBLANES - s:SUBLANES - s + tt, :] * convw_ref[CONV_W - 1 - s:CONV_W - s, :]
    ext_scr[:, 0:SUBLANES, :] = ext_scr[:, tt:tt + SUBLANES, :]
    qkv = _silu(conv.reshape(rows, w3))
    ones = _head_ones()
    l2n = lambda y: y * lax.rsqrt(_head_sum(y * y, ones) + RMS_EPS)
    q = l2n(qkv[:, :MIX]) * (HEAD_DIM ** -0.5)
    k = l2n(qkv[:, MIX:2 * MIX])
    v = qkv[:, 2 * MIX:]
    zgate = pc_ref[:, :, w3:].reshape(rows, MIX)
    ab = _dot_01(pab_ref[...].reshape(rows, LANES), eab_ref[...], 3)
    log_alpha = -jnp.exp(alog_ref[...]) * _softplus(ab[:, :MIX] + dtb_ref[...])
    beta = _sigmoid(ab[:, MIX:])
    gcum = _dot_01_left(_chunk_tril(rows, c), log_alpha, 3)
    eg = jnp.exp(gcum)
    bk = beta * k
    strict, incl = _tri_masks(n, c)

    g_tiles = _group(gcum, c)
    g_col = jnp.stack([g_tiles[i] + pltpu.roll(g_tiles[i], HEAD_DIM, axis=1)
                       for i in range(g_tiles.shape[0])])
    g_row = jnp.stack([jnp.transpose(g_col[i])[:n, :] for i in range(g_col.shape[0])])
    diff = g_col[:, :, :n] - g_row
    dec_s = jnp.where(strict, jnp.exp(jnp.where(strict, diff, 0.0)), 0.0)
    dec_i = jnp.where(incl, jnp.exp(jnp.where(incl, diff, 0.0)), 0.0)
    qg_g = _group(q * eg, c)
    kmat = _bmm_nt(jnp.concatenate([_group(bk, c), _group(q, c)], axis=1), _group(k, c))
    attn = kmat[:, n:] * dec_i
    t_inv = _unit_lower_inverse(kmat[:, :n] * dec_s, c)
    sol = _bmm(t_inv, jnp.concatenate([_group(beta * v, c), _group(bk * eg, c)], axis=2))
    kd_t = _transpose_tiles(_group(k * jnp.exp(_chunk_last(gcum, c) - gcum), c))
    decay = _column_scale(eg, c)

    st = st_scr[...]
    per_step = bb * N_PAIRS
    o_tiles = []
    for ch in range(nch):
        sel = slice(ch * per_step, (ch + 1) * per_step)
        wq = _bmm(jnp.concatenate([sol[sel, :, LANES:], qg_g[sel]], axis=1), st)
        delta = sol[sel, :, :LANES] - wq[:, :n]
        o_tiles.append(wq[:, n:] + _bmm(attn[sel], delta))
        st = decay[sel] * st + _bmm(kd_t[sel], delta)
    st_scr[...] = st
    o = _ungroup(o_tiles[0] if nch == 1 else jnp.concatenate(o_tiles, axis=0), c)

    o = o * lax.rsqrt(_head_sum(o * o, ones) * (1.0 / HEAD_DIM) + RMS_EPS) * nw_ref[...]
    o_ref[...] = (o * _silu(zgate)).reshape(bb, tt, MIX)
    st_out_ref[...] = st.reshape(bb, N_PAIRS, LANES, LANES)


def _gdn(pc, pab, conv0, st0, conv_w, a_log, dt_bias, norm_w, eab, c):
    bsz, t, _ = pc.shape
    bb, tt = _mixer_tiling(bsz, t, c)
    vec = lambda w: _const_spec((1, w))
    state = pl.BlockSpec((bb, N_PAIRS, LANES, LANES), lambda b, i: (b, 0, 0, 0))
    return pl.pallas_call(
        functools.partial(_gdn_kernel, c=c),
        grid=(bsz // bb, t // tt),
        in_specs=[pl.BlockSpec((bb, tt, C_MAIN), lambda b, i: (b, i, 0)),
                  pl.BlockSpec((bb, tt, LANES), lambda b, i: (b, i, 0)),
                  pl.BlockSpec((bb, SUBLANES, 3 * MIX), lambda b, i: (b, 0, 0)), state,
                  _const_spec((CONV_W, 3 * MIX)), vec(MIX), vec(MIX), vec(MIX), _const_spec(eab.shape)],
        out_specs=[pl.BlockSpec((bb, tt, MIX), lambda b, i: (b, i, 0)), state],
        out_shape=[jax.ShapeDtypeStruct((bsz, t, MIX), F32),
                   jax.ShapeDtypeStruct((bsz, N_PAIRS, LANES, LANES), F32)],
        scratch_shapes=[pltpu.VMEM((bb, tt + SUBLANES, 3 * MIX), F32),
                        pltpu.VMEM((bb * N_PAIRS, LANES, LANES), F32)],
        compiler_params=_params("arbitrary", "arbitrary"),
        name="gdn",
    )(pc, pab, conv0, st0, conv_w, a_log, dt_bias, norm_w, eab)


def _band_core(q, k_g, v_g, bias, valid):
    bb, cq, _ = q.shape
    wn = k_g.shape[1]
    q = q * (HEAD_DIM ** -0.5)
    q_g = jnp.stack([_stack(q[b, :, j * LANES:(j + 1) * LANES])
                     for b in range(bb) for j in range(N_PAIRS)]).astype(BF16)
    s = _bmm_nt(q_g, k_g)
    s = (s.reshape(bb, N_PAIRS, 2 * cq, wn) + bias).reshape(bb * N_PAIRS, 2 * cq, wn)
    if valid is not None:
        s = jnp.where(valid, s, -1e30)
    e = jnp.exp(s - jnp.max(s, axis=-1, keepdims=True))
    pv = _bmm(e, v_g) / jnp.sum(e, axis=-1, keepdims=True)
    lane = _iota((cq, LANES), 1)
    return jnp.stack([
        jnp.concatenate([jnp.where(lane < HEAD_DIM, pv[b * N_PAIRS + j, :cq], pv[b * N_PAIRS + j, cq:])
                         for j in range(N_PAIRS)], axis=1) for b in range(bb)])


def _band_prompt_kernel(q_ref, k_ref, v_ref, bias_ref, o_ref):
    cq = q_ref.shape[1]
    wn = BAND + cq
    first = pl.program_id(1) * cq - BAND
    start = pl.multiple_of(jnp.maximum(first, 0), LANES)
    shift = pl.multiple_of(start - first, LANES)
    window = lambda ref: jnp.stack([ref[0, pl.ds(start, wn), j * LANES:(j + 1) * LANES]
                                    for j in range(N_PAIRS)]).astype(BF16)
    qc = (_iota((2 * cq, wn), 0) % cq) // CHUNK
    kc = (_iota((2 * cq, wn), 1) + shift) // CHUNK
    valid = (kc >= qc) & (kc <= qc + BAND // CHUNK)
    o_ref[...] = _band_core(q_ref[...], window(k_ref), window(v_ref), bias_ref[:, :, pl.ds(shift, wn)], valid)


def _band_prompt(q, k, v, table):
    bsz, t, _ = q.shape
    cq = BAND_QUERIES
    bias = _rel_bias(table, cq, 2 * BAND + cq).reshape(N_PAIRS, 2 * cq, 2 * BAND + cq)
    full = pl.BlockSpec((1, t, MIX), lambda b, i: (b, 0, 0))
    return pl.pallas_call(
        _band_prompt_kernel,
        grid=(bsz, t // cq),
        in_specs=[pl.BlockSpec((1, cq, MIX), lambda b, i: (b, i, 0)), full, full, _const_spec(bias.shape)],
        out_specs=pl.BlockSpec((1, cq, MIX), lambda b, i: (b, i, 0)),
        out_shape=jax.ShapeDtypeStruct((bsz, t, MIX), F32),
        compiler_params=_params("arbitrary", "arbitrary"),
        name="band_prompt",
    )(q, k, v, bias)


def _band_sample_kernel(q_ref, kp_ref, vp_ref, kn_ref, vn_ref, bias_ref, o_ref):
    bb = q_ref.shape[0]
    keys = lambda past, new: jnp.stack([
        jnp.concatenate([past[b, :, j * LANES:(j + 1) * LANES], new[b, :, j * LANES:(j + 1) * LANES]], axis=0)
        for b in range(bb) for j in range(N_PAIRS)]).astype(BF16)
    o_ref[...] = _band_core(q_ref[...], keys(kp_ref, kn_ref), keys(vp_ref, vn_ref), bias_ref[...], None)


def _band_sample(q, k, v, k_past, v_past, table):
    bsz, t, _ = q.shape
    past = k_past.shape[1]
    bb = min(bsz, MIXER_ROWS // t)
    bias = _rel_bias(table, t, past + t).reshape(N_PAIRS, 2 * t, past + t)
    new = pl.BlockSpec((bb, t, MIX), lambda b: (b, 0, 0))
    old = pl.BlockSpec((bb, past, MIX), lambda b: (b, 0, 0))
    return pl.pallas_call(
        _band_sample_kernel,
        grid=(bsz // bb,),
        in_specs=[new, old, old, new, new, _const_spec(bias.shape)],
        out_specs=new,
        out_shape=jax.ShapeDtypeStruct((bsz, t, MIX), F32),
        compiler_params=_params("arbitrary"),
        name="band_sample",
    )(q, k_past.reshape(bsz, past, MIX), v_past.reshape(bsz, past, MIX), k, v, bias)


def _merge_kernel(x_ref, oa_ref, ob_ref, oc_ref, sh_ref, sc_ref, gm_ref, sh2_ref, sc2_ref, nw_ref, nw2_ref,
                  wgate_ref, bgate_ref, wbr_ref, wout_ref, wrt_ref, rb_ref,
                  x1_ref, h2_ref, comb_ref):
    x = x_ref[...]
    h = _norm_mod(x, nw_ref[...], sc_ref[0], sh_ref[0]).astype(BF16)
    gates = _sigmoid(_dot(h, wgate_ref[...]) + bgate_ref[...])
    mixed = None
    for i, o_ref in enumerate((oa_ref, ob_ref, oc_ref)):
        term = gates[:, i * D_MODEL:(i + 1) * D_MODEL] * _bdot(o_ref[...], wbr_ref[i])
        mixed = term if mixed is None else mixed + term
    x1 = x + gm_ref[0] * _bdot(mixed, wout_ref[...])
    x1_ref[...] = x1
    h2 = _norm_mod(x1, nw2_ref[...], sc2_ref[0], sh2_ref[0])
    h2_ref[...] = h2.astype(BF16)

    scores = _sigmoid(_dot_nt(wrt_ref[...], h2, HIGHEST))
    sel = scores + rb_ref[...]
    tm = scores.shape[1]
    per = N_EXPERTS // N_GROUPS
    best_val, best = None, None
    for g in range(N_GROUPS):
        rows = [sel[g * per + i:g * per + i + 1, :] for i in range(per)]
        top2 = None
        for i in range(per):
            for i2 in range(i + 1, per):
                pair = rows[i] + rows[i2]
                top2 = pair if top2 is None else jnp.maximum(top2, pair)
        if g == 0:
            best_val, best = top2, jnp.zeros((1, tm), jnp.int32)
        else:
            better = top2 > best_val
            best = jnp.where(better, g, best)
            best_val = jnp.where(better, top2, best_val)
    eidx = _iota((N_EXPERTS, tm), 0)
    cand = jnp.where(eidx // per == best, sel, -jnp.inf)
    m1 = jnp.max(cand, axis=0, keepdims=True)
    i1 = jnp.min(jnp.where(cand == m1, eidx, N_EXPERTS), axis=0, keepdims=True)
    cand2 = jnp.where(eidx == i1, -jnp.inf, cand)
    m2 = jnp.max(cand2, axis=0, keepdims=True)
    i2 = jnp.min(jnp.where(cand2 == m2, eidx, N_EXPERTS), axis=0, keepdims=True)
    w1 = jnp.sum(jnp.where(eidx == i1, scores, 0.0), axis=0, keepdims=True)
    w2 = jnp.sum(jnp.where(eidx == i2, scores, 0.0), axis=0, keepdims=True)
    den = w1 + w2
    comb_ref[...] = jnp.where(eidx == i1, w1 / den, 0.0) + jnp.where(eidx == i2, w2 / den, 0.0)


def _merge(x, oa, ob, oc, mods, norm_w, norm2_w, w_gate, b_gate, w_branch, w_out, w_router_t, router_bias,
           tm, tiles_per_group):
    n = x.shape[0]
    row = lambda w: pl.BlockSpec((tm, w), lambda i: (i, 0))
    mod_specs = [_mod_spec(m, tm, tiles_per_group) for m in mods]
    return pl.pallas_call(
        _merge_kernel,
        grid=(n // tm,),
        in_specs=[row(D_MODEL), row(MIX), row(MIX), row(MIX)] + mod_specs
                 + [_const_spec((1, D_MODEL)), _const_spec((1, D_MODEL)), _const_spec(w_gate.shape),
                    _const_spec(b_gate.shape), _const_spec(w_branch.shape), _const_spec(w_out.shape),
                    _const_spec(w_router_t.shape), _const_spec(router_bias.shape)],
        out_specs=[row(D_MODEL), row(D_MODEL), pl.BlockSpec((N_EXPERTS, tm), lambda i: (0, i))],
        out_shape=[jax.ShapeDtypeStruct((n, D_MODEL), F32), jax.ShapeDtypeStruct((n, D_MODEL), BF16),
                   jax.ShapeDtypeStruct((N_EXPERTS, n), F32)],
        compiler_params=_params("arbitrary"),
        name="merge",
    )(x, oa, ob, oc, *mods, norm_w, norm2_w, w_gate, b_gate, w_branch, w_out, w_router_t, router_bias)


def _moe_kernel(x1_ref, h2_ref, comb_ref, gf_ref, wg_ref, wu_ref, wd_ref, o_ref):
    h2 = h2_ref[...]
    comb = jnp.transpose(comb_ref[...])
    per = N_EXPERTS // N_GROUPS
    acc = None
    for g in range(N_GROUPS):
        hid = [(_silu(_dot(h2, wg_ref[e])) * _dot(h2, wu_ref[e]) * comb[:, e:e + 1]).astype(BF16)
               for e in range(g * per, (g + 1) * per)]
        term = _dot(jnp.concatenate(hid, axis=1), wd_ref[g])
        acc = term if acc is None else acc + term
    o_ref[...] = x1_ref[...] + gf_ref[0] * acc


def _moe(x1, h2, comb, g_ffn, wg, wu, wd, tm, tiles_per_group):
    n = x1.shape[0]
    row = lambda w: pl.BlockSpec((tm, w), lambda i: (i, 0))
    return pl.pallas_call(
        _moe_kernel,
        grid=(n // tm,),
        in_specs=[row(D_MODEL), row(D_MODEL), pl.BlockSpec((N_EXPERTS, tm), lambda i: (0, i)),
                  _mod_spec(g_ffn, tm, tiles_per_group),
                  _const_spec(wg.shape), _const_spec(wu.shape), _const_spec(wd.shape)],
        out_specs=row(D_MODEL),
        out_shape=jax.ShapeDtypeStruct((n, D_MODEL), F32),
        compiler_params=_params("arbitrary"),
        name="moe",
    )(x1, h2, comb, g_ffn, wg, wu, wd)


def _pair_state(s):
    bsz = s.shape[0]
    s = s.reshape(bsz, N_PAIRS, 2, HEAD_DIM, HEAD_DIM)
    z = jnp.zeros_like(s[:, :, 0])
    top = jnp.concatenate([s[:, :, 0], z], axis=-1)
    bot = jnp.concatenate([z, s[:, :, 1]], axis=-1)
    return jnp.concatenate([top, bot], axis=-2)


def _unpair_state(s):
    bsz = s.shape[0]
    return jnp.stack([s[:, :, :HEAD_DIM, :HEAD_DIM], s[:, :, HEAD_DIM:, HEAD_DIM:]], axis=2).reshape(
        bsz, N_HEADS, HEAD_DIM, HEAD_DIM)


def _rel_bias(table, cq, wn):
    span = cq - 1 + wn
    offs = jnp.arange(-(cq - 1), wn + 1)
    diag = table[:, jnp.clip(BAND - offs, -MAX_REL, MAX_REL) + MAX_REL].astype(F32)
    skew = jnp.tile(diag, (1, cq))[:, :cq * span].reshape(-1, cq, span)
    return skew[:, :, cq - 1:]


def _layer(x, mod, state, wts, tm):
    bsz, t, _ = x.shape
    n = bsz * t
    c = min(CHUNK, t)
    shift0, wkv0, k_past, v_past, conv0, s0 = state
    if t % tm == 0:
        tiles_per_group = t // tm
        mods = [m.reshape(bsz, 1, D_MODEL) for m in jnp.split(mod, 6, axis=-1)]
    else:
        tiles_per_group = 1
        mods = [jnp.repeat(m, t, axis=0).reshape(n // tm, tm, D_MODEL) for m in jnp.split(mod, 6, axis=-1)]
    sh_mix, sc_mix, g_mix, sh_ffn, sc_ffn, g_ffn = mods
    xf = x.reshape(n, D_MODEL)

    pa, q, k, v, pc, pab = _inproj(xf, sh_mix, sc_mix, wts["norm_mix_w"], wts["wa"], wts["wb"], wts["wc"],
                                   wts["wab"], wts["q_norm"], wts["k_norm"], tm, tiles_per_group)
    pa = pa.reshape(bsz, t, A_COLS)
    q, k, v = (z.reshape(bsz, t, MIX) for z in (q, k, v))
    pc = pc.reshape(bsz, t, C_MAIN)

    oa, wkv = _rwkv(pa, shift0.reshape(bsz, 1, A_COLS), _pair_state(jnp.swapaxes(wkv0, -1, -2)),
                    wts["mu"], wts["w0"], wts["a0"], wts["wwa"], wts["g_up"], wts["k_k"], wts["k_a"], wts["r_k"],
                    wts["gn_w"], wts["gn_b"], c)
    new_wkv = jnp.swapaxes(_unpair_state(wkv), -1, -2)

    if k_past is None:
        ob = _band_prompt(q, k, v, wts["rel_bias"])
        new_k, new_v = k[:, -BAND:], v[:, -BAND:]
    else:
        ob = _band_sample(q, k, v, k_past, v_past, wts["rel_bias"])
        new_k, new_v = k, v

    conv_pad = jnp.pad(conv0, ((0, 0), (SUBLANES - (CONV_W - 1), 0), (0, 0)))
    oc, s_new = _gdn(pc, pab.reshape(bsz, t, LANES), conv_pad, _pair_state(s0), wts["conv_w"], wts["a_log"],
                     wts["dt_bias"], wts["gdn_norm_w"], wts["eab"], c)
    tail = min(t, CONV_W - 1)
    new_conv = jnp.concatenate([conv0, pc[:, t - tail:, :3 * MIX]], axis=1)[:, -(CONV_W - 1):]

    x1, h2, comb = _merge(xf, oa.reshape(n, MIX), ob.reshape(n, MIX), oc.reshape(n, MIX),
                          [sh_mix, sc_mix, g_mix, sh_ffn, sc_ffn], wts["norm_mix_w"], wts["norm_ffn_w"],
                          wts["w_gate"], wts["b_gate"], wts["w_branch"], wts["w_out"], wts["w_router_t"],
                          wts["router_bias"], tm, tiles_per_group)
    x2 = _moe(x1, h2, comb, g_ffn, wts["wg"], wts["wu"], wts["wd"], tm, tiles_per_group)
    heads = lambda z: z.reshape(bsz, -1, N_HEADS, HEAD_DIM)
    return x2.reshape(bsz, t, D_MODEL), (pa[:, -1], new_wkv, heads(new_k), heads(new_v), new_conv,
                                         _unpair_state(s_new))


def _prepare_layer(l, w_in, norm_mix_w, norm_ffn_w, rwkv_mu, rwkv_w0, rwkv_w_up, rwkv_a0, rwkv_a_up, rwkv_g_up,
                   rwkv_k_k, rwkv_k_a, rwkv_r_k, rwkv_gn_w, rwkv_gn_b, band_q_norm, band_k_norm, band_rel_bias,
                   gdn_conv_w, gdn_a_log, gdn_dt_bias, gdn_norm_w, w_branch, w_gate, b_gate, w_out,
                   w_router, router_bias, w_exp_gate, w_exp_up, w_exp_down):
    row = lambda z: z.reshape(1, -1).astype(F32)
    per_head = lambda z: jnp.repeat(z, HEAD_DIM).reshape(1, MIX)
    win = w_in[l].astype(BF16)
    c0 = A_COLS + B_COLS
    wab = jnp.pad(win[:, c0 + C_MAIN:], ((0, 0), (0, LANES - 2 * N_HEADS)))
    zeros = jnp.zeros((HEAD_DIM, MIX), F32)
    wwa = jnp.concatenate([jnp.concatenate([rwkv_w_up[l], zeros], axis=1),
                           jnp.concatenate([zeros, rwkv_a_up[l]], axis=1)], axis=0)
    head_of_lane = jnp.arange(MIX) // HEAD_DIM
    src = jnp.arange(LANES)[:, None]
    eab = jnp.concatenate([(src == head_of_lane[None, :]), (src == N_HEADS + head_of_lane[None, :])],
                          axis=1).astype(BF16)
    return dict(
        norm_mix_w=row(norm_mix_w[l]), norm_ffn_w=row(norm_ffn_w[l]),
        wa=win[:, :A_COLS], wb=win[:, A_COLS:c0], wc=win[:, c0:c0 + C_MAIN], wab=wab,
        q_norm=row(jnp.tile(band_q_norm[l], N_HEADS)), k_norm=row(jnp.tile(band_k_norm[l], N_HEADS)),
        mu=row(rwkv_mu[l]), w0=row(rwkv_w0[l]), a0=row(rwkv_a0[l]), wwa=wwa, g_up=rwkv_g_up[l],
        k_k=row(rwkv_k_k[l]), k_a=row(rwkv_k_a[l]), r_k=row(rwkv_r_k[l]), gn_w=row(rwkv_gn_w[l]),
        gn_b=row(rwkv_gn_b[l]), rel_bias=band_rel_bias[l],
        conv_w=gdn_conv_w[l], a_log=per_head(gdn_a_log[l]), dt_bias=per_head(gdn_dt_bias[l]),
        gdn_norm_w=row(jnp.tile(gdn_norm_w[l], N_HEADS)), eab=eab,
        w_gate=w_gate[l].astype(BF16), b_gate=row(b_gate[l]), w_branch=w_branch[l].astype(BF16),
        w_out=w_out[l].astype(BF16), w_router_t=jnp.transpose(w_router), router_bias=router_bias.reshape(-1, 1),
        wg=w_exp_gate[l].astype(BF16), wu=w_exp_up[l].astype(BF16),
        wd=w_exp_down[l].astype(BF16).reshape(N_GROUPS, -1, D_MODEL),
    )


def kernel(x_prompt, x_sample, c_prompt, c_sample, state_rwkv_shift, state_rwkv_wkv, cache_band_k, cache_band_v, state_gdn_conv, state_gdn_S, w_ada, b_ada, norm_mix_w, norm_ffn_w, w_in, rwkv_mu, rwkv_w0, rwkv_w_up, rwkv_a0, rwkv_a_up, rwkv_g_up, rwkv_k_k, rwkv_k_a, rwkv_r_k, rwkv_gn_w, rwkv_gn_b, band_q_norm, band_k_norm, band_rel_bias, gdn_conv_w, gdn_a_log, gdn_dt_bias, gdn_norm_w, w_branch, w_gate, b_gate, w_out, w_router, router_bias, w_exp_gate, w_exp_up, w_exp_down):
    depth = w_ada.shape[0]
    bsz = x_prompt.shape[0]
    mod_p = _adaln(c_prompt, w_ada, b_ada)
    mod_s = _adaln(c_sample, w_ada, b_ada)
    zero_state = (jnp.zeros((bsz, A_COLS), F32), jnp.zeros((bsz, N_HEADS, HEAD_DIM, HEAD_DIM), F32), None, None,
                  jnp.zeros((bsz, CONV_W - 1, 3 * MIX), F32), jnp.zeros((bsz, N_HEADS, HEAD_DIM, HEAD_DIM), F32))
    tm = TOKEN_TILE
    xp, xs = x_prompt, x_sample
    new_p, new_s = [], []
    for l in range(depth):
        wts = _prepare_layer(l, w_in, norm_mix_w, norm_ffn_w, rwkv_mu, rwkv_w0, rwkv_w_up, rwkv_a0, rwkv_a_up,
                             rwkv_g_up, rwkv_k_k, rwkv_k_a, rwkv_r_k, rwkv_gn_w, rwkv_gn_b, band_q_norm,
                             band_k_norm, band_rel_bias, gdn_conv_w, gdn_a_log, gdn_dt_bias, gdn_norm_w,
                             w_branch, w_gate, b_gate, w_out, w_router, router_bias, w_exp_gate, w_exp_up,
                             w_exp_down)
        xp, st_p = _layer(xp, mod_p[l], zero_state, wts, tm)
        xs, st_s = _layer(xs, mod_s[l], (state_rwkv_shift[l], state_rwkv_wkv[l], cache_band_k[l], cache_band_v[l],
                                         state_gdn_conv[l], state_gdn_S[l]), wts, tm)
        new_p.append(st_p)
        new_s.append(st_s)
    p_out = [jnp.stack(z, axis=0) for z in zip(*new_p)]
    s_out = [jnp.stack(z, axis=0) for z in zip(*new_s)]
    return (xp, xs, *p_out, *s_out)
```

```python
import functools
import math

import jax
import jax.numpy as jnp
from jax import lax
from jax.experimental import pallas as pl
from jax.experimental.pallas import tpu as pltpu

F32 = jnp.float32
BF16 = jnp.bfloat16
HIGHEST = lax.Precision.HIGHEST

D_MODEL = 1024
MIX = 512
HEAD_DIM = 64
N_HEADS = MIX // HEAD_DIM
LANES = 128
N_PAIRS = MIX // LANES
SUBLANES = 8
CHUNK = 64
TOKEN_TILE = 512
MIXER_ROWS = 256
BAND_QUERIES = 2 * CHUNK
A_COLS = 3 * MIX + 64 + 64 + 128
B_COLS = 3 * MIX
C_MAIN = 4 * MIX
CONV_W = 4
BAND = 8 * CHUNK
MAX_REL = 2 * CHUNK
N_EXPERTS = 16
N_GROUPS = 4
EXPERT_DIM = D_MODEL // 4
RMS_EPS = 1e-6
A_GN_EPS = 64e-5
A_DECAY_SCALE = math.exp(-0.5)
VMEM_LIMIT_BYTES = 56 * 1024 * 1024


def _params(*sem):
    return pltpu.CompilerParams(dimension_semantics=sem, vmem_limit_bytes=VMEM_LIMIT_BYTES)


def _dot(a, b, precision=None):
    return lax.dot_general(a, b, (((1,), (0,)), ((), ())), precision=precision, preferred_element_type=F32)


def _dot_nt(a, b, precision=None):
    return lax.dot_general(a, b, (((1,), (1,)), ((), ())), precision=precision, preferred_element_type=F32)


def _bdot(a, b):
    return _dot(a.astype(BF16), b.astype(BF16))


def _bdot_nt(a, b):
    return _dot_nt(a.astype(BF16), b.astype(BF16))


def _hdot(a, b):
    return _dot(a, b, HIGHEST)


def _bmm(a, b):
    return lax.dot_general(a.astype(BF16), b.astype(BF16), (((2,), (1,)), ((0,), (0,))),
                           preferred_element_type=F32)


def _bmm_nt(a, b):
    return lax.dot_general(a.astype(BF16), b.astype(BF16), (((2,), (2,)), ((0,), (0,))),
                           preferred_element_type=F32)


def _sigmoid(x):
    return 1.0 / (1.0 + jnp.exp(-x))


def _silu(x):
    return x * _sigmoid(x)


def _softplus(x):
    return jnp.maximum(x, 0.0) + jnp.log1p(jnp.exp(-jnp.abs(x)))


def _iota(shape, dim):
    return lax.broadcasted_iota(jnp.int32, shape, dim)


def _bf16_terms(x, terms):
    parts = []
    for _ in range(terms):
        part = x.astype(BF16)
        parts.append(part)
        x = x - part.astype(F32)
    return parts


def _dot_01(x, m, terms):
    out = None
    for part in _bf16_terms(x, terms):
        term = _dot(part, m)
        out = term if out is None else out + term
    return out


def _dot_01_left(m, x, terms):
    out = None
    for part in _bf16_terms(x, terms):
        term = _dot(m, part)
        out = term if out is None else out + term
    return out


def _head_ones():
    return (_iota((LANES, LANES), 0) // HEAD_DIM == _iota((LANES, LANES), 1) // HEAD_DIM).astype(BF16)


def _head_sum(x, ones):
    parts = [_dot_01(x[:, g * LANES:(g + 1) * LANES], ones, 2) for g in range(x.shape[1] // LANES)]
    return parts[0] if len(parts) == 1 else jnp.concatenate(parts, axis=1)


def _stack(x):
    lane = _iota(x.shape, 1)
    return jnp.concatenate([jnp.where(lane < HEAD_DIM, x, 0.0), jnp.where(lane >= HEAD_DIM, x, 0.0)], axis=0)


def _tiles(x, c):
    return jnp.stack([x[u * c:(u + 1) * c, j * LANES:(j + 1) * LANES]
                      for u in range(x.shape[0] // c) for j in range(N_PAIRS)])


def _untile(xs):
    return jnp.concatenate([jnp.concatenate([xs[u * N_PAIRS + j] for j in range(N_PAIRS)], axis=1)
                            for u in range(xs.shape[0] // N_PAIRS)], axis=0)


def _block_diag(x):
    x = x.astype(BF16)
    first = _iota(x.shape[1:], 1) < x.shape[2] // 2
    zero = jnp.zeros_like(x)
    return jnp.concatenate([jnp.where(first, x, zero), jnp.where(first, zero, x)], axis=1)


def _keep_head_blocks(x):
    same = _iota(x.shape[1:], 0) // HEAD_DIM == _iota(x.shape[1:], 1) // HEAD_DIM
    return jnp.where(same, x, 0.0)


def _transpose_tiles(xs):
    return jnp.stack([jnp.transpose(xs[g]) for g in range(xs.shape[0])])


def _chunk_last(x, c):
    return jnp.concatenate([jnp.broadcast_to(x[(u + 1) * c - 1:(u + 1) * c], (c, x.shape[1]))
                            for u in range(x.shape[0] // c)], axis=0)


def _column_scale(row_values, c):
    tiles = []
    for u in range(row_values.shape[0] // c):
        last = row_values[(u + 1) * c - 1:(u + 1) * c]
        for j in range(N_PAIRS):
            tiles.append(jnp.transpose(jnp.broadcast_to(last[:, j * LANES:(j + 1) * LANES], (LANES, LANES))))
    return jnp.stack(tiles)


def _chunk_tril(rows, c):
    row, col = _iota((rows, rows), 0), _iota((rows, rows), 1)
    return ((col <= row) & (row // c == col // c)).astype(BF16)


def _tri_masks(c):
    row, col = _iota((c, 2 * c), 0), _iota((c, 2 * c), 1) % c
    return col < row, col <= row


def _unit_lower_inverse(a, c):
    row, col = _iota((c, 2 * c), 0), _iota((c, 2 * c), 1) % c
    eye = (row == col).astype(F32)
    same = lambda s: row // s == col // s
    a0 = jnp.where(same(SUBLANES), a, 0.0)
    a2 = _bmm(a0, _block_diag(a0))
    a4 = _bmm(a2, _block_diag(a2))
    x = _bmm(_bmm(eye - a0, _block_diag(eye + a2)), _block_diag(eye + a4))
    s = SUBLANES
    while s < c:
        off = jnp.where(same(2 * s) & jnp.logical_not(same(s)), a, 0.0)
        x = x - _bmm(_bmm(x, _block_diag(off)), _block_diag(x))
        s *= 2
    return x


def _adaln_kernel(c_ref, w_ref, b_ref, o_ref):
    o_ref[0] = _hdot(_silu(c_ref[...]), w_ref[0]) + b_ref[0]


def _adaln(c, w_ada, b_ada):
    depth, bsz, tn = w_ada.shape[0], c.shape[0], 768
    return pl.pallas_call(
        _adaln_kernel,
        grid=(depth, 6 * D_MODEL // tn),
        in_specs=[pl.BlockSpec((bsz, D_MODEL), lambda l, j: (0, 0)),
                  pl.BlockSpec((1, D_MODEL, tn), lambda l, j: (l, 0, j)),
                  pl.BlockSpec((1, 1, tn), lambda l, j: (l, 0, j))],
        out_specs=pl.BlockSpec((1, bsz, tn), lambda l, j: (l, 0, j)),
        out_shape=jax.ShapeDtypeStruct((depth, bsz, 6 * D_MODEL), F32),
        compiler_params=_params("arbitrary", "arbitrary"),
        name="adaln",
    )(c, w_ada, b_ada.reshape(depth, 1, 6 * D_MODEL))


def _norm_mod(x, norm_w, scale, shift):
    y = x * lax.rsqrt(jnp.mean(x * x, axis=-1, keepdims=True) + RMS_EPS)
    return y * norm_w * (1.0 + scale) + shift


def _inproj_kernel(x_ref, sh_ref, sc_ref, nw_ref, wa_ref, wb_ref, wc_ref, wab_ref, qn_ref, kn_ref,
                   pa_ref, q_ref, k_ref, v_ref, pc_ref, pab_ref):
    h = _norm_mod(x_ref[...], nw_ref[...], sc_ref[0], sh_ref[0]).astype(BF16)
    pa_ref[...] = _dot(h, wa_ref[...])
    pb = _dot(h, wb_ref[...])
    ones = _head_ones()

    def head_rms(y, w):
        return y * lax.rsqrt(_head_sum(y * y, ones) * (1.0 / HEAD_DIM) + RMS_EPS) * w

    q_ref[...] = head_rms(pb[:, :MIX], qn_ref[...])
    k_ref[...] = head_rms(pb[:, MIX:2 * MIX], kn_ref[...])
    v_ref[...] = pb[:, 2 * MIX:]
    pc_ref[...] = _dot(h, wc_ref[...])
    pab_ref[...] = _dot(h, wab_ref[...])


def _mod_spec(mod, tm, tiles_per_group):
    rows = mod.shape[1]
    return pl.BlockSpec((1, rows, D_MODEL), lambda i: (i // tiles_per_group, 0, 0))


def _const_spec(shape):
    zeros = (0,) * len(shape)
    return pl.BlockSpec(shape, lambda *_: zeros, pipeline_mode=pl.Buffered(1))


def _inproj(x, shift, scale, norm_w, wa, wb, wc, wab, q_norm, k_norm, tm, tiles_per_group):
    n = x.shape[0]
    row = lambda w: pl.BlockSpec((tm, w), lambda i: (i, 0))
    widths = (A_COLS, MIX, MIX, MIX, C_MAIN, LANES)
    return pl.pallas_call(
        _inproj_kernel,
        grid=(n // tm,),
        in_specs=[row(D_MODEL), _mod_spec(shift, tm, tiles_per_group), _mod_spec(scale, tm, tiles_per_group),
                  _const_spec((1, D_MODEL)), _const_spec(wa.shape), _const_spec(wb.shape), _const_spec(wc.shape),
                  _const_spec(wab.shape), _const_spec((1, MIX)), _const_spec((1, MIX))],
        out_specs=[row(w) for w in widths],
        out_shape=[jax.ShapeDtypeStruct((n, w), F32) for w in widths],
        compiler_params=_params("arbitrary"),
        name="inproj",
    )(x, shift, scale, norm_w, wa, wb, wc, wab, q_norm, k_norm)


def _rwkv_kernel(p_ref, shift0_ref, st0_ref, mu_ref, w0_ref, a0_ref, wwa_ref, gup_ref, kk_ref, ka_ref, rk_ref,
                 gnw_ref, gnb_ref, o_ref, st_out_ref, ext_scr, st_scr, *, c):
    bb, tt, _ = p_ref.shape
    rows, nch, n = bb * tt, tt // c, 2 * c
    assert bb == 1 or nch == 1
    ci = pl.program_id(1)

    @pl.when(ci == 0)
    def _():
        ext_scr[:, SUBLANES - 1:SUBLANES, :] = shift0_ref[...]
        st_scr[...] = st0_ref[...].reshape(bb * N_PAIRS, LANES, LANES)

    ext_scr[:, SUBLANES:SUBLANES + tt, :] = p_ref[...]
    p = p_ref[...].reshape(rows, A_COLS)
    p_prev = ext_scr[:, SUBLANES - 1:SUBLANES - 1 + tt, :].reshape(rows, A_COLS)
    ext_scr[:, 0:SUBLANES, :] = ext_scr[:, tt:tt + SUBLANES, :]
    xs = p + (p_prev - p) * mu_ref[...]
    r, k, v = xs[:, :MIX], xs[:, MIX:2 * MIX], xs[:, 2 * MIX:3 * MIX]
    lora_in = xs[:, 3 * MIX:3 * MIX + LANES]
    gd = xs[:, 3 * MIX + LANES:]
    lane = _iota((1, LANES), 1)
    lora = _bdot(jnp.where(lane < HEAD_DIM, jnp.tanh(lora_in), lora_in), wwa_ref[...])
    log_w = -A_DECAY_SCALE * _sigmoid(w0_ref[...] + lora[:, :MIX])
    a = _sigmoid(a0_ref[...] + lora[:, MIX:])
    g = _bdot(_sigmoid(gd), gup_ref[...])
    ones = _head_ones()
    kk_raw = k * kk_ref[...]
    kk = kk_raw * lax.rsqrt(_head_sum(kk_raw * kk_raw, ones) + RMS_EPS)
    k = k * (1.0 + (a - 1.0) * ka_ref[...])
    b = kk * a
    cum = _dot_01_left(_chunk_tril(rows, c), log_w, 3)
    cum_last = _chunk_last(cum, c)
    e_cum, e_neg = jnp.exp(cum), jnp.exp(-cum)
    e_prev, e_tail = jnp.exp(cum - log_w), jnp.exp(cum_last - cum)
    strict, incl = _tri_masks(c)

    kk_g, r_g, v_g = _tiles(kk * e_prev, c), _tiles(r * e_cum, c), _tiles(v, c)
    amat = _bmm_nt(jnp.concatenate([kk_g, r_g], axis=1),
                   jnp.concatenate([_block_diag(_tiles(k * e_neg, c)), _block_diag(_tiles(b * e_neg, c))],
                                   axis=1))
    a_kk = jnp.where(strict, amat[:, :c, :n], 0.0)
    a_bk = jnp.where(strict, amat[:, :c, n:], 0.0)
    a_rk = jnp.where(incl, amat[:, c:, :n], 0.0)
    a_rb = jnp.where(incl, amat[:, c:, n:], 0.0)
    t_inv = _unit_lower_inverse(a_bk, c)
    av = _bmm(jnp.concatenate([a_kk, a_rk], axis=1), _block_diag(v_g))
    wu = _bmm(t_inv, jnp.concatenate([_block_diag(kk_g), _block_diag(av[:, :c])], axis=2))
    kd_v = _keep_head_blocks(_bmm(_transpose_tiles(_tiles(k * e_tail, c)), v_g))
    bd_t = _transpose_tiles(_tiles(b * e_tail, c))
    decay = _column_scale(e_cum, c)

    st = st_scr[...]
    per_step = bb * N_PAIRS
    o_tiles = []
    for ch in range(nch):
        sel = slice(ch * per_step, (ch + 1) * per_step)
        zo = _bmm(jnp.concatenate([wu[sel, :, :LANES], r_g[sel]], axis=1), st)
        z = wu[sel, :, LANES:] + zo[:, :c]
        o_tiles.append(zo[:, c:] + av[sel, c:] - _bmm(a_rb[sel], _block_diag(z)))
        st = decay[sel] * st + kd_v[sel] - _keep_head_blocks(_bmm(bd_t[sel], z))
    st_scr[...] = st
    o = _untile(o_tiles[0] if nch == 1 else jnp.concatenate(o_tiles, axis=0))

    mean = _head_sum(o, ones) * (1.0 / HEAD_DIM)
    cen = o - mean
    var = _head_sum(cen * cen, ones) * (1.0 / HEAD_DIM)
    o = cen * lax.rsqrt(var + A_GN_EPS) * gnw_ref[...] + gnb_ref[...]
    bonus = _head_sum(r * k * rk_ref[...], ones) * v
    o_ref[...] = ((o + bonus) * g).reshape(bb, tt, MIX)
    st_out_ref[...] = st.reshape(bb, N_PAIRS, LANES, LANES)


def _mixer_tiling(bsz, t, c):
    if t > c:
        return 1, min(t, MIXER_ROWS)
    return min(bsz, MIXER_ROWS // t), t


def _rwkv(pa, shift0, st0, mu, w0, a0, wwa, gup, k_k, k_a, r_k, gn_w, gn_b, c):
    bsz, t, _ = pa.shape
    bb, tt = _mixer_tiling(bsz, t, c)
    vec = lambda w: _const_spec((1, w))
    state = pl.BlockSpec((bb, N_PAIRS, LANES, LANES), lambda b, i: (b, 0, 0, 0))
    return pl.pallas_call(
        functools.partial(_rwkv_kernel, c=c),
        grid=(bsz // bb, t // tt),
        in_specs=[pl.BlockSpec((bb, tt, A_COLS), lambda b, i: (b, i, 0)),
                  pl.BlockSpec((bb, 1, A_COLS), lambda b, i: (b, 0, 0)), state,
                  vec(A_COLS), vec(MIX), vec(MIX), _const_spec(wwa.shape), _const_spec(gup.shape),
                  vec(MIX), vec(MIX), vec(MIX), vec(MIX), vec(MIX)],
        out_specs=[pl.BlockSpec((bb, tt, MIX), lambda b, i: (b, i, 0)), state],
        out_shape=[jax.ShapeDtypeStruct((bsz, t, MIX), F32),
                   jax.ShapeDtypeStruct((bsz, N_PAIRS, LANES, LANES), F32)],
        scratch_shapes=[pltpu.VMEM((bb, tt + SUBLANES, A_COLS), F32),
                        pltpu.VMEM((bb * N_PAIRS, LANES, LANES), F32)],
        compiler_params=_params("arbitrary", "arbitrary"),
        name="rwkv",
    )(pa, shift0, st0, mu, w0, a0, wwa, gup, k_k, k_a, r_k, gn_w, gn_b)


def _gdn_kernel(pc_ref, pab_ref, conv0_ref, st0_ref, convw_ref, alog_ref, dtb_ref, nw_ref, eab_ref,
                o_ref, st_out_ref, ext_scr, st_scr, *, c):
    bb, tt, _ = pc_ref.shape
    rows, nch, n = bb * tt, tt // c, 2 * c
    assert bb == 1 or nch == 1
    w3 = 3 * MIX
    ci = pl.program_id(1)

    @pl.when(ci == 0)
    def _():
        ext_scr[:, 0:SUBLANES, :] = conv0_ref[...]
        st_scr[...] = st0_ref[...].reshape(bb * N_PAIRS, LANES, LANES)

    ext_scr[:, SUBLANES:SUBLANES + tt, :] = pc_ref[:, :, :w3]
    conv = ext_scr[:, SUBLANES:SUBLANES + tt, :] * convw_ref[CONV_W - 1:CONV_W, :]
    for s in range(1, CONV_W):
        conv = conv + ext_scr[:, SUBLANES - s:SUBLANES - s + tt, :] * convw_ref[CONV_W - 1 - s:CONV_W - s, :]
    ext_scr[:, 0:SUBLANES, :] = ext_scr[:, tt:tt + SUBLANES, :]
    qkv = _silu(conv.reshape(rows, w3))
    ones = _head_ones()
    l2n = lambda y: y * lax.rsqrt(_head_sum(y * y, ones) + RMS_EPS)
    q = l2n(qkv[:, :MIX]) * (HEAD_DIM ** -0.5)
    k = l2n(qkv[:, MIX:2 * MIX])
    v = qkv[:, 2 * MIX:]
    zgate = pc_ref[:, :, w3:].reshape(rows, MIX)
    ab = _dot_01(pab_ref[...].reshape(rows, LANES), eab_ref[...], 3)
    log_alpha = -jnp.exp(alog_ref[...]) * _softplus(ab[:, :MIX] + dtb_ref[...])
    beta = _sigmoid(ab[:, MIX:])
    gcum = _dot_01_left(_chunk_tril(rows, c), log_alpha, 3)
    eg = jnp.exp(gcum)
    bk = beta * k
    strict, incl = _tri_masks(c)

    g_t = _tiles(gcum, c)
    g_col = jnp.concatenate([g_t[:, :, :c], g_t[:, :, HEAD_DIM:HEAD_DIM + c]], axis=2)
    g_time = _transpose_tiles(g_t)
    g_row = jnp.concatenate([g_time[:, 0:1, :], g_time[:, HEAD_DIM:HEAD_DIM + 1, :]], axis=2)
    diff = g_col - g_row
    dec_s = jnp.where(strict, jnp.exp(jnp.where(strict, diff, 0.0)), 0.0)
    dec_i = jnp.where(incl, jnp.exp(jnp.where(incl, diff, 0.0)), 0.0)
    qg_g = _tiles(q * eg, c)
    kmat = _bmm_nt(jnp.concatenate([_tiles(bk, c), _tiles(q, c)], axis=1), _block_diag(_tiles(k, c)))
    attn = kmat[:, c:] * dec_i
    t_inv = _unit_lower_inverse(kmat[:, :c] * dec_s, c)
    sol = _bmm(t_inv, jnp.concatenate([_block_diag(_tiles(beta * v, c)), _block_diag(_tiles(bk * eg, c))],
                                      axis=2))
    kd_t = _transpose_tiles(_tiles(k * jnp.exp(_chunk_last(gcum, c) - gcum), c))
    decay = _column_scale(eg, c)

    st = st_scr[...]
    per_step = bb * N_PAIRS
    o_tiles = []
    for ch in range(nch):
        sel = slice(ch * per_step, (ch + 1) * per_step)
        wq = _bmm(jnp.concatenate([sol[sel, :, LANES:], qg_g[sel]], axis=1), st)
        delta = sol[sel, :, :LANES] - wq[:, :c]
        o_tiles.append(wq[:, c:] + _bmm(attn[sel], _block_diag(delta)))
        st = decay[sel] * st + _keep_head_blocks(_bmm(kd_t[sel], delta))
    st_scr[...] = st
    o = _untile(o_tiles[0] if nch == 1 else jnp.concatenate(o_tiles, axis=0))

    o = o * lax.rsqrt(_head_sum(o * o, ones) * (1.0 / HEAD_DIM) + RMS_EPS) * nw_ref[...]
    o_ref[...] = (o * _silu(zgate)).reshape(bb, tt, MIX)
    st_out_ref[...] = st.reshape(bb, N_PAIRS, LANES, LANES)


def _gdn(pc, pab, conv0, st0, conv_w, a_log, dt_bias, norm_w, eab, c):
    bsz, t, _ = pc.shape
    bb, tt = _mixer_tiling(bsz, t, c)
    vec = lambda w: _const_spec((1, w))
    state = pl.BlockSpec((bb, N_PAIRS, LANES, LANES), lambda b, i: (b, 0, 0, 0))
    return pl.pallas_call(
        functools.partial(_gdn_kernel, c=c),
        grid=(bsz // bb, t // tt),
        in_specs=[pl.BlockSpec((bb, tt, C_MAIN), lambda b, i: (b, i, 0)),
                  pl.BlockSpec((bb, tt, LANES), lambda b, i: (b, i, 0)),
                  pl.BlockSpec((bb, SUBLANES, 3 * MIX), lambda b, i: (b, 0, 0)), state,
                  _const_spec((CONV_W, 3 * MIX)), vec(MIX), vec(MIX), vec(MIX), _const_spec(eab.shape)],
        out_specs=[pl.BlockSpec((bb, tt, MIX), lambda b, i: (b, i, 0)), state],
        out_shape=[jax.ShapeDtypeStruct((bsz, t, MIX), F32),
                   jax.ShapeDtypeStruct((bsz, N_PAIRS, LANES, LANES), F32)],
        scratch_shapes=[pltpu.VMEM((bb, tt + SUBLANES, 3 * MIX), F32),
                        pltpu.VMEM((bb * N_PAIRS, LANES, LANES), F32)],
        compiler_params=_params("arbitrary", "arbitrary"),
        name="gdn",
    )(pc, pab, conv0, st0, conv_w, a_log, dt_bias, norm_w, eab)


def _band_core(q, k_g, v_g, bias, valid):
    bb, cq, _ = q.shape
    wn = k_g.shape[1]
    q = q * (HEAD_DIM ** -0.5)
    q_g = jnp.stack([_stack(q[b, :, j * LANES:(j + 1) * LANES])
                     for b in range(bb) for j in range(N_PAIRS)]).astype(BF16)
    s = _bmm_nt(q_g, k_g)
    s = (s.reshape(bb, N_PAIRS, 2 * cq, wn) + bias).reshape(bb * N_PAIRS, 2 * cq, wn)
    if valid is not None:
        s = jnp.where(valid, s, -1e30)
    e = jnp.exp(s - jnp.max(s, axis=-1, keepdims=True))
    pv = _bmm(e, v_g) / jnp.sum(e, axis=-1, keepdims=True)
    lane = _iota((cq, LANES), 1)
    return jnp.stack([
        jnp.concatenate([jnp.where(lane < HEAD_DIM, pv[b * N_PAIRS + j, :cq], pv[b * N_PAIRS + j, cq:])
                         for j in range(N_PAIRS)], axis=1) for b in range(bb)])


def _band_prompt_kernel(q_ref, k_ref, v_ref, bias_ref, o_ref):
    cq = q_ref.shape[1]
    wn = BAND + cq
    first = pl.program_id(1) * cq - BAND
    start = pl.multiple_of(jnp.maximum(first, 0), LANES)
    shift = pl.multiple_of(start - first, LANES)
    window = lambda ref: jnp.stack([ref[0, pl.ds(start, wn), j * LANES:(j + 1) * LANES]
                                    for j in range(N_PAIRS)]).astype(BF16)
    qc = (_iota((2 * cq, wn), 0) % cq) // CHUNK
    kc = (_iota((2 * cq, wn), 1) + shift) // CHUNK
    valid = (kc >= qc) & (kc <= qc + BAND // CHUNK)
    o_ref[...] = _band_core(q_ref[...], window(k_ref), window(v_ref), bias_ref[:, :, pl.ds(shift, wn)], valid)


def _band_prompt(q, k, v, table):
    bsz, t, _ = q.shape
    cq = BAND_QUERIES
    bias = _rel_bias(table, cq, 2 * BAND + cq).reshape(N_PAIRS, 2 * cq, 2 * BAND + cq)
    full = pl.BlockSpec((1, t, MIX), lambda b, i: (b, 0, 0))
    return pl.pallas_call(
        _band_prompt_kernel,
        grid=(bsz, t // cq),
        in_specs=[pl.BlockSpec((1, cq, MIX), lambda b, i: (b, i, 0)), full, full, _const_spec(bias.shape)],
        out_specs=pl.BlockSpec((1, cq, MIX), lambda b, i: (b, i, 0)),
        out_shape=jax.ShapeDtypeStruct((bsz, t, MIX), F32),
        compiler_params=_params("arbitrary", "arbitrary"),
        name="band_prompt",
    )(q, k, v, bias)


def _band_sample_kernel(q_ref, kp_ref, vp_ref, kn_ref, vn_ref, bias_ref, o_ref):
    bb = q_ref.shape[0]
    keys = lambda past, new: jnp.stack([
        jnp.concatenate([past[b, :, j * LANES:(j + 1) * LANES], new[b, :, j * LANES:(j + 1) * LANES]], axis=0)
        for b in range(bb) for j in range(N_PAIRS)]).astype(BF16)
    o_ref[...] = _band_core(q_ref[...], keys(kp_ref, kn_ref), keys(vp_ref, vn_ref), bias_ref[...], None)


def _band_sample(q, k, v, caches, table):
    k_past, v_past, layer = caches
    bsz, t, _ = q.shape
    past = k_past.shape[2]
    bb = min(bsz, MIXER_ROWS // t)
    bias = _rel_bias(table, t, past + t).reshape(N_PAIRS, 2 * t, past + t)
    new = pl.BlockSpec((bb, t, MIX), lambda b: (b, 0, 0))
    old = pl.BlockSpec((None, bb, past, MIX), lambda b: (layer, b, 0, 0))
    return pl.pallas_call(
        _band_sample_kernel,
        grid=(bsz // bb,),
        in_specs=[new, old, old, new, new, _const_spec(bias.shape)],
        out_specs=new,
        out_shape=jax.ShapeDtypeStruct((bsz, t, MIX), F32),
        compiler_params=_params("arbitrary"),
        name="band_sample",
    )(q, k_past, v_past, k, v, bias)


def _merge_kernel(x_ref, oa_ref, ob_ref, oc_ref, sh_ref, sc_ref, gm_ref, sh2_ref, sc2_ref, nw_ref, nw2_ref,
                  wgate_ref, bgate_ref, wbr_ref, wout_ref, wrt_ref, rb_ref,
                  x1_ref, h2_ref, comb_ref):
    x = x_ref[...]
    h = _norm_mod(x, nw_ref[...], sc_ref[0], sh_ref[0]).astype(BF16)
    gates = _sigmoid(_dot(h, wgate_ref[...]) + bgate_ref[...])
    mixed = None
    for i, o_ref in enumerate((oa_ref, ob_ref, oc_ref)):
        term = gates[:, i * D_MODEL:(i + 1) * D_MODEL] * _bdot(o_ref[...], wbr_ref[i])
        mixed = term if mixed is None else mixed + term
    x1 = x + gm_ref[0] * _bdot(mixed, wout_ref[...])
    x1_ref[...] = x1
    h2 = _norm_mod(x1, nw2_ref[...], sc2_ref[0], sh2_ref[0])
    h2_ref[...] = h2.astype(BF16)

    scores = _sigmoid(_dot_nt(wrt_ref[...], h2, HIGHEST))
    sel = scores + rb_ref[...]
    tm = scores.shape[1]
    per = N_EXPERTS // N_GROUPS
    best_val, best = None, None
    for g in range(N_GROUPS):
        rows = [sel[g * per + i:g * per + i + 1, :] for i in range(per)]
        top2 = None
        for i in range(per):
            for i2 in range(i + 1, per):
                pair = rows[i] + rows[i2]
                top2 = pair if top2 is None else jnp.maximum(top2, pair)
        if g == 0:
            best_val, best = top2, jnp.zeros((1, tm), jnp.int32)
        else:
            better = top2 > best_val
            best = jnp.where(better, g, best)
            best_val = jnp.where(better, top2, best_val)
    eidx = _iota((N_EXPERTS, tm), 0)
    cand = jnp.where(eidx // per == best, sel, -jnp.inf)
    m1 = jnp.max(cand, axis=0, keepdims=True)
    i1 = jnp.min(jnp.where(cand == m1, eidx, N_EXPERTS), axis=0, keepdims=True)
    cand2 = jnp.where(eidx == i1, -jnp.inf, cand)
    m2 = jnp.max(cand2, axis=0, keepdims=True)
    i2 = jnp.min(jnp.where(cand2 == m2, eidx, N_EXPERTS), axis=0, keepdims=True)
    w1 = jnp.sum(jnp.where(eidx == i1, scores, 0.0), axis=0, keepdims=True)
    w2 = jnp.sum(jnp.where(eidx == i2, scores, 0.0), axis=0, keepdims=True)
    den = w1 + w2
    comb_ref[...] = jnp.where(eidx == i1, w1 / den, 0.0) + jnp.where(eidx == i2, w2 / den, 0.0)


def _merge(x, oa, ob, oc, mods, norm_w, norm2_w, w_gate, b_gate, w_branch, w_out, w_router_t, router_bias,
           tm, tiles_per_group):
    n = x.shape[0]
    row = lambda w: pl.BlockSpec((tm, w), lambda i: (i, 0))
    mod_specs = [_mod_spec(m, tm, tiles_per_group) for m in mods]
    return pl.pallas_call(
        _merge_kernel,
        grid=(n // tm,),
        in_specs=[row(D_MODEL), row(MIX), row(MIX), row(MIX)] + mod_specs
                 + [_const_spec((1, D_MODEL)), _const_spec((1, D_MODEL)), _const_spec(w_gate.shape),
                    _const_spec(b_gate.shape), _const_spec(w_branch.shape), _const_spec(w_out.shape),
                    _const_spec(w_router_t.shape), _const_spec(router_bias.shape)],
        out_specs=[row(D_MODEL), row(D_MODEL), pl.BlockSpec((N_EXPERTS, tm), lambda i: (0, i))],
        out_shape=[jax.ShapeDtypeStruct((n, D_MODEL), F32), jax.ShapeDtypeStruct((n, D_MODEL), BF16),
                   jax.ShapeDtypeStruct((N_EXPERTS, n), F32)],
        compiler_params=_params("arbitrary"),
        name="merge",
    )(x, oa, ob, oc, *mods, norm_w, norm2_w, w_gate, b_gate, w_branch, w_out, w_router_t, router_bias)


def _moe_kernel(x1_ref, h2_ref, comb_ref, gf_ref, wg_ref, wu_ref, wd_ref, o_ref):
    h2 = h2_ref[...]
    comb = jnp.transpose(comb_ref[...])
    per = N_EXPERTS // N_GROUPS
    acc = None
    for g in range(N_GROUPS):
        hid = [(_silu(_dot(h2, wg_ref[e])) * _dot(h2, wu_ref[e]) * comb[:, e:e + 1]).astype(BF16)
               for e in range(g * per, (g + 1) * per)]
        term = _dot(jnp.concatenate(hid, axis=1), wd_ref[g])
        acc = term if acc is None else acc + term
    o_ref[...] = x1_ref[...] + gf_ref[0] * acc


def _moe(x1, h2, comb, g_ffn, wg, wu, wd, tm, tiles_per_group):
    n = x1.shape[0]
    row = lambda w: pl.BlockSpec((tm, w), lambda i: (i, 0))
    return pl.pallas_call(
        _moe_kernel,
        grid=(n // tm,),
        in_specs=[row(D_MODEL), row(D_MODEL), pl.BlockSpec((N_EXPERTS, tm), lambda i: (0, i)),
                  _mod_spec(g_ffn, tm, tiles_per_group),
                  _const_spec(wg.shape), _const_spec(wu.shape), _const_spec(wd.shape)],
        out_specs=row(D_MODEL),
        out_shape=jax.ShapeDtypeStruct((n, D_MODEL), F32),
        compiler_params=_params("arbitrary"),
        name="moe",
    )(x1, h2, comb, g_ffn, wg, wu, wd)


def _pair_state(s):
    bsz = s.shape[0]
    s = s.reshape(bsz, N_PAIRS, 2, HEAD_DIM, HEAD_DIM)
    z = jnp.zeros_like(s[:, :, 0])
    top = jnp.concatenate([s[:, :, 0], z], axis=-1)
    bot = jnp.concatenate([z, s[:, :, 1]], axis=-1)
    return jnp.concatenate([top, bot], axis=-2)


def _unpair_state(s):
    bsz = s.shape[0]
    return jnp.stack([s[:, :, :HEAD_DIM, :HEAD_DIM], s[:, :, HEAD_DIM:, HEAD_DIM:]], axis=2).reshape(
        bsz, N_HEADS, HEAD_DIM, HEAD_DIM)


def _rel_bias(table, cq, wn):
    span = cq - 1 + wn
    offs = jnp.arange(-(cq - 1), wn + 1)
    diag = table[:, jnp.clip(BAND - offs, -MAX_REL, MAX_REL) + MAX_REL].astype(F32)
    skew = jnp.tile(diag, (1, cq))[:, :cq * span].reshape(-1, cq, span)
    return skew[:, :, cq - 1:]


def _layer(x, mod, state, wts, tm):
    bsz, t, _ = x.shape
    n = bsz * t
    c = min(CHUNK, t)
    shift0, wkv0, caches, conv0, s0 = state
    if t % tm == 0:
        tiles_per_group = t // tm
        mods = [m.reshape(bsz, 1, D_MODEL) for m in jnp.split(mod, 6, axis=-1)]
    else:
        tiles_per_group = 1
        mods = [jnp.repeat(m, t, axis=0).reshape(n // tm, tm, D_MODEL) for m in jnp.split(mod, 6, axis=-1)]
    sh_mix, sc_mix, g_mix, sh_ffn, sc_ffn, g_ffn = mods
    xf = x.reshape(n, D_MODEL)

    pa, q, k, v, pc, pab = _inproj(xf, sh_mix, sc_mix, wts["norm_mix_w"], wts["wa"], wts["wb"], wts["wc"],
                                   wts["wab"], wts["q_norm"], wts["k_norm"], tm, tiles_per_group)
    pa = pa.reshape(bsz, t, A_COLS)
    q, k, v = (z.reshape(bsz, t, MIX) for z in (q, k, v))
    pc = pc.reshape(bsz, t, C_MAIN)

    oa, wkv = _rwkv(pa, shift0.reshape(bsz, 1, A_COLS), _pair_state(jnp.swapaxes(wkv0, -1, -2)),
                    wts["mu"], wts["w0"], wts["a0"], wts["wwa"], wts["g_up"], wts["k_k"], wts["k_a"], wts["r_k"],
                    wts["gn_w"], wts["gn_b"], c)
    new_wkv = jnp.swapaxes(_unpair_state(wkv), -1, -2)

    if caches is None:
        ob = _band_prompt(q, k, v, wts["rel_bias"])
        new_k, new_v = k[:, -BAND:], v[:, -BAND:]
    else:
        ob = _band_sample(q, k, v, caches, wts["rel_bias"])
        new_k, new_v = k, v

    conv_pad = jnp.pad(conv0, ((0, 0), (SUBLANES - (CONV_W - 1), 0), (0, 0)))
    oc, s_new = _gdn(pc, pab.reshape(bsz, t, LANES), conv_pad, _pair_state(s0), wts["conv_w"], wts["a_log"],
                     wts["dt_bias"], wts["gdn_norm_w"], wts["eab"], c)
    tail = min(t, CONV_W - 1)
    new_conv = jnp.concatenate([conv0, pc[:, t - tail:, :3 * MIX]], axis=1)[:, -(CONV_W - 1):]

    x1, h2, comb = _merge(xf, oa.reshape(n, MIX), ob.reshape(n, MIX), oc.reshape(n, MIX),
                          [sh_mix, sc_mix, g_mix, sh_ffn, sc_ffn], wts["norm_mix_w"], wts["norm_ffn_w"],
                          wts["w_gate"], wts["b_gate"], wts["w_branch"], wts["w_out"], wts["w_router_t"],
                          wts["router_bias"], tm, tiles_per_group)
    x2 = _moe(x1, h2, comb, g_ffn, wts["wg"], wts["wu"], wts["wd"], tm, tiles_per_group)
    heads = lambda z: z.reshape(bsz, -1, N_HEADS, HEAD_DIM)
    return x2.reshape(bsz, t, D_MODEL), (pa[:, -1], new_wkv, heads(new_k), heads(new_v), new_conv,
                                         _unpair_state(s_new))


def _prepare_layer(l, w_in, norm_mix_w, norm_ffn_w, rwkv_mu, rwkv_w0, rwkv_w_up, rwkv_a0, rwkv_a_up, rwkv_g_up,
                   rwkv_k_k, rwkv_k_a, rwkv_r_k, rwkv_gn_w, rwkv_gn_b, band_q_norm, band_k_norm, band_rel_bias,
                   gdn_conv_w, gdn_a_log, gdn_dt_bias, gdn_norm_w, w_branch, w_gate, b_gate, w_out,
                   w_router, router_bias, w_exp_gate, w_exp_up, w_exp_down):
    row = lambda z: z.reshape(1, -1).astype(F32)
    per_head = lambda z: jnp.repeat(z, HEAD_DIM).reshape(1, MIX)
    win = w_in[l].astype(BF16)
    c0 = A_COLS + B_COLS
    wab = jnp.pad(win[:, c0 + C_MAIN:], ((0, 0), (0, LANES - 2 * N_HEADS)))
    zeros = jnp.zeros((HEAD_DIM, MIX), F32)
    wwa = jnp.concatenate([jnp.concatenate([rwkv_w_up[l], zeros], axis=1),
                           jnp.concatenate([zeros, rwkv_a_up[l]], axis=1)], axis=0)
    head_of_lane = jnp.arange(MIX) // HEAD_DIM
    src = jnp.arange(LANES)[:, None]
    eab = jnp.concatenate([(src == head_of_lane[None, :]), (src == N_HEADS + head_of_lane[None, :])],
                          axis=1).astype(BF16)
    return dict(
        norm_mix_w=row(norm_mix_w[l]), norm_ffn_w=row(norm_ffn_w[l]),
        wa=win[:, :A_COLS], wb=win[:, A_COLS:c0], wc=win[:, c0:c0 + C_MAIN], wab=wab,
        q_norm=row(jnp.tile(band_q_norm[l], N_HEADS)), k_norm=row(jnp.tile(band_k_norm[l], N_HEADS)),
        mu=row(rwkv_mu[l]), w0=row(rwkv_w0[l]), a0=row(rwkv_a0[l]), wwa=wwa, g_up=rwkv_g_up[l],
        k_k=row(rwkv_k_k[l]), k_a=row(rwkv_k_a[l]), r_k=row(rwkv_r_k[l]), gn_w=row(rwkv_gn_w[l]),
        gn_b=row(rwkv_gn_b[l]), rel_bias=band_rel_bias[l],
        conv_w=gdn_conv_w[l], a_log=per_head(gdn_a_log[l]), dt_bias=per_head(gdn_dt_bias[l]),
        gdn_norm_w=row(jnp.tile(gdn_norm_w[l], N_HEADS)), eab=eab,
        w_gate=w_gate[l].astype(BF16), b_gate=row(b_gate[l]), w_branch=w_branch[l].astype(BF16),
        w_out=w_out[l].astype(BF16), w_router_t=jnp.transpose(w_router), router_bias=router_bias.reshape(-1, 1),
        wg=w_exp_gate[l].astype(BF16), wu=w_exp_up[l].astype(BF16),
        wd=w_exp_down[l].astype(BF16).reshape(N_GROUPS, -1, D_MODEL),
    )


def kernel(x_prompt, x_sample, c_prompt, c_sample, state_rwkv_shift, state_rwkv_wkv, cache_band_k, cache_band_v, state_gdn_conv, state_gdn_S, w_ada, b_ada, norm_mix_w, norm_ffn_w, w_in, rwkv_mu, rwkv_w0, rwkv_w_up, rwkv_a0, rwkv_a_up, rwkv_g_up, rwkv_k_k, rwkv_k_a, rwkv_r_k, rwkv_gn_w, rwkv_gn_b, band_q_norm, band_k_norm, band_rel_bias, gdn_conv_w, gdn_a_log, gdn_dt_bias, gdn_norm_w, w_branch, w_gate, b_gate, w_out, w_router, router_bias, w_exp_gate, w_exp_up, w_exp_down):
    depth = w_ada.shape[0]
    bsz = x_prompt.shape[0]
    mod_p = _adaln(c_prompt, w_ada, b_ada)
    mod_s = _adaln(c_sample, w_ada, b_ada)
    zero_state = (jnp.zeros((bsz, A_COLS), F32), jnp.zeros((bsz, N_HEADS, HEAD_DIM, HEAD_DIM), F32), None,
                  jnp.zeros((bsz, CONV_W - 1, 3 * MIX), F32), jnp.zeros((bsz, N_HEADS, HEAD_DIM, HEAD_DIM), F32))
    k_cache = cache_band_k.reshape(cache_band_k.shape[:3] + (MIX,))
    v_cache = cache_band_v.reshape(cache_band_v.shape[:3] + (MIX,))
    tm = TOKEN_TILE
    xp, xs = x_prompt, x_sample
    new_p, new_s = [], []
    for l in range(depth):
        wts = _prepare_layer(l, w_in, norm_mix_w, norm_ffn_w, rwkv_mu, rwkv_w0, rwkv_w_up, rwkv_a0, rwkv_a_up,
                             rwkv_g_up, rwkv_k_k, rwkv_k_a, rwkv_r_k, rwkv_gn_w, rwkv_gn_b, band_q_norm,
                             band_k_norm, band_rel_bias, gdn_conv_w, gdn_a_log, gdn_dt_bias, gdn_norm_w,
                             w_branch, w_gate, b_gate, w_out, w_router, router_bias, w_exp_gate, w_exp_up,
                             w_exp_down)
        xp, st_p = _layer(xp, mod_p[l], zero_state, wts, tm)
        xs, st_s = _layer(xs, mod_s[l], (state_rwkv_shift[l], state_rwkv_wkv[l], (k_cache, v_cache, l),
                                         state_gdn_conv[l], state_gdn_S[l]), wts, tm)
        new_p.append(st_p)
        new_s.append(st_s)
    p_out = [jnp.stack(z, axis=0) for z in zip(*new_p)]
    s_out = [jnp.stack(z, axis=0) for z in zip(*new_s)]
    return (xp, xs, *p_out, *s_out)
```

```python
import functools
import math

import jax
import jax.numpy as jnp
from jax import lax
from jax.experimental import pallas as pl
from jax.experimental.pallas import tpu as pltpu

F32 = jnp.float32
BF16 = jnp.bfloat16

D_MODEL = 1024
MIX = 512
HEAD_DIM = 64
N_HEADS = MIX // HEAD_DIM
LANES = 128
N_PAIRS = MIX // LANES
SUBLANES = 8
CHUNK = 64
TOKEN_TILE = 512
MIXER_ROWS = 256
BAND_QUERIES = 2 * CHUNK
A_COLS = 3 * MIX + 64 + 64 + 128
B_COLS = 3 * MIX
C_MAIN = 4 * MIX
CONV_W = 4
BAND = 8 * CHUNK
MAX_REL = 2 * CHUNK
N_EXPERTS = 16
N_GROUPS = 4
EXPERT_DIM = D_MODEL // 4
RMS_EPS = 1e-6
A_GN_EPS = 64e-5
A_DECAY_SCALE = math.exp(-0.5)
VMEM_LIMIT_BYTES = 56 * 1024 * 1024


def _params(*sem):
    return pltpu.CompilerParams(dimension_semantics=sem, vmem_limit_bytes=VMEM_LIMIT_BYTES)


def _dot(a, b, precision=None):
    return lax.dot_general(a, b, (((1,), (0,)), ((), ())), precision=precision, preferred_element_type=F32)


def _dot_nt(a, b, precision=None):
    return lax.dot_general(a, b, (((1,), (1,)), ((), ())), precision=precision, preferred_element_type=F32)


def _bdot(a, b):
    return _dot(a.astype(BF16), b.astype(BF16))


def _bdot_nt(a, b):
    return _dot_nt(a.astype(BF16), b.astype(BF16))


def _dot_split(a, b, nt=False):
    dot = _dot_nt if nt else _dot
    a_hi, a_lo = _bf16_terms(a, 2)
    b_hi, b_lo = _bf16_terms(b, 2)
    return dot(a_hi, b_hi) + (dot(a_hi, b_lo) + dot(a_lo, b_hi))


def _bmm(a, b):
    return lax.dot_general(a.astype(BF16), b.astype(BF16), (((2,), (1,)), ((0,), (0,))),
                           preferred_element_type=F32)


def _bmm_nt(a, b):
    return lax.dot_general(a.astype(BF16), b.astype(BF16), (((2,), (2,)), ((0,), (0,))),
                           preferred_element_type=F32)


def _sigmoid(x):
    return 1.0 / (1.0 + jnp.exp(-x))


def _silu(x):
    return x * _sigmoid(x)


def _softplus(x):
    return jnp.maximum(x, 0.0) + jnp.log1p(jnp.exp(-jnp.abs(x)))


def _iota(shape, dim):
    return lax.broadcasted_iota(jnp.int32, shape, dim)


def _bf16_terms(x, terms):
    parts = []
    for _ in range(terms):
        part = x.astype(BF16)
        parts.append(part)
        x = x - part.astype(F32)
    return parts


def _dot_01(x, m, terms):
    out = None
    for part in _bf16_terms(x, terms):
        term = _dot(part, m)
        out = term if out is None else out + term
    return out


def _dot_01_left(m, x, terms):
    out = None
    for part in _bf16_terms(x, terms):
        term = _dot(m, part)
        out = term if out is None else out + term
    return out


def _head_ones():
    return (_iota((LANES, LANES), 0) // HEAD_DIM == _iota((LANES, LANES), 1) // HEAD_DIM).astype(BF16)


def _head_sum(x, ones):
    parts = [_dot_01(x[:, g * LANES:(g + 1) * LANES], ones, 2) for g in range(x.shape[1] // LANES)]
    return parts[0] if len(parts) == 1 else jnp.concatenate(parts, axis=1)


def _stack(x):
    lane = _iota(x.shape, 1)
    return jnp.concatenate([jnp.where(lane < HEAD_DIM, x, 0.0), jnp.where(lane >= HEAD_DIM, x, 0.0)], axis=0)


def _tiles(x, c):
    return jnp.stack([x[u * c:(u + 1) * c, j * LANES:(j + 1) * LANES]
                      for u in range(x.shape[0] // c) for j in range(N_PAIRS)])


def _untile(xs):
    return jnp.concatenate([jnp.concatenate([xs[u * N_PAIRS + j] for j in range(N_PAIRS)], axis=1)
                            for u in range(xs.shape[0] // N_PAIRS)], axis=0)


def _block_diag(x):
    x = x.astype(BF16)
    first = _iota(x.shape[1:], 1) < x.shape[2] // 2
    zero = jnp.zeros_like(x)
    return jnp.concatenate([jnp.where(first, x, zero), jnp.where(first, zero, x)], axis=1)


def _keep_head_blocks(x):
    same = _iota(x.shape[1:], 0) // HEAD_DIM == _iota(x.shape[1:], 1) // HEAD_DIM
    return jnp.where(same, x, 0.0)


def _transpose_tiles(xs):
    return jnp.stack([jnp.transpose(xs[g]) for g in range(xs.shape[0])])


def _chunk_last(x, c):
    return jnp.concatenate([jnp.broadcast_to(x[(u + 1) * c - 1:(u + 1) * c], (c, x.shape[1]))
                            for u in range(x.shape[0] // c)], axis=0)


def _column_scale(row_values, c):
    tiles = []
    for u in range(row_values.shape[0] // c):
        last = row_values[(u + 1) * c - 1:(u + 1) * c]
        for j in range(N_PAIRS):
            tiles.append(jnp.transpose(jnp.broadcast_to(last[:, j * LANES:(j + 1) * LANES], (LANES, LANES))))
    return jnp.stack(tiles)


def _chunk_tril(rows, c):
    row, col = _iota((rows, rows), 0), _iota((rows, rows), 1)
    return ((col <= row) & (row // c == col // c)).astype(BF16)


def _tri_masks(c):
    row, col = _iota((c, 2 * c), 0), _iota((c, 2 * c), 1) % c
    return col < row, col <= row


def _unit_lower_inverse(a, c):
    row, col = _iota((c, 2 * c), 0), _iota((c, 2 * c), 1) % c
    eye = (row == col).astype(F32)
    same = lambda s: row // s == col // s
    a0 = jnp.where(same(SUBLANES), a, 0.0)
    a2 = _bmm(a0, _block_diag(a0))
    a4 = _bmm(a2, _block_diag(a2))
    x = _bmm(_bmm(eye - a0, _block_diag(eye + a2)), _block_diag(eye + a4))
    s = SUBLANES
    while s < c:
        off = jnp.where(same(2 * s) & jnp.logical_not(same(s)), a, 0.0)
        x = x - _bmm(_bmm(x, _block_diag(off)), _block_diag(x))
        s *= 2
    return x


def _adaln_kernel(c_ref, w_ref, b_ref, o_ref):
    o_ref[0] = _dot_split(_silu(c_ref[...]), w_ref[0]) + b_ref[0]


def _adaln(c, w_ada, b_ada):
    depth, bsz, tn = w_ada.shape[0], c.shape[0], 768
    return pl.pallas_call(
        _adaln_kernel,
        grid=(depth, 6 * D_MODEL // tn),
        in_specs=[pl.BlockSpec((bsz, D_MODEL), lambda l, j: (0, 0)),
                  pl.BlockSpec((1, D_MODEL, tn), lambda l, j: (l, 0, j)),
                  pl.BlockSpec((1, 1, tn), lambda l, j: (l, 0, j))],
        out_specs=pl.BlockSpec((1, bsz, tn), lambda l, j: (l, 0, j)),
        out_shape=jax.ShapeDtypeStruct((depth, bsz, 6 * D_MODEL), F32),
        compiler_params=_params("arbitrary", "arbitrary"),
        name="adaln",
    )(c, w_ada, b_ada.reshape(depth, 1, 6 * D_MODEL))


def _norm_mod(x, norm_w, scale, shift):
    y = x * lax.rsqrt(jnp.mean(x * x, axis=-1, keepdims=True) + RMS_EPS)
    return y * norm_w * (1.0 + scale) + shift


def _inproj_kernel(x_ref, sh_ref, sc_ref, nw_ref, wa_ref, wb_ref, wc_ref, wab_ref, qn_ref, kn_ref,
                   pa_ref, q_ref, k_ref, v_ref, pc_ref, pab_ref):
    h = _norm_mod(x_ref[...], nw_ref[...], sc_ref[0], sh_ref[0]).astype(BF16)
    pa_ref[...] = _dot(h, wa_ref[...])
    pb = _dot(h, wb_ref[...])
    ones = _head_ones()

    def head_rms(y, w):
        return y * lax.rsqrt(_head_sum(y * y, ones) * (1.0 / HEAD_DIM) + RMS_EPS) * w

    q_ref[...] = head_rms(pb[:, :MIX], qn_ref[...])
    k_ref[...] = head_rms(pb[:, MIX:2 * MIX], kn_ref[...])
    v_ref[...] = pb[:, 2 * MIX:]
    pc_ref[...] = _dot(h, wc_ref[...])
    pab_ref[...] = _dot(h, wab_ref[...])


def _mod_spec(mod, tm, tiles_per_group):
    rows = mod.shape[1]
    return pl.BlockSpec((1, rows, D_MODEL), lambda i: (i // tiles_per_group, 0, 0))


def _const_spec(shape):
    zeros = (0,) * len(shape)
    return pl.BlockSpec(shape, lambda *_: zeros, pipeline_mode=pl.Buffered(1))


def _inproj(x, shift, scale, norm_w, wa, wb, wc, wab, q_norm, k_norm, tm, tiles_per_group):
    n = x.shape[0]
    row = lambda w: pl.BlockSpec((tm, w), lambda i: (i, 0))
    widths = (A_COLS, MIX, MIX, MIX, C_MAIN, LANES)
    return pl.pallas_call(
        _inproj_kernel,
        grid=(n // tm,),
        in_specs=[row(D_MODEL), _mod_spec(shift, tm, tiles_per_group), _mod_spec(scale, tm, tiles_per_group),
                  _const_spec((1, D_MODEL)), _const_spec(wa.shape), _const_spec(wb.shape), _const_spec(wc.shape),
                  _const_spec(wab.shape), _const_spec((1, MIX)), _const_spec((1, MIX))],
        out_specs=[row(w) for w in widths],
        out_shape=[jax.ShapeDtypeStruct((n, w), F32) for w in widths],
        compiler_params=_params("arbitrary"),
        name="inproj",
    )(x, shift, scale, norm_w, wa, wb, wc, wab, q_norm, k_norm)


def _rwkv_kernel(p_ref, shift0_ref, st0_ref, mu_ref, w0_ref, a0_ref, wwa_ref, gup_ref, kk_ref, ka_ref, rk_ref,
                 gnw_ref, gnb_ref, o_ref, st_out_ref, ext_scr, st_scr, *, c):
    bb, tt, _ = p_ref.shape
    rows, nch, n = bb * tt, tt // c, 2 * c
    assert bb == 1 or nch == 1
    ci = pl.program_id(1)

    @pl.when(ci == 0)
    def _():
        ext_scr[:, SUBLANES - 1:SUBLANES, :] = shift0_ref[...]
        st_scr[...] = st0_ref[...].reshape(bb * N_PAIRS, LANES, LANES)

    ext_scr[:, SUBLANES:SUBLANES + tt, :] = p_ref[...]
    p = p_ref[...].reshape(rows, A_COLS)
    p_prev = ext_scr[:, SUBLANES - 1:SUBLANES - 1 + tt, :].reshape(rows, A_COLS)
    ext_scr[:, 0:SUBLANES, :] = ext_scr[:, tt:tt + SUBLANES, :]
    xs = p + (p_prev - p) * mu_ref[...]
    r, k, v = xs[:, :MIX], xs[:, MIX:2 * MIX], xs[:, 2 * MIX:3 * MIX]
    lora_in = xs[:, 3 * MIX:3 * MIX + LANES]
    gd = xs[:, 3 * MIX + LANES:]
    lane = _iota((1, LANES), 1)
    lora = _bdot(jnp.where(lane < HEAD_DIM, jnp.tanh(lora_in), lora_in), wwa_ref[...])
    log_w = -A_DECAY_SCALE * _sigmoid(w0_ref[...] + lora[:, :MIX])
    a = _sigmoid(a0_ref[...] + lora[:, MIX:])
    g = _bdot(_sigmoid(gd), gup_ref[...])
    ones = _head_ones()
    kk_raw = k * kk_ref[...]
    kk = kk_raw * lax.rsqrt(_head_sum(kk_raw * kk_raw, ones) + RMS_EPS)
    k = k * (1.0 + (a - 1.0) * ka_ref[...])
    b = kk * a
    cum = _dot_01_left(_chunk_tril(rows, c), log_w, 3)
    cum_last = _chunk_last(cum, c)
    e_cum, e_neg = jnp.exp(cum), jnp.exp(-cum)
    e_prev, e_tail = jnp.exp(cum - log_w), jnp.exp(cum_last - cum)
    strict, incl = _tri_masks(c)

    kk_g, r_g, v_g = _tiles(kk * e_prev, c), _tiles(r * e_cum, c), _tiles(v, c)
    amat = _bmm_nt(jnp.concatenate([kk_g, r_g], axis=1),
                   jnp.concatenate([_block_diag(_tiles(k * e_neg, c)), _block_diag(_tiles(b * e_neg, c))],
                                   axis=1))
    a_kk = jnp.where(strict, amat[:, :c, :n], 0.0)
    a_bk = jnp.where(strict, amat[:, :c, n:], 0.0)
    a_rk = jnp.where(incl, amat[:, c:, :n], 0.0)
    a_rb = jnp.where(incl, amat[:, c:, n:], 0.0)
    t_inv = _unit_lower_inverse(a_bk, c)
    av = _bmm(jnp.concatenate([a_kk, a_rk], axis=1), _block_diag(v_g))
    wu = _bmm(t_inv, jnp.concatenate([_block_diag(kk_g), _block_diag(av[:, :c])], axis=2))
    kd_v = _keep_head_blocks(_bmm(_transpose_tiles(_tiles(k * e_tail, c)), v_g))
    bd_t = _transpose_tiles(_tiles(b * e_tail, c))
    decay = _column_scale(e_cum, c)

    st = st_scr[...]
    per_step = bb * N_PAIRS
    o_tiles = []
    for ch in range(nch):
        sel = slice(ch * per_step, (ch + 1) * per_step)
        zo = _bmm(jnp.concatenate([wu[sel, :, :LANES], r_g[sel]], axis=1), st)
        z = wu[sel, :, LANES:] + zo[:, :c]
        o_tiles.append(zo[:, c:] + av[sel, c:] - _bmm(a_rb[sel], _block_diag(z)))
        st = decay[sel] * st + kd_v[sel] - _keep_head_blocks(_bmm(bd_t[sel], z))
    st_scr[...] = st
    o = _untile(o_tiles[0] if nch == 1 else jnp.concatenate(o_tiles, axis=0))

    mean = _head_sum(o, ones) * (1.0 / HEAD_DIM)
    cen = o - mean
    var = _head_sum(cen * cen, ones) * (1.0 / HEAD_DIM)
    o = cen * lax.rsqrt(var + A_GN_EPS) * gnw_ref[...] + gnb_ref[...]
    bonus = _head_sum(r * k * rk_ref[...], ones) * v
    o_ref[...] = ((o + bonus) * g).reshape(bb, tt, MIX)
    st_out_ref[...] = st.reshape(bb, N_PAIRS, LANES, LANES)


def _mixer_tiling(bsz, t, c):
    if t > c:
        return 1, min(t, MIXER_ROWS)
    return min(bsz, MIXER_ROWS // t), t


def _rwkv(pa, shift0, st0, mu, w0, a0, wwa, gup, k_k, k_a, r_k, gn_w, gn_b, c):
    bsz, t, _ = pa.shape
    bb, tt = _mixer_tiling(bsz, t, c)
    vec = lambda w: _const_spec((1, w))
    state = pl.BlockSpec((bb, N_PAIRS, LANES, LANES), lambda b, i: (b, 0, 0, 0))
    return pl.pallas_call(
        functools.partial(_rwkv_kernel, c=c),
        grid=(bsz // bb, t // tt),
        in_specs=[pl.BlockSpec((bb, tt, A_COLS), lambda b, i: (b, i, 0)),
                  pl.BlockSpec((bb, 1, A_COLS), lambda b, i: (b, 0, 0)), state,
                  vec(A_COLS), vec(MIX), vec(MIX), _const_spec(wwa.shape), _const_spec(gup.shape),
                  vec(MIX), vec(MIX), vec(MIX), vec(MIX), vec(MIX)],
        out_specs=[pl.BlockSpec((bb, tt, MIX), lambda b, i: (b, i, 0)), state],
        out_shape=[jax.ShapeDtypeStruct((bsz, t, MIX), F32),
                   jax.ShapeDtypeStruct((bsz, N_PAIRS, LANES, LANES), F32)],
        scratch_shapes=[pltpu.VMEM((bb, tt + SUBLANES, A_COLS), F32),
                        pltpu.VMEM((bb * N_PAIRS, LANES, LANES), F32)],
        compiler_params=_params("arbitrary", "arbitrary"),
        name="rwkv",
    )(pa, shift0, st0, mu, w0, a0, wwa, gup, k_k, k_a, r_k, gn_w, gn_b)


def _gdn_kernel(pc_ref, pab_ref, conv0_ref, st0_ref, convw_ref, alog_ref, dtb_ref, nw_ref, eab_ref,
                o_ref, st_out_ref, ext_scr, st_scr, *, c):
    bb, tt, _ = pc_ref.shape
    rows, nch, n = bb * tt, tt // c, 2 * c
    assert bb == 1 or nch == 1
    w3 = 3 * MIX
    ci = pl.program_id(1)

    @pl.when(ci == 0)
    def _():
        ext_scr[:, 0:SUBLANES, :] = conv0_ref[...]
        st_scr[...] = st0_ref[...].reshape(bb * N_PAIRS, LANES, LANES)

    ext_scr[:, SUBLANES:SUBLANES + tt, :] = pc_ref[:, :, :w3]
    conv = ext_scr[:, SUBLANES:SUBLANES + tt, :] * convw_ref[CONV_W - 1:CONV_W, :]
    for s in range(1, CONV_W):
        conv = conv + ext_scr[:, SUBLANES - s:SUBLANES - s + tt, :] * convw_ref[CONV_W - 1 - s:CONV_W - s, :]
    ext_scr[:, 0:SUBLANES, :] = ext_scr[:, tt:tt + SUBLANES, :]
    qkv = _silu(conv.reshape(rows, w3))
    ones = _head_ones()
    l2n = lambda y: y * lax.rsqrt(_head_sum(y * y, ones) + RMS_EPS)
    q = l2n(qkv[:, :MIX]) * (HEAD_DIM ** -0.5)
    k = l2n(qkv[:, MIX:2 * MIX])
    v = qkv[:, 2 * MIX:]
    zgate = pc_ref[:, :, w3:].reshape(rows, MIX)
    ab = _dot_01(pab_ref[...].reshape(rows, LANES), eab_ref[...], 3)
    log_alpha = -jnp.exp(alog_ref[...]) * _softplus(ab[:, :MIX] + dtb_ref[...])
    beta = _sigmoid(ab[:, MIX:])
    gcum = _dot_01_left(_chunk_tril(rows, c), log_alpha, 3)
    eg = jnp.exp(gcum)
    bk = beta * k
    strict, incl = _tri_masks(c)

    g_t = _tiles(gcum, c)
    g_col = jnp.concatenate([g_t[:, :, :c], g_t[:, :, HEAD_DIM:HEAD_DIM + c]], axis=2)
    g_time = _transpose_tiles(g_t)
    g_row = jnp.concatenate([g_time[:, 0:1, :], g_time[:, HEAD_DIM:HEAD_DIM + 1, :]], axis=2)
    diff = g_col - g_row
    dec_s = jnp.where(strict, jnp.exp(jnp.where(strict, diff, 0.0)), 0.0)
    dec_i = jnp.where(incl, jnp.exp(jnp.where(incl, diff, 0.0)), 0.0)
    qg_g = _tiles(q * eg, c)
    kmat = _bmm_nt(jnp.concatenate([_tiles(bk, c), _tiles(q, c)], axis=1), _block_diag(_tiles(k, c)))
    attn = kmat[:, c:] * dec_i
    t_inv = _unit_lower_inverse(kmat[:, :c] * dec_s, c)
    sol = _bmm(t_inv, jnp.concatenate([_block_diag(_tiles(beta * v, c)), _block_diag(_tiles(bk * eg, c))],
                                      axis=2))
    kd_t = _transpose_tiles(_tiles(k * jnp.exp(_chunk_last(gcum, c) - gcum), c))
    decay = _column_scale(eg, c)

    st = st_scr[...]
    per_step = bb * N_PAIRS
    o_tiles = []
    for ch in range(nch):
        sel = slice(ch * per_step, (ch + 1) * per_step)
        wq = _bmm(jnp.concatenate([sol[sel, :, LANES:], qg_g[sel]], axis=1), st)
        delta = sol[sel, :, :LANES] - wq[:, :c]
        o_tiles.append(wq[:, c:] + _bmm(attn[sel], _block_diag(delta)))
        st = decay[sel] * st + _keep_head_blocks(_bmm(kd_t[sel], delta))
    st_scr[...] = st
    o = _untile(o_tiles[0] if nch == 1 else jnp.concatenate(o_tiles, axis=0))

    o = o * lax.rsqrt(_head_sum(o * o, ones) * (1.0 / HEAD_DIM) + RMS_EPS) * nw_ref[...]
    o_ref[...] = (o * _silu(zgate)).reshape(bb, tt, MIX)
    st_out_ref[...] = st.reshape(bb, N_PAIRS, LANES, LANES)


def _gdn(pc, pab, conv0, st0, conv_w, a_log, dt_bias, norm_w, eab, c):
    bsz, t, _ = pc.shape
    bb, tt = _mixer_tiling(bsz, t, c)
    vec = lambda w: _const_spec((1, w))
    state = pl.BlockSpec((bb, N_PAIRS, LANES, LANES), lambda b, i: (b, 0, 0, 0))
    return pl.pallas_call(
        functools.partial(_gdn_kernel, c=c),
        grid=(bsz // bb, t // tt),
        in_specs=[pl.BlockSpec((bb, tt, C_MAIN), lambda b, i: (b, i, 0)),
                  pl.BlockSpec((bb, tt, LANES), lambda b, i: (b, i, 0)),
                  pl.BlockSpec((bb, SUBLANES, 3 * MIX), lambda b, i: (b, 0, 0)), state,
                  _const_spec((CONV_W, 3 * MIX)), vec(MIX), vec(MIX), vec(MIX), _const_spec(eab.shape)],
        out_specs=[pl.BlockSpec((bb, tt, MIX), lambda b, i: (b, i, 0)), state],
        out_shape=[jax.ShapeDtypeStruct((bsz, t, MIX), F32),
                   jax.ShapeDtypeStruct((bsz, N_PAIRS, LANES, LANES), F32)],
        scratch_shapes=[pltpu.VMEM((bb, tt + SUBLANES, 3 * MIX), F32),
                        pltpu.VMEM((bb * N_PAIRS, LANES, LANES), F32)],
        compiler_params=_params("arbitrary", "arbitrary"),
        name="gdn",
    )(pc, pab, conv0, st0, conv_w, a_log, dt_bias, norm_w, eab)


def _band_core(q, k_g, v_g, bias):
    bb, cq, _ = q.shape
    wn = k_g.shape[1]
    q = q * (HEAD_DIM ** -0.5)
    q_g = jnp.stack([_stack(q[b, :, j * LANES:(j + 1) * LANES])
                     for b in range(bb) for j in range(N_PAIRS)]).astype(BF16)
    s = _bmm_nt(q_g, k_g)
    s = (s.reshape(bb, N_PAIRS, 2 * cq, wn) + bias).reshape(bb * N_PAIRS, 2 * cq, wn)
    e = jnp.exp(s - jnp.max(s, axis=-1, keepdims=True))
    pv = _bmm(e, v_g) / jnp.sum(e, axis=-1, keepdims=True)
    lane = _iota((cq, LANES), 1)
    return jnp.stack([
        jnp.concatenate([jnp.where(lane < HEAD_DIM, pv[b * N_PAIRS + j, :cq], pv[b * N_PAIRS + j, cq:])
                         for j in range(N_PAIRS)], axis=1) for b in range(bb)])


def _band_prompt_kernel(q_ref, k_ref, v_ref, bias_ref, o_ref):
    cq = q_ref.shape[1]
    wn = BAND + cq
    first = pl.program_id(1) * cq - BAND
    start = pl.multiple_of(jnp.maximum(first, 0), LANES)
    shift = pl.multiple_of(start - first, LANES)
    window = lambda ref: jnp.stack([ref[0, pl.ds(start, wn), j * LANES:(j + 1) * LANES]
                                    for j in range(N_PAIRS)]).astype(BF16)
    o_ref[...] = _band_core(q_ref[...], window(k_ref), window(v_ref), bias_ref[:, :, pl.ds(shift, wn)])


def _band_prompt(q, k, v, table):
    bsz, t, _ = q.shape
    cq = BAND_QUERIES
    span = 2 * BAND + cq
    q_chunk = jnp.arange(cq)[:, None] // CHUNK
    k_chunk = jnp.arange(span)[None, :] // CHUNK
    in_band = (k_chunk >= q_chunk) & (k_chunk <= q_chunk + BAND // CHUNK)
    bias = jnp.where(in_band, _rel_bias(table, cq, span), -1e30).reshape(N_PAIRS, 2 * cq, span)
    full = pl.BlockSpec((1, t, MIX), lambda b, i: (b, 0, 0))
    return pl.pallas_call(
        _band_prompt_kernel,
        grid=(bsz, t // cq),
        in_specs=[pl.BlockSpec((1, cq, MIX), lambda b, i: (b, i, 0)), full, full, _const_spec(bias.shape)],
        out_specs=pl.BlockSpec((1, cq, MIX), lambda b, i: (b, i, 0)),
        out_shape=jax.ShapeDtypeStruct((bsz, t, MIX), F32),
        compiler_params=_params("arbitrary", "arbitrary"),
        name="band_prompt",
    )(q, k, v, bias)


def _band_sample_kernel(q_ref, kp_ref, vp_ref, kn_ref, vn_ref, bias_ref, o_ref):
    bb = q_ref.shape[0]
    keys = lambda past, new: jnp.stack([
        jnp.concatenate([past[b, :, j * LANES:(j + 1) * LANES], new[b, :, j * LANES:(j + 1) * LANES]], axis=0)
        for b in range(bb) for j in range(N_PAIRS)]).astype(BF16)
    o_ref[...] = _band_core(q_ref[...], keys(kp_ref, kn_ref), keys(vp_ref, vn_ref), bias_ref[...])


def _band_sample(q, k, v, caches, table):
    k_past, v_past, layer = caches
    bsz, t, _ = q.shape
    past = k_past.shape[2]
    bb = min(bsz, MIXER_ROWS // t)
    bias = _rel_bias(table, t, past + t).reshape(N_PAIRS, 2 * t, past + t)
    new = pl.BlockSpec((bb, t, MIX), lambda b: (b, 0, 0))
    old = pl.BlockSpec((None, bb, past, MIX), lambda b: (layer, b, 0, 0))
    return pl.pallas_call(
        _band_sample_kernel,
        grid=(bsz // bb,),
        in_specs=[new, old, old, new, new, _const_spec(bias.shape)],
        out_specs=new,
        out_shape=jax.ShapeDtypeStruct((bsz, t, MIX), F32),
        compiler_params=_params("arbitrary"),
        name="band_sample",
    )(q, k_past, v_past, k, v, bias)


def _merge_kernel(x_ref, oa_ref, ob_ref, oc_ref, sh_ref, sc_ref, gm_ref, sh2_ref, sc2_ref, nw_ref, nw2_ref,
                  wgate_ref, bgate_ref, wbr_ref, wout_ref, wrt_ref, rb_ref,
                  x1_ref, h2_ref, comb_ref):
    x = x_ref[...]
    h = _norm_mod(x, nw_ref[...], sc_ref[0], sh_ref[0]).astype(BF16)
    gates = _sigmoid(_dot(h, wgate_ref[...]) + bgate_ref[...])
    mixed = None
    for i, o_ref in enumerate((oa_ref, ob_ref, oc_ref)):
        term = gates[:, i * D_MODEL:(i + 1) * D_MODEL] * _bdot(o_ref[...], wbr_ref[i])
        mixed = term if mixed is None else mixed + term
    x1 = x + gm_ref[0] * _bdot(mixed, wout_ref[...])
    x1_ref[...] = x1
    h2 = _norm_mod(x1, nw2_ref[...], sc2_ref[0], sh2_ref[0])
    h2_ref[...] = h2.astype(BF16)

    scores = _sigmoid(_dot_split(wrt_ref[...], h2, nt=True))
    sel = scores + rb_ref[...]
    tm = scores.shape[1]
    per = N_EXPERTS // N_GROUPS
    best_val, best = None, None
    for g in range(N_GROUPS):
        rows = [sel[g * per + i:g * per + i + 1, :] for i in range(per)]
        top2 = None
        for i in range(per):
            for i2 in range(i + 1, per):
                pair = rows[i] + rows[i2]
                top2 = pair if top2 is None else jnp.maximum(top2, pair)
        if g == 0:
            best_val, best = top2, jnp.zeros((1, tm), jnp.int32)
        else:
            better = top2 > best_val
            best = jnp.where(better, g, best)
            best_val = jnp.where(better, top2, best_val)
    eidx = _iota((N_EXPERTS, tm), 0)
    cand = jnp.where(eidx // per == best, sel, -jnp.inf)
    m1 = jnp.max(cand, axis=0, keepdims=True)
    i1 = jnp.min(jnp.where(cand == m1, eidx, N_EXPERTS), axis=0, keepdims=True)
    cand2 = jnp.where(eidx == i1, -jnp.inf, cand)
    m2 = jnp.max(cand2, axis=0, keepdims=True)
    i2 = jnp.min(jnp.where(cand2 == m2, eidx, N_EXPERTS), axis=0, keepdims=True)
    w1 = jnp.sum(jnp.where(eidx == i1, scores, 0.0), axis=0, keepdims=True)
    w2 = jnp.sum(jnp.where(eidx == i2, scores, 0.0), axis=0, keepdims=True)
    den = w1 + w2
    comb_ref[...] = jnp.where(eidx == i1, w1 / den, 0.0) + jnp.where(eidx == i2, w2 / den, 0.0)


def _merge(x, oa, ob, oc, mods, norm_w, norm2_w, w_gate, b_gate, w_branch, w_out, w_router_t, router_bias,
           tm, tiles_per_group):
    n = x.shape[0]
    row = lambda w: pl.BlockSpec((tm, w), lambda i: (i, 0))
    mod_specs = [_mod_spec(m, tm, tiles_per_group) for m in mods]
    return pl.pallas_call(
        _merge_kernel,
        grid=(n // tm,),
        in_specs=[row(D_MODEL), row(MIX), row(MIX), row(MIX)] + mod_specs
                 + [_const_spec((1, D_MODEL)), _const_spec((1, D_MODEL)), _const_spec(w_gate.shape),
                    _const_spec(b_gate.shape), _const_spec(w_branch.shape), _const_spec(w_out.shape),
                    _const_spec(w_router_t.shape), _const_spec(router_bias.shape)],
        out_specs=[row(D_MODEL), row(D_MODEL), pl.BlockSpec((N_EXPERTS, tm), lambda i: (0, i))],
        out_shape=[jax.ShapeDtypeStruct((n, D_MODEL), F32), jax.ShapeDtypeStruct((n, D_MODEL), BF16),
                   jax.ShapeDtypeStruct((N_EXPERTS, n), F32)],
        compiler_params=_params("arbitrary"),
        name="merge",
    )(x, oa, ob, oc, *mods, norm_w, norm2_w, w_gate, b_gate, w_branch, w_out, w_router_t, router_bias)


def _moe_kernel(x1_ref, h2_ref, comb_ref, gf_ref, wg_ref, wu_ref, wd_ref, o_ref):
    h2 = h2_ref[...]
    comb = jnp.transpose(comb_ref[...])
    per = N_EXPERTS // N_GROUPS
    acc = None
    for g in range(N_GROUPS):
        hid = [(_silu(_dot(h2, wg_ref[e])) * _dot(h2, wu_ref[e]) * comb[:, e:e + 1]).astype(BF16)
               for e in range(g * per, (g + 1) * per)]
        term = _dot(jnp.concatenate(hid, axis=1), wd_ref[g])
        acc = term if acc is None else acc + term
    o_ref[...] = x1_ref[...] + gf_ref[0] * acc


def _moe(x1, h2, comb, g_ffn, wg, wu, wd, tm, tiles_per_group):
    n = x1.shape[0]
    row = lambda w: pl.BlockSpec((tm, w), lambda i: (i, 0))
    return pl.pallas_call(
        _moe_kernel,
        grid=(n // tm,),
        in_specs=[row(D_MODEL), row(D_MODEL), pl.BlockSpec((N_EXPERTS, tm), lambda i: (0, i)),
                  _mod_spec(g_ffn, tm, tiles_per_group),
                  _const_spec(wg.shape), _const_spec(wu.shape), _const_spec(wd.shape)],
        out_specs=row(D_MODEL),
        out_shape=jax.ShapeDtypeStruct((n, D_MODEL), F32),
        compiler_params=_params("arbitrary"),
        name="moe",
    )(x1, h2, comb, g_ffn, wg, wu, wd)


def _pair_state(s):
    bsz = s.shape[0]
    s = s.reshape(bsz, N_PAIRS, 2, HEAD_DIM, HEAD_DIM)
    z = jnp.zeros_like(s[:, :, 0])
    top = jnp.concatenate([s[:, :, 0], z], axis=-1)
    bot = jnp.concatenate([z, s[:, :, 1]], axis=-1)
    return jnp.concatenate([top, bot], axis=-2)


def _unpair_state(s):
    bsz = s.shape[0]
    return jnp.stack([s[:, :, :HEAD_DIM, :HEAD_DIM], s[:, :, HEAD_DIM:, HEAD_DIM:]], axis=2).reshape(
        bsz, N_HEADS, HEAD_DIM, HEAD_DIM)


def _rel_bias(table, cq, wn):
    span = cq - 1 + wn
    offs = jnp.arange(-(cq - 1), wn + 1)
    diag = table[:, jnp.clip(BAND - offs, -MAX_REL, MAX_REL) + MAX_REL].astype(F32)
    skew = jnp.tile(diag, (1, cq))[:, :cq * span].reshape(-1, cq, span)
    return skew[:, :, cq - 1:]


def _layer(x, mod, state, wts, tm):
    bsz, t, _ = x.shape
    n = bsz * t
    c = min(CHUNK, t)
    shift0, wkv0, caches, conv0, s0 = state
    if t % tm == 0:
        tiles_per_group = t // tm
        mods = [m.reshape(bsz, 1, D_MODEL) for m in jnp.split(mod, 6, axis=-1)]
    else:
        tiles_per_group = 1
        mods = [jnp.repeat(m, t, axis=0).reshape(n // tm, tm, D_MODEL) for m in jnp.split(mod, 6, axis=-1)]
    sh_mix, sc_mix, g_mix, sh_ffn, sc_ffn, g_ffn = mods
    xf = x.reshape(n, D_MODEL)

    pa, q, k, v, pc, pab = _inproj(xf, sh_mix, sc_mix, wts["norm_mix_w"], wts["wa"], wts["wb"], wts["wc"],
                                   wts["wab"], wts["q_norm"], wts["k_norm"], tm, tiles_per_group)
    pa = pa.reshape(bsz, t, A_COLS)
    q, k, v = (z.reshape(bsz, t, MIX) for z in (q, k, v))
    pc = pc.reshape(bsz, t, C_MAIN)

    oa, wkv = _rwkv(pa, shift0.reshape(bsz, 1, A_COLS), _pair_state(jnp.swapaxes(wkv0, -1, -2)),
                    wts["mu"], wts["w0"], wts["a0"], wts["wwa"], wts["g_up"], wts["k_k"], wts["k_a"], wts["r_k"],
                    wts["gn_w"], wts["gn_b"], c)
    new_wkv = jnp.swapaxes(_unpair_state(wkv), -1, -2)

    if caches is None:
        ob = _band_prompt(q, k, v, wts["rel_bias"])
        new_k, new_v = k[:, -BAND:], v[:, -BAND:]
    else:
        ob = _band_sample(q, k, v, caches, wts["rel_bias"])
        new_k, new_v = k, v

    conv_pad = jnp.pad(conv0, ((0, 0), (SUBLANES - (CONV_W - 1), 0), (0, 0)))
    oc, s_new = _gdn(pc, pab.reshape(bsz, t, LANES), conv_pad, _pair_state(s0), wts["conv_w"], wts["a_log"],
                     wts["dt_bias"], wts["gdn_norm_w"], wts["eab"], c)
    tail = min(t, CONV_W - 1)
    new_conv = jnp.concatenate([conv0, pc[:, t - tail:, :3 * MIX]], axis=1)[:, -(CONV_W - 1):]

    x1, h2, comb = _merge(xf, oa.reshape(n, MIX), ob.reshape(n, MIX), oc.reshape(n, MIX),
                          [sh_mix, sc_mix, g_mix, sh_ffn, sc_ffn], wts["norm_mix_w"], wts["norm_ffn_w"],
                          wts["w_gate"], wts["b_gate"], wts["w_branch"], wts["w_out"], wts["w_router_t"],
                          wts["router_bias"], tm, tiles_per_group)
    x2 = _moe(x1, h2, comb, g_ffn, wts["wg"], wts["wu"], wts["wd"], tm, tiles_per_group)
    heads = lambda z: z.reshape(bsz, -1, N_HEADS, HEAD_DIM)
    return x2.reshape(bsz, t, D_MODEL), (pa[:, -1], new_wkv, heads(new_k), heads(new_v), new_conv,
                                         _unpair_state(s_new))


def _prepare_layer(l, w_in, norm_mix_w, norm_ffn_w, rwkv_mu, rwkv_w0, rwkv_w_up, rwkv_a0, rwkv_a_up, rwkv_g_up,
                   rwkv_k_k, rwkv_k_a, rwkv_r_k, rwkv_gn_w, rwkv_gn_b, band_q_norm, band_k_norm, band_rel_bias,
                   gdn_conv_w, gdn_a_log, gdn_dt_bias, gdn_norm_w, w_branch, w_gate, b_gate, w_out,
                   w_router, router_bias, w_exp_gate, w_exp_up, w_exp_down):
    row = lambda z: z.reshape(1, -1).astype(F32)
    per_head = lambda z: jnp.repeat(z, HEAD_DIM).reshape(1, MIX)
    win = w_in[l].astype(BF16)
    c0 = A_COLS + B_COLS
    wab = jnp.pad(win[:, c0 + C_MAIN:], ((0, 0), (0, LANES - 2 * N_HEADS)))
    zeros = jnp.zeros((HEAD_DIM, MIX), F32)
    wwa = jnp.concatenate([jnp.concatenate([rwkv_w_up[l], zeros], axis=1),
                           jnp.concatenate([zeros, rwkv_a_up[l]], axis=1)], axis=0)
    head_of_lane = jnp.arange(MIX) // HEAD_DIM
    src = jnp.arange(LANES)[:, None]
    eab = jnp.concatenate([(src == head_of_lane[None, :]), (src == N_HEADS + head_of_lane[None, :])],
                          axis=1).astype(BF16)
    return dict(
        norm_mix_w=row(norm_mix_w[l]), norm_ffn_w=row(norm_ffn_w[l]),
        wa=win[:, :A_COLS], wb=win[:, A_COLS:c0], wc=win[:, c0:c0 + C_MAIN], wab=wab,
        q_norm=row(jnp.tile(band_q_norm[l], N_HEADS)), k_norm=row(jnp.tile(band_k_norm[l], N_HEADS)),
        mu=row(rwkv_mu[l]), w0=row(rwkv_w0[l]), a0=row(rwkv_a0[l]), wwa=wwa, g_up=rwkv_g_up[l],
        k_k=row(rwkv_k_k[l]), k_a=row(rwkv_k_a[l]), r_k=row(rwkv_r_k[l]), gn_w=row(rwkv_gn_w[l]),
        gn_b=row(rwkv_gn_b[l]), rel_bias=band_rel_bias[l],
        conv_w=gdn_conv_w[l], a_log=per_head(gdn_a_log[l]), dt_bias=per_head(gdn_dt_bias[l]),
        gdn_norm_w=row(jnp.tile(gdn_norm_w[l], N_HEADS)), eab=eab,
        w_gate=w_gate[l].astype(BF16), b_gate=row(b_gate[l]), w_branch=w_branch[l].astype(BF16),
        w_out=w_out[l].astype(BF16), w_router_t=jnp.transpose(w_router), router_bias=router_bias.reshape(-1, 1),
        wg=w_exp_gate[l].astype(BF16), wu=w_exp_up[l].astype(BF16),
        wd=w_exp_down[l].astype(BF16).reshape(N_GROUPS, -1, D_MODEL),
    )


def kernel(x_prompt, x_sample, c_prompt, c_sample, state_rwkv_shift, state_rwkv_wkv, cache_band_k, cache_band_v, state_gdn_conv, state_gdn_S, w_ada, b_ada, norm_mix_w, norm_ffn_w, w_in, rwkv_mu, rwkv_w0, rwkv_w_up, rwkv_a0, rwkv_a_up, rwkv_g_up, rwkv_k_k, rwkv_k_a, rwkv_r_k, rwkv_gn_w, rwkv_gn_b, band_q_norm, band_k_norm, band_rel_bias, gdn_conv_w, gdn_a_log, gdn_dt_bias, gdn_norm_w, w_branch, w_gate, b_gate, w_out, w_router, router_bias, w_exp_gate, w_exp_up, w_exp_down):
    depth = w_ada.shape[0]
    bsz = x_prompt.shape[0]
    mod_p = _adaln(c_prompt, w_ada, b_ada)
    mod_s = _adaln(c_sample, w_ada, b_ada)
    zero_state = (jnp.zeros((bsz, A_COLS), F32), jnp.zeros((bsz, N_HEADS, HEAD_DIM, HEAD_DIM), F32), None,
                  jnp.zeros((bsz, CONV_W - 1, 3 * MIX), F32), jnp.zeros((bsz, N_HEADS, HEAD_DIM, HEAD_DIM), F32))
    k_cache = cache_band_k.reshape(cache_band_k.shape[:3] + (MIX,))
    v_cache = cache_band_v.reshape(cache_band_v.shape[:3] + (MIX,))
    tm = TOKEN_TILE
    xp, xs = x_prompt, x_sample
    new_p, new_s = [], []
    for l in range(depth):
        wts = _prepare_layer(l, w_in, norm_mix_w, norm_ffn_w, rwkv_mu, rwkv_w0, rwkv_w_up, rwkv_a0, rwkv_a_up,
                             rwkv_g_up, rwkv_k_k, rwkv_k_a, rwkv_r_k, rwkv_gn_w, rwkv_gn_b, band_q_norm,
                             band_k_norm, band_rel_bias, gdn_conv_w, gdn_a_log, gdn_dt_bias, gdn_norm_w,
                             w_branch, w_gate, b_gate, w_out, w_router, router_bias, w_exp_gate, w_exp_up,
                             w_exp_down)
        xp, st_p = _layer(xp, mod_p[l], zero_state, wts, tm)
        xs, st_s = _layer(xs, mod_s[l], (state_rwkv_shift[l], state_rwkv_wkv[l], (k_cache, v_cache, l),
                                         state_gdn_conv[l], state_gdn_S[l]), wts, tm)
        new_p.append(st_p)
        new_s.append(st_s)
    p_out = [jnp.stack(z, axis=0) for z in zip(*new_p)]
    s_out = [jnp.stack(z, axis=0) for z in zip(*new_s)]
    return (xp, xs, *p_out, *s_out)
```

```python
import functools
import math

import jax
import jax.numpy as jnp
from jax import lax
from jax.experimental import pallas as pl
from jax.experimental.pallas import tpu as pltpu

F32 = jnp.float32
BF16 = jnp.bfloat16

D_MODEL = 1024
MIX = 512
HEAD_DIM = 64
N_HEADS = MIX // HEAD_DIM
LANES = 128
N_PAIRS = MIX // LANES
SUBLANES = 8
CHUNK = 64
TOKEN_TILE = 512
MIXER_ROWS = 256
BAND_QUERIES = 2 * CHUNK
A_COLS = 3 * MIX + 64 + 64 + 128
B_COLS = 3 * MIX
C_MAIN = 4 * MIX
CONV_W = 4
BAND = 8 * CHUNK
MAX_REL = 2 * CHUNK
N_EXPERTS = 16
N_GROUPS = 4
EXPERT_DIM = D_MODEL // 4
RMS_EPS = 1e-6
A_GN_EPS = 64e-5
A_DECAY_SCALE = math.exp(-0.5)
LOG2_E = math.log2(math.e)
SCORE_SCALE = HEAD_DIM ** -0.5 * LOG2_E
VMEM_LIMIT_BYTES = 56 * 1024 * 1024


def _params(*sem):
    return pltpu.CompilerParams(dimension_semantics=sem, vmem_limit_bytes=VMEM_LIMIT_BYTES)


def _dot(a, b, precision=None):
    return lax.dot_general(a, b, (((1,), (0,)), ((), ())), precision=precision, preferred_element_type=F32)


def _dot_nt(a, b, precision=None):
    return lax.dot_general(a, b, (((1,), (1,)), ((), ())), precision=precision, preferred_element_type=F32)


def _bdot(a, b):
    return _dot(a.astype(BF16), b.astype(BF16))


def _bdot_nt(a, b):
    return _dot_nt(a.astype(BF16), b.astype(BF16))


def _dot_split(a, b, nt=False):
    dot = _dot_nt if nt else _dot
    a_hi, a_lo = _bf16_terms(a, 2)
    b_hi, b_lo = _bf16_terms(b, 2)
    return dot(a_hi, b_hi) + (dot(a_hi, b_lo) + dot(a_lo, b_hi))


def _bmm(a, b):
    return lax.dot_general(a.astype(BF16), b.astype(BF16), (((2,), (1,)), ((0,), (0,))),
                           preferred_element_type=F32)


def _bmm_nt(a, b):
    return lax.dot_general(a.astype(BF16), b.astype(BF16), (((2,), (2,)), ((0,), (0,))),
                           preferred_element_type=F32)


def _sigmoid(x):
    return 1.0 / (1.0 + jnp.exp(-x))


def _silu(x):
    return x * _sigmoid(x)


def _softplus(x):
    return jnp.maximum(x, 0.0) + jnp.log1p(jnp.exp(-jnp.abs(x)))


def _iota(shape, dim):
    return lax.broadcasted_iota(jnp.int32, shape, dim)


def _bf16_terms(x, terms):
    parts = []
    for _ in range(terms):
        part = x.astype(BF16)
        parts.append(part)
        x = x - part.astype(F32)
    return parts


def _dot_01(x, m, terms):
    out = None
    for part in _bf16_terms(x, terms):
        term = _dot(part, m)
        out = term if out is None else out + term
    return out


def _dot_01_left(m, x, terms):
    out = None
    for part in _bf16_terms(x, terms):
        term = _dot(m, part)
        out = term if out is None else out + term
    return out


def _head_ones():
    return (_iota((LANES, LANES), 0) // HEAD_DIM == _iota((LANES, LANES), 1) // HEAD_DIM).astype(BF16)


def _head_sum(x, ones):
    parts = [_dot_01(x[:, g * LANES:(g + 1) * LANES], ones, 2) for g in range(x.shape[1] // LANES)]
    return parts[0] if len(parts) == 1 else jnp.concatenate(parts, axis=1)


def _stack(x):
    lane = _iota(x.shape, 1)
    return jnp.concatenate([jnp.where(lane < HEAD_DIM, x, 0.0), jnp.where(lane >= HEAD_DIM, x, 0.0)], axis=0)


def _tiles(x, c):
    return jnp.stack([x[u * c:(u + 1) * c, j * LANES:(j + 1) * LANES]
                      for u in range(x.shape[0] // c) for j in range(N_PAIRS)])


def _untile(xs):
    return jnp.concatenate([jnp.concatenate([xs[u * N_PAIRS + j] for j in range(N_PAIRS)], axis=1)
                            for u in range(xs.shape[0] // N_PAIRS)], axis=0)


def _block_diag(x):
    x = x.astype(BF16)
    first = _iota(x.shape[1:], 1) < x.shape[2] // 2
    zero = jnp.zeros_like(x)
    return jnp.concatenate([jnp.where(first, x, zero), jnp.where(first, zero, x)], axis=1)


def _keep_head_blocks(x):
    same = _iota(x.shape[1:], 0) // HEAD_DIM == _iota(x.shape[1:], 1) // HEAD_DIM
    return jnp.where(same, x, 0.0)


def _transpose_tiles(xs):
    return jnp.stack([jnp.transpose(xs[g]) for g in range(xs.shape[0])])


def _chunk_last(x, c):
    return jnp.concatenate([jnp.broadcast_to(x[(u + 1) * c - 1:(u + 1) * c], (c, x.shape[1]))
                            for u in range(x.shape[0] // c)], axis=0)


def _column_scale(row_values, c):
    tiles = []
    for u in range(row_values.shape[0] // c):
        last = row_values[(u + 1) * c - 1:(u + 1) * c]
        for j in range(N_PAIRS):
            tiles.append(jnp.transpose(jnp.broadcast_to(last[:, j * LANES:(j + 1) * LANES], (LANES, LANES))))
    return jnp.stack(tiles)


def _chunk_tril(rows, c):
    row, col = _iota((rows, rows), 0), _iota((rows, rows), 1)
    return ((col <= row) & (row // c == col // c)).astype(BF16)


def _tri_masks(c):
    row, col = _iota((c, 2 * c), 0), _iota((c, 2 * c), 1) % c
    return col < row, col <= row


def _unit_lower_inverse(a, c):
    row, col = _iota((c, 2 * c), 0), _iota((c, 2 * c), 1) % c
    eye = (row == col).astype(F32)
    same = lambda s: row // s == col // s
    a0 = jnp.where(same(SUBLANES), a, 0.0)
    a2 = _bmm(a0, _block_diag(a0))
    a4 = _bmm(a2, _block_diag(a2))
    x = _bmm(_bmm(eye - a0, _block_diag(eye + a2)), _block_diag(eye + a4))
    s = SUBLANES
    while s < c:
        off = jnp.where(same(2 * s) & jnp.logical_not(same(s)), a, 0.0)
        x = x - _bmm(_bmm(x, _block_diag(off)), _block_diag(x))
        s *= 2
    return x


def _adaln_kernel(c_ref, w_ref, b_ref, o_ref):
    o_ref[0] = _dot_split(_silu(c_ref[...]), w_ref[0]) + b_ref[0]


def _adaln(c, w_ada, b_ada):
    depth, bsz, tn = w_ada.shape[0], c.shape[0], 768
    return pl.pallas_call(
        _adaln_kernel,
        grid=(depth, 6 * D_MODEL // tn),
        in_specs=[pl.BlockSpec((bsz, D_MODEL), lambda l, j: (0, 0)),
                  pl.BlockSpec((1, D_MODEL, tn), lambda l, j: (l, 0, j)),
                  pl.BlockSpec((1, 1, tn), lambda l, j: (l, 0, j))],
        out_specs=pl.BlockSpec((1, bsz, tn), lambda l, j: (l, 0, j)),
        out_shape=jax.ShapeDtypeStruct((depth, bsz, 6 * D_MODEL), F32),
        compiler_params=_params("arbitrary", "arbitrary"),
        name="adaln",
    )(c, w_ada, b_ada.reshape(depth, 1, 6 * D_MODEL))


def _norm_mod(x, norm_w, scale, shift):
    y = x * lax.rsqrt(jnp.mean(x * x, axis=-1, keepdims=True) + RMS_EPS)
    return y * norm_w * (1.0 + scale) + shift


def _inproj_kernel(x_ref, sh_ref, sc_ref, nw_ref, wa_ref, wb_ref, wc_ref, wab_ref, qn_ref, kn_ref,
                   pa_ref, q_ref, k_ref, v_ref, pc_ref, pab_ref):
    h = _norm_mod(x_ref[...], nw_ref[...], sc_ref[0], sh_ref[0]).astype(BF16)
    pa_ref[...] = _dot(h, wa_ref[...])
    pb = _dot(h, wb_ref[...])
    ones = _head_ones()

    def head_rms(y, w):
        return y * lax.rsqrt(_head_sum(y * y, ones) * (1.0 / HEAD_DIM) + RMS_EPS) * w

    q_ref[...] = (head_rms(pb[:, :MIX], qn_ref[...]) * SCORE_SCALE).astype(BF16)
    k_ref[...] = head_rms(pb[:, MIX:2 * MIX], kn_ref[...])
    v_ref[...] = pb[:, 2 * MIX:]
    pc_ref[...] = _dot(h, wc_ref[...])
    pab_ref[...] = _dot(h, wab_ref[...])


def _mod_spec(mod, tm, tiles_per_group):
    rows = mod.shape[1]
    return pl.BlockSpec((1, rows, D_MODEL), lambda i: (i // tiles_per_group, 0, 0))


def _const_spec(shape):
    zeros = (0,) * len(shape)
    return pl.BlockSpec(shape, lambda *_: zeros, pipeline_mode=pl.Buffered(1))


def _inproj(x, shift, scale, norm_w, wa, wb, wc, wab, q_norm, k_norm, tm, tiles_per_group):
    n = x.shape[0]
    row = lambda w: pl.BlockSpec((tm, w), lambda i: (i, 0))
    widths = (A_COLS, MIX, MIX, MIX, C_MAIN, LANES)
    return pl.pallas_call(
        _inproj_kernel,
        grid=(n // tm,),
        in_specs=[row(D_MODEL), _mod_spec(shift, tm, tiles_per_group), _mod_spec(scale, tm, tiles_per_group),
                  _const_spec((1, D_MODEL)), _const_spec(wa.shape), _const_spec(wb.shape), _const_spec(wc.shape),
                  _const_spec(wab.shape), _const_spec((1, MIX)), _const_spec((1, MIX))],
        out_specs=[row(w) for w in widths],
        out_shape=[jax.ShapeDtypeStruct((n, w), BF16 if i == 1 else F32) for i, w in enumerate(widths)],
        compiler_params=_params("arbitrary"),
        name="inproj",
    )(x, shift, scale, norm_w, wa, wb, wc, wab, q_norm, k_norm)


def _rwkv_kernel(p_ref, shift0_ref, st0_ref, mu_ref, w0_ref, a0_ref, wwa_ref, gup_ref, kk_ref, ka_ref, rk_ref,
                 gnw_ref, gnb_ref, o_ref, st_out_ref, ext_scr, st_scr, *, c):
    bb, tt, _ = p_ref.shape
    rows, nch, n = bb * tt, tt // c, 2 * c
    assert bb == 1 or nch == 1
    ci = pl.program_id(1)

    @pl.when(ci == 0)
    def _():
        ext_scr[:, SUBLANES - 1:SUBLANES, :] = shift0_ref[...]
        st_scr[...] = st0_ref[...].reshape(bb * N_PAIRS, LANES, LANES)

    ext_scr[:, SUBLANES:SUBLANES + tt, :] = p_ref[...]
    p = p_ref[...].reshape(rows, A_COLS)
    p_prev = ext_scr[:, SUBLANES - 1:SUBLANES - 1 + tt, :].reshape(rows, A_COLS)
    ext_scr[:, 0:SUBLANES, :] = ext_scr[:, tt:tt + SUBLANES, :]
    xs = p + (p_prev - p) * mu_ref[...]
    r, k, v = xs[:, :MIX], xs[:, MIX:2 * MIX], xs[:, 2 * MIX:3 * MIX]
    lora_in = xs[:, 3 * MIX:3 * MIX + LANES]
    gd = xs[:, 3 * MIX + LANES:]
    lane = _iota((1, LANES), 1)
    lora = _bdot(jnp.where(lane < HEAD_DIM, jnp.tanh(lora_in), lora_in), wwa_ref[...])
    log_w = -A_DECAY_SCALE * _sigmoid(w0_ref[...] + lora[:, :MIX])
    a = _sigmoid(a0_ref[...] + lora[:, MIX:])
    g = _bdot(_sigmoid(gd), gup_ref[...])
    ones = _head_ones()
    kk_raw = k * kk_ref[...]
    kk = kk_raw * lax.rsqrt(_head_sum(kk_raw * kk_raw, ones) + RMS_EPS)
    k = k * (1.0 + (a - 1.0) * ka_ref[...])
    b = kk * a
    cum = _dot_01_left(_chunk_tril(rows, c), log_w, 3)
    cum_last = _chunk_last(cum, c)
    e_cum, e_neg = jnp.exp(cum), jnp.exp(-cum)
    e_prev, e_tail = jnp.exp(cum - log_w), jnp.exp(cum_last - cum)
    strict, incl = _tri_masks(c)

    kk_g, r_g, v_g = _tiles(kk * e_prev, c), _tiles(r * e_cum, c), _tiles(v, c)
    amat = _bmm_nt(jnp.concatenate([kk_g, r_g], axis=1),
                   jnp.concatenate([_block_diag(_tiles(k * e_neg, c)), _block_diag(_tiles(b * e_neg, c))],
                                   axis=1))
    a_kk = jnp.where(strict, amat[:, :c, :n], 0.0)
    a_bk = jnp.where(strict, amat[:, :c, n:], 0.0)
    a_rk = jnp.where(incl, amat[:, c:, :n], 0.0)
    a_rb = jnp.where(incl, amat[:, c:, n:], 0.0)
    t_inv = _unit_lower_inverse(a_bk, c)
    av = _bmm(jnp.concatenate([a_kk, a_rk], axis=1), _block_diag(v_g))
    wu = _bmm(t_inv, jnp.concatenate([_block_diag(kk_g), _block_diag(av[:, :c])], axis=2))
    kd_v = _keep_head_blocks(_bmm(_transpose_tiles(_tiles(k * e_tail, c)), v_g))
    bd_t = _transpose_tiles(_tiles(b * e_tail, c))
    decay = _column_scale(e_cum, c)

    st = st_scr[...]
    per_step = bb * N_PAIRS
    o_tiles = []
    for ch in range(nch):
        sel = slice(ch * per_step, (ch + 1) * per_step)
        zo = _bmm(jnp.concatenate([wu[sel, :, :LANES], r_g[sel]], axis=1), st)
        z = wu[sel, :, LANES:] + zo[:, :c]
        o_tiles.append(zo[:, c:] + av[sel, c:] - _bmm(a_rb[sel], _block_diag(z)))
        st = decay[sel] * st + kd_v[sel] - _keep_head_blocks(_bmm(bd_t[sel], z))
    st_scr[...] = st
    o = _untile(o_tiles[0] if nch == 1 else jnp.concatenate(o_tiles, axis=0))

    mean = _head_sum(o, ones) * (1.0 / HEAD_DIM)
    cen = o - mean
    var = _head_sum(cen * cen, ones) * (1.0 / HEAD_DIM)
    o = cen * lax.rsqrt(var + A_GN_EPS) * gnw_ref[...] + gnb_ref[...]
    bonus = _head_sum(r * k * rk_ref[...], ones) * v
    o_ref[...] = ((o + bonus) * g).astype(BF16).reshape(bb, tt, MIX)
    st_out_ref[...] = st.reshape(bb, N_PAIRS, LANES, LANES)


def _mixer_tiling(bsz, t, c):
    if t > c:
        return 1, min(t, MIXER_ROWS)
    return min(bsz, MIXER_ROWS // t), t


def _rwkv(pa, shift0, st0, mu, w0, a0, wwa, gup, k_k, k_a, r_k, gn_w, gn_b, c):
    bsz, t, _ = pa.shape
    bb, tt = _mixer_tiling(bsz, t, c)
    vec = lambda w: _const_spec((1, w))
    state = pl.BlockSpec((bb, N_PAIRS, LANES, LANES), lambda b, i: (b, 0, 0, 0))
    return pl.pallas_call(
        functools.partial(_rwkv_kernel, c=c),
        grid=(bsz // bb, t // tt),
        in_specs=[pl.BlockSpec((bb, tt, A_COLS), lambda b, i: (b, i, 0)),
                  pl.BlockSpec((bb, 1, A_COLS), lambda b, i: (b, 0, 0)), state,
                  vec(A_COLS), vec(MIX), vec(MIX), _const_spec(wwa.shape), _const_spec(gup.shape),
                  vec(MIX), vec(MIX), vec(MIX), vec(MIX), vec(MIX)],
        out_specs=[pl.BlockSpec((bb, tt, MIX), lambda b, i: (b, i, 0)), state],
        out_shape=[jax.ShapeDtypeStruct((bsz, t, MIX), BF16),
                   jax.ShapeDtypeStruct((bsz, N_PAIRS, LANES, LANES), F32)],
        scratch_shapes=[pltpu.VMEM((bb, tt + SUBLANES, A_COLS), F32),
                        pltpu.VMEM((bb * N_PAIRS, LANES, LANES), F32)],
        compiler_params=_params("arbitrary", "arbitrary"),
        name="rwkv",
    )(pa, shift0, st0, mu, w0, a0, wwa, gup, k_k, k_a, r_k, gn_w, gn_b)


def _gdn_kernel(pc_ref, pab_ref, conv0_ref, st0_ref, convw_ref, alog_ref, dtb_ref, nw_ref, eab_ref,
                o_ref, st_out_ref, ext_scr, st_scr, *, c):
    bb, tt, _ = pc_ref.shape
    rows, nch, n = bb * tt, tt // c, 2 * c
    assert bb == 1 or nch == 1
    w3 = 3 * MIX
    ci = pl.program_id(1)

    @pl.when(ci == 0)
    def _():
        ext_scr[:, 0:SUBLANES, :] = conv0_ref[...]
        st_scr[...] = st0_ref[...].reshape(bb * N_PAIRS, LANES, LANES)

    ext_scr[:, SUBLANES:SUBLANES + tt, :] = pc_ref[:, :, :w3]
    conv = ext_scr[:, SUBLANES:SUBLANES + tt, :] * convw_ref[CONV_W - 1:CONV_W, :]
    for s in range(1, CONV_W):
        conv = conv + ext_scr[:, SUBLANES - s:SUBLANES - s + tt, :] * convw_ref[CONV_W - 1 - s:CONV_W - s, :]
    ext_scr[:, 0:SUBLANES, :] = ext_scr[:, tt:tt + SUBLANES, :]
    qkv = _silu(conv.reshape(rows, w3))
    ones = _head_ones()
    l2n = lambda y: y * lax.rsqrt(_head_sum(y * y, ones) + RMS_EPS)
    q = l2n(qkv[:, :MIX]) * (HEAD_DIM ** -0.5)
    k = l2n(qkv[:, MIX:2 * MIX])
    v = qkv[:, 2 * MIX:]
    zgate = pc_ref[:, :, w3:].reshape(rows, MIX)
    ab = _dot_01(pab_ref[...].reshape(rows, LANES), eab_ref[...], 3)
    log_alpha = -jnp.exp(alog_ref[...]) * _softplus(ab[:, :MIX] + dtb_ref[...])
    beta = _sigmoid(ab[:, MIX:])
    gcum = _dot_01_left(_chunk_tril(rows, c), log_alpha, 3)
    eg = jnp.exp(gcum)
    bk = beta * k
    strict, incl = _tri_masks(c)

    g_t = _tiles(gcum, c)
    g_col = jnp.concatenate([g_t[:, :, :c], g_t[:, :, HEAD_DIM:HEAD_DIM + c]], axis=2)
    g_time = _transpose_tiles(g_t)
    g_row = jnp.concatenate([g_time[:, 0:1, :], g_time[:, HEAD_DIM:HEAD_DIM + 1, :]], axis=2)
    diff = g_col - g_row
    dec_s = jnp.where(strict, jnp.exp(jnp.where(strict, diff, 0.0)), 0.0)
    dec_i = jnp.where(incl, jnp.exp(jnp.where(incl, diff, 0.0)), 0.0)
    qg_g = _tiles(q * eg, c)
    kmat = _bmm_nt(jnp.concatenate([_tiles(bk, c), _tiles(q, c)], axis=1), _block_diag(_tiles(k, c)))
    attn = kmat[:, c:] * dec_i
    t_inv = _unit_lower_inverse(kmat[:, :c] * dec_s, c)
    sol = _bmm(t_inv, jnp.concatenate([_block_diag(_tiles(beta * v, c)), _block_diag(_tiles(bk * eg, c))],
                                      axis=2))
    kd_t = _transpose_tiles(_tiles(k * jnp.exp(_chunk_last(gcum, c) - gcum), c))
    decay = _column_scale(eg, c)

    st = st_scr[...]
    per_step = bb * N_PAIRS
    o_tiles = []
    for ch in range(nch):
        sel = slice(ch * per_step, (ch + 1) * per_step)
        wq = _bmm(jnp.concatenate([sol[sel, :, LANES:], qg_g[sel]], axis=1), st)
        delta = sol[sel, :, :LANES] - wq[:, :c]
        o_tiles.append(wq[:, c:] + _bmm(attn[sel], _block_diag(delta)))
        st = decay[sel] * st + _keep_head_blocks(_bmm(kd_t[sel], delta))
    st_scr[...] = st
    o = _untile(o_tiles[0] if nch == 1 else jnp.concatenate(o_tiles, axis=0))

    o = o * lax.rsqrt(_head_sum(o * o, ones) * (1.0 / HEAD_DIM) + RMS_EPS) * nw_ref[...]
    o_ref[...] = (o * _silu(zgate)).astype(BF16).reshape(bb, tt, MIX)
    st_out_ref[...] = st.reshape(bb, N_PAIRS, LANES, LANES)


def _gdn(pc, pab, conv0, st0, conv_w, a_log, dt_bias, norm_w, eab, c):
    bsz, t, _ = pc.shape
    bb, tt = _mixer_tiling(bsz, t, c)
    vec = lambda w: _const_spec((1, w))
    state = pl.BlockSpec((bb, N_PAIRS, LANES, LANES), lambda b, i: (b, 0, 0, 0))
    return pl.pallas_call(
        functools.partial(_gdn_kernel, c=c),
        grid=(bsz // bb, t // tt),
        in_specs=[pl.BlockSpec((bb, tt, C_MAIN), lambda b, i: (b, i, 0)),
                  pl.BlockSpec((bb, tt, LANES), lambda b, i: (b, i, 0)),
                  pl.BlockSpec((bb, SUBLANES, 3 * MIX), lambda b, i: (b, 0, 0)), state,
                  _const_spec((CONV_W, 3 * MIX)), vec(MIX), vec(MIX), vec(MIX), _const_spec(eab.shape)],
        out_specs=[pl.BlockSpec((bb, tt, MIX), lambda b, i: (b, i, 0)), state],
        out_shape=[jax.ShapeDtypeStruct((bsz, t, MIX), BF16),
                   jax.ShapeDtypeStruct((bsz, N_PAIRS, LANES, LANES), F32)],
        scratch_shapes=[pltpu.VMEM((bb, tt + SUBLANES, 3 * MIX), F32),
                        pltpu.VMEM((bb * N_PAIRS, LANES, LANES), F32)],
        compiler_params=_params("arbitrary", "arbitrary"),
        name="gdn",
    )(pc, pab, conv0, st0, conv_w, a_log, dt_bias, norm_w, eab)


def _band_core(q, k_g, v_g, bias):
    bb, cq, _ = q.shape
    wn = k_g.shape[1]
    q_g = jnp.stack([_stack(q[b, :, j * LANES:(j + 1) * LANES]) for b in range(bb) for j in range(N_PAIRS)])
    s = _bmm_nt(q_g, k_g)
    s = (s.reshape(bb, N_PAIRS, 2 * cq, wn) + bias).reshape(bb * N_PAIRS, 2 * cq, wn)
    e = jnp.exp2(s - jnp.max(s, axis=-1, keepdims=True))
    pv = _bmm(e, v_g) / jnp.sum(e, axis=-1, keepdims=True)
    lane = _iota((cq, LANES), 1)
    return jnp.stack([
        jnp.concatenate([jnp.where(lane < HEAD_DIM, pv[b * N_PAIRS + j, :cq], pv[b * N_PAIRS + j, cq:])
                         for j in range(N_PAIRS)], axis=1) for b in range(bb)]).astype(BF16)


def _band_prompt_kernel(q_ref, k_ref, v_ref, bias_ref, o_ref):
    cq = q_ref.shape[1]
    wn = BAND + cq
    first = pl.program_id(1) * cq - BAND
    start = pl.multiple_of(jnp.maximum(first, 0), LANES)
    shift = pl.multiple_of(start - first, LANES)
    window = lambda ref: jnp.stack([ref[0, pl.ds(start, wn), j * LANES:(j + 1) * LANES]
                                    for j in range(N_PAIRS)]).astype(BF16)
    o_ref[...] = _band_core(q_ref[...], window(k_ref), window(v_ref), bias_ref[:, :, pl.ds(shift, wn)])


def _band_prompt(q, k, v, table):
    bsz, t, _ = q.shape
    cq = BAND_QUERIES
    span = 2 * BAND + cq
    q_chunk = jnp.arange(cq)[:, None] // CHUNK
    k_chunk = jnp.arange(span)[None, :] // CHUNK
    in_band = (k_chunk >= q_chunk) & (k_chunk <= q_chunk + BAND // CHUNK)
    bias = jnp.where(in_band, _rel_bias(table, cq, span) * LOG2_E, -1e30).reshape(N_PAIRS, 2 * cq, span)
    full = pl.BlockSpec((1, t, MIX), lambda b, i: (b, 0, 0))
    return pl.pallas_call(
        _band_prompt_kernel,
        grid=(bsz, t // cq),
        in_specs=[pl.BlockSpec((1, cq, MIX), lambda b, i: (b, i, 0)), full, full, _const_spec(bias.shape)],
        out_specs=pl.BlockSpec((1, cq, MIX), lambda b, i: (b, i, 0)),
        out_shape=jax.ShapeDtypeStruct((bsz, t, MIX), BF16),
        compiler_params=_params("arbitrary", "arbitrary"),
        name="band_prompt",
    )(q, k, v, bias)


def _band_sample_kernel(q_ref, kp_ref, vp_ref, kn_ref, vn_ref, bias_ref, o_ref):
    bb = q_ref.shape[0]
    keys = lambda past, new: jnp.stack([
        jnp.concatenate([past[b, :, j * LANES:(j + 1) * LANES], new[b, :, j * LANES:(j + 1) * LANES]], axis=0)
        for b in range(bb) for j in range(N_PAIRS)]).astype(BF16)
    o_ref[...] = _band_core(q_ref[...], keys(kp_ref, kn_ref), keys(vp_ref, vn_ref), bias_ref[...])


def _band_sample(q, k, v, caches, table):
    k_past, v_past, layer = caches
    bsz, t, _ = q.shape
    past = k_past.shape[2]
    bb = min(bsz, MIXER_ROWS // t)
    bias = (_rel_bias(table, t, past + t) * LOG2_E).reshape(N_PAIRS, 2 * t, past + t)
    new = pl.BlockSpec((bb, t, MIX), lambda b: (b, 0, 0))
    old = pl.BlockSpec((None, bb, past, MIX), lambda b: (layer, b, 0, 0))
    return pl.pallas_call(
        _band_sample_kernel,
        grid=(bsz // bb,),
        in_specs=[new, old, old, new, new, _const_spec(bias.shape)],
        out_specs=new,
        out_shape=jax.ShapeDtypeStruct((bsz, t, MIX), BF16),
        compiler_params=_params("arbitrary"),
        name="band_sample",
    )(q, k_past, v_past, k, v, bias)


def _merge_kernel(x_ref, oa_ref, ob_ref, oc_ref, sh_ref, sc_ref, gm_ref, sh2_ref, sc2_ref, nw_ref, nw2_ref,
                  wgate_ref, bgate_ref, wbr_ref, wout_ref, wrt_ref, rb_ref,
                  x1_ref, h2_ref, comb_ref):
    x = x_ref[...]
    h = _norm_mod(x, nw_ref[...], sc_ref[0], sh_ref[0]).astype(BF16)
    gates = _sigmoid(_dot(h, wgate_ref[...]) + bgate_ref[...])
    mixed = None
    for i, o_ref in enumerate((oa_ref, ob_ref, oc_ref)):
        term = gates[:, i * D_MODEL:(i + 1) * D_MODEL] * _bdot(o_ref[...], wbr_ref[i])
        mixed = term if mixed is None else mixed + term
    x1 = x + gm_ref[0] * _bdot(mixed, wout_ref[...])
    x1_ref[...] = x1
    h2 = _norm_mod(x1, nw2_ref[...], sc2_ref[0], sh2_ref[0])
    h2_ref[...] = h2.astype(BF16)

    scores = _sigmoid(_dot_split(wrt_ref[...], h2, nt=True))
    sel = scores + rb_ref[...]
    tm = scores.shape[1]
    per = N_EXPERTS // N_GROUPS
    best_val, best = None, None
    for g in range(N_GROUPS):
        rows = [sel[g * per + i:g * per + i + 1, :] for i in range(per)]
        top2 = None
        for i in range(per):
            for i2 in range(i + 1, per):
                pair = rows[i] + rows[i2]
                top2 = pair if top2 is None else jnp.maximum(top2, pair)
        if g == 0:
            best_val, best = top2, jnp.zeros((1, tm), jnp.int32)
        else:
            better = top2 > best_val
            best = jnp.where(better, g, best)
            best_val = jnp.where(better, top2, best_val)
    eidx = _iota((N_EXPERTS, tm), 0)
    cand = jnp.where(eidx // per == best, sel, -jnp.inf)
    m1 = jnp.max(cand, axis=0, keepdims=True)
    i1 = jnp.min(jnp.where(cand == m1, eidx, N_EXPERTS), axis=0, keepdims=True)
    cand2 = jnp.where(eidx == i1, -jnp.inf, cand)
    m2 = jnp.max(cand2, axis=0, keepdims=True)
    i2 = jnp.min(jnp.where(cand2 == m2, eidx, N_EXPERTS), axis=0, keepdims=True)
    w1 = jnp.sum(jnp.where(eidx == i1, scores, 0.0), axis=0, keepdims=True)
    w2 = jnp.sum(jnp.where(eidx == i2, scores, 0.0), axis=0, keepdims=True)
    den = w1 + w2
    comb_ref[...] = jnp.where(eidx == i1, w1 / den, 0.0) + jnp.where(eidx == i2, w2 / den, 0.0)


def _merge(x, oa, ob, oc, mods, norm_w, norm2_w, w_gate, b_gate, w_branch, w_out, w_router_t, router_bias,
           tm, tiles_per_group):
    n = x.shape[0]
    row = lambda w: pl.BlockSpec((tm, w), lambda i: (i, 0))
    mod_specs = [_mod_spec(m, tm, tiles_per_group) for m in mods]
    return pl.pallas_call(
        _merge_kernel,
        grid=(n // tm,),
        in_specs=[row(D_MODEL), row(MIX), row(MIX), row(MIX)] + mod_specs
                 + [_const_spec((1, D_MODEL)), _const_spec((1, D_MODEL)), _const_spec(w_gate.shape),
                    _const_spec(b_gate.shape), _const_spec(w_branch.shape), _const_spec(w_out.shape),
                    _const_spec(w_router_t.shape), _const_spec(router_bias.shape)],
        out_specs=[row(D_MODEL), row(D_MODEL), pl.BlockSpec((N_EXPERTS, tm), lambda i: (0, i))],
        out_shape=[jax.ShapeDtypeStruct((n, D_MODEL), F32), jax.ShapeDtypeStruct((n, D_MODEL), BF16),
                   jax.ShapeDtypeStruct((N_EXPERTS, n), F32)],
        compiler_params=_params("arbitrary"),
        name="merge",
    )(x, oa, ob, oc, *mods, norm_w, norm2_w, w_gate, b_gate, w_branch, w_out, w_router_t, router_bias)


def _moe_kernel(x1_ref, h2_ref, comb_ref, gf_ref, wg_ref, wu_ref, wd_ref, o_ref):
    h2 = h2_ref[...]
    comb = jnp.transpose(comb_ref[...])
    per = N_EXPERTS // N_GROUPS
    acc = None
    for g in range(N_GROUPS):
        hid = [(_silu(_dot(h2, wg_ref[e])) * _dot(h2, wu_ref[e]) * comb[:, e:e + 1]).astype(BF16)
               for e in range(g * per, (g + 1) * per)]
        term = _dot(jnp.concatenate(hid, axis=1), wd_ref[g])
        acc = term if acc is None else acc + term
    o_ref[...] = x1_ref[...] + gf_ref[0] * acc


def _moe(x1, h2, comb, g_ffn, wg, wu, wd, tm, tiles_per_group):
    n = x1.shape[0]
    row = lambda w: pl.BlockSpec((tm, w), lambda i: (i, 0))
    return pl.pallas_call(
        _moe_kernel,
        grid=(n // tm,),
        in_specs=[row(D_MODEL), row(D_MODEL), pl.BlockSpec((N_EXPERTS, tm), lambda i: (0, i)),
                  _mod_spec(g_ffn, tm, tiles_per_group),
                  _const_spec(wg.shape), _const_spec(wu.shape), _const_spec(wd.shape)],
        out_specs=row(D_MODEL),
        out_shape=jax.ShapeDtypeStruct((n, D_MODEL), F32),
        compiler_params=_params("arbitrary"),
        name="moe",
    )(x1, h2, comb, g_ffn, wg, wu, wd)


def _pair_state(s):
    bsz = s.shape[0]
    s = s.reshape(bsz, N_PAIRS, 2, HEAD_DIM, HEAD_DIM)
    z = jnp.zeros_like(s[:, :, 0])
    top = jnp.concatenate([s[:, :, 0], z], axis=-1)
    bot = jnp.concatenate([z, s[:, :, 1]], axis=-1)
    return jnp.concatenate([top, bot], axis=-2)


def _unpair_state(s):
    bsz = s.shape[0]
    return jnp.stack([s[:, :, :HEAD_DIM, :HEAD_DIM], s[:, :, HEAD_DIM:, HEAD_DIM:]], axis=2).reshape(
        bsz, N_HEADS, HEAD_DIM, HEAD_DIM)


def _rel_bias(table, cq, wn):
    span = cq - 1 + wn
    offs = jnp.arange(-(cq - 1), wn + 1)
    diag = table[:, jnp.clip(BAND - offs, -MAX_REL, MAX_REL) + MAX_REL].astype(F32)
    skew = jnp.tile(diag, (1, cq))[:, :cq * span].reshape(-1, cq, span)
    return skew[:, :, cq - 1:]


def _layer(x, mod, state, wts, tm):
    bsz, t, _ = x.shape
    n = bsz * t
    c = min(CHUNK, t)
    shift0, wkv0, caches, conv0, s0 = state
    if t % tm == 0:
        tiles_per_group = t // tm
        mods = [m.reshape(bsz, 1, D_MODEL) for m in jnp.split(mod, 6, axis=-1)]
    else:
        tiles_per_group = 1
        mods = [jnp.repeat(m, t, axis=0).reshape(n // tm, tm, D_MODEL) for m in jnp.split(mod, 6, axis=-1)]
    sh_mix, sc_mix, g_mix, sh_ffn, sc_ffn, g_ffn = mods
    xf = x.reshape(n, D_MODEL)

    pa, q, k, v, pc, pab = _inproj(xf, sh_mix, sc_mix, wts["norm_mix_w"], wts["wa"], wts["wb"], wts["wc"],
                                   wts["wab"], wts["q_norm"], wts["k_norm"], tm, tiles_per_group)
    pa = pa.reshape(bsz, t, A_COLS)
    q, k, v = (z.reshape(bsz, t, MIX) for z in (q, k, v))
    pc = pc.reshape(bsz, t, C_MAIN)

    oa, wkv = _rwkv(pa, shift0.reshape(bsz, 1, A_COLS), _pair_state(jnp.swapaxes(wkv0, -1, -2)),
                    wts["mu"], wts["w0"], wts["a0"], wts["wwa"], wts["g_up"], wts["k_k"], wts["k_a"], wts["r_k"],
                    wts["gn_w"], wts["gn_b"], c)
    new_wkv = jnp.swapaxes(_unpair_state(wkv), -1, -2)

    if caches is None:
        ob = _band_prompt(q, k, v, wts["rel_bias"])
        new_k, new_v = k[:, -BAND:], v[:, -BAND:]
    else:
        ob = _band_sample(q, k, v, caches, wts["rel_bias"])
        new_k, new_v = k, v

    conv_pad = jnp.pad(conv0, ((0, 0), (SUBLANES - (CONV_W - 1), 0), (0, 0)))
    oc, s_new = _gdn(pc, pab.reshape(bsz, t, LANES), conv_pad, _pair_state(s0), wts["conv_w"], wts["a_log"],
                     wts["dt_bias"], wts["gdn_norm_w"], wts["eab"], c)
    tail = min(t, CONV_W - 1)
    new_conv = jnp.concatenate([conv0, pc[:, t - tail:, :3 * MIX]], axis=1)[:, -(CONV_W - 1):]

    x1, h2, comb = _merge(xf, oa.reshape(n, MIX), ob.reshape(n, MIX), oc.reshape(n, MIX),
                          [sh_mix, sc_mix, g_mix, sh_ffn, sc_ffn], wts["norm_mix_w"], wts["norm_ffn_w"],
                          wts["w_gate"], wts["b_gate"], wts["w_branch"], wts["w_out"], wts["w_router_t"],
                          wts["router_bias"], tm, tiles_per_group)
    x2 = _moe(x1, h2, comb, g_ffn, wts["wg"], wts["wu"], wts["wd"], tm, tiles_per_group)
    heads = lambda z: z.reshape(bsz, -1, N_HEADS, HEAD_DIM)
    return x2.reshape(bsz, t, D_MODEL), (pa[:, -1], new_wkv, heads(new_k), heads(new_v), new_conv,
                                         _unpair_state(s_new))


def _prepare_layer(l, w_in, norm_mix_w, norm_ffn_w, rwkv_mu, rwkv_w0, rwkv_w_up, rwkv_a0, rwkv_a_up, rwkv_g_up,
                   rwkv_k_k, rwkv_k_a, rwkv_r_k, rwkv_gn_w, rwkv_gn_b, band_q_norm, band_k_norm, band_rel_bias,
                   gdn_conv_w, gdn_a_log, gdn_dt_bias, gdn_norm_w, w_branch, w_gate, b_gate, w_out,
                   w_router, router_bias, w_exp_gate, w_exp_up, w_exp_down):
    row = lambda z: z.reshape(1, -1).astype(F32)
    per_head = lambda z: jnp.repeat(z, HEAD_DIM).reshape(1, MIX)
    win = w_in[l].astype(BF16)
    c0 = A_COLS + B_COLS
    wab = jnp.pad(win[:, c0 + C_MAIN:], ((0, 0), (0, LANES - 2 * N_HEADS)))
    zeros = jnp.zeros((HEAD_DIM, MIX), F32)
    wwa = jnp.concatenate([jnp.concatenate([rwkv_w_up[l], zeros], axis=1),
                           jnp.concatenate([zeros, rwkv_a_up[l]], axis=1)], axis=0)
    head_of_lane = jnp.arange(MIX) // HEAD_DIM
    src = jnp.arange(LANES)[:, None]
    eab = jnp.concatenate([(src == head_of_lane[None, :]), (src == N_HEADS + head_of_lane[None, :])],
                          axis=1).astype(BF16)
    return dict(
        norm_mix_w=row(norm_mix_w[l]), norm_ffn_w=row(norm_ffn_w[l]),
        wa=win[:, :A_COLS], wb=win[:, A_COLS:c0], wc=win[:, c0:c0 + C_MAIN], wab=wab,
        q_norm=row(jnp.tile(band_q_norm[l], N_HEADS)), k_norm=row(jnp.tile(band_k_norm[l], N_HEADS)),
        mu=row(rwkv_mu[l]), w0=row(rwkv_w0[l]), a0=row(rwkv_a0[l]), wwa=wwa, g_up=rwkv_g_up[l],
        k_k=row(rwkv_k_k[l]), k_a=row(rwkv_k_a[l]), r_k=row(rwkv_r_k[l]), gn_w=row(rwkv_gn_w[l]),
        gn_b=row(rwkv_gn_b[l]), rel_bias=band_rel_bias[l],
        conv_w=gdn_conv_w[l], a_log=per_head(gdn_a_log[l]), dt_bias=per_head(gdn_dt_bias[l]),
        gdn_norm_w=row(jnp.tile(gdn_norm_w[l], N_HEADS)), eab=eab,
        w_gate=w_gate[l].astype(BF16), b_gate=row(b_gate[l]), w_branch=w_branch[l].astype(BF16),
        w_out=w_out[l].astype(BF16), w_router_t=jnp.transpose(w_router), router_bias=router_bias.reshape(-1, 1),
        wg=w_exp_gate[l].astype(BF16), wu=w_exp_up[l].astype(BF16),
        wd=w_exp_down[l].astype(BF16).reshape(N_GROUPS, -1, D_MODEL),
    )


def kernel(x_prompt, x_sample, c_prompt, c_sample, state_rwkv_shift, state_rwkv_wkv, cache_band_k, cache_band_v, state_gdn_conv, state_gdn_S, w_ada, b_ada, norm_mix_w, norm_ffn_w, w_in, rwkv_mu, rwkv_w0, rwkv_w_up, rwkv_a0, rwkv_a_up, rwkv_g_up, rwkv_k_k, rwkv_k_a, rwkv_r_k, rwkv_gn_w, rwkv_gn_b, band_q_norm, band_k_norm, band_rel_bias, gdn_conv_w, gdn_a_log, gdn_dt_bias, gdn_norm_w, w_branch, w_gate, b_gate, w_out, w_router, router_bias, w_exp_gate, w_exp_up, w_exp_down):
    depth = w_ada.shape[0]
    bsz = x_prompt.shape[0]
    mod = _adaln(jnp.concatenate([c_prompt, c_sample], axis=0), w_ada, b_ada)
    mod_p, mod_s = mod[:, :bsz], mod[:, bsz:]
    zero_state = (jnp.zeros((bsz, A_COLS), F32), jnp.zeros((bsz, N_HEADS, HEAD_DIM, HEAD_DIM), F32), None,
                  jnp.zeros((bsz, CONV_W - 1, 3 * MIX), F32), jnp.zeros((bsz, N_HEADS, HEAD_DIM, HEAD_DIM), F32))
    k_cache = cache_band_k.reshape(cache_band_k.shape[:3] + (MIX,))
    v_cache = cache_band_v.reshape(cache_band_v.shape[:3] + (MIX,))
    tm = TOKEN_TILE
    xp, xs = x_prompt, x_sample
    new_p, new_s = [], []
    for l in range(depth):
        wts = _prepare_layer(l, w_in, norm_mix_w, norm_ffn_w, rwkv_mu, rwkv_w0, rwkv_w_up, rwkv_a0, rwkv_a_up,
                             rwkv_g_up, rwkv_k_k, rwkv_k_a, rwkv_r_k, rwkv_gn_w, rwkv_gn_b, band_q_norm,
                             band_k_norm, band_rel_bias, gdn_conv_w, gdn_a_log, gdn_dt_bias, gdn_norm_w,
                             w_branch, w_gate, b_gate, w_out, w_router, router_bias, w_exp_gate, w_exp_up,
                             w_exp_down)
        xp, st_p = _layer(xp, mod_p[l], zero_state, wts, tm)
        xs, st_s = _layer(xs, mod_s[l], (state_rwkv_shift[l], state_rwkv_wkv[l], (k_cache, v_cache, l),
                                         state_gdn_conv[l], state_gdn_S[l]), wts, tm)
        new_p.append(st_p)
        new_s.append(st_s)
    p_out = [jnp.stack(z, axis=0) for z in zip(*new_p)]
    s_out = [jnp.stack(z, axis=0) for z in zip(*new_s)]
    return (xp, xs, *p_out, *s_out)
```

```python
import functools
import math

import jax
import jax.numpy as jnp
from jax import lax
from jax.experimental import pallas as pl
from jax.experimental.pallas import tpu as pltpu

F32 = jnp.float32
BF16 = jnp.bfloat16

D_MODEL = 1024
MIX = 512
HEAD_DIM = 64
N_HEADS = MIX // HEAD_DIM
LANES = 128
N_PAIRS = MIX // LANES
SUBLANES = 8
CHUNK = 64
TOKEN_TILE = 512
MIXER_ROWS = 512
MIXER_SPAN = 256
CUMSUM_ROWS = 256
BAND_SAMPLE_ROWS = 256
BAND_QUERIES = 2 * CHUNK
A_COLS = 3 * MIX + 64 + 64 + 128
B_COLS = 3 * MIX
C_MAIN = 4 * MIX
CONV_W = 4
BAND = 8 * CHUNK
MAX_REL = 2 * CHUNK
N_EXPERTS = 16
N_GROUPS = 4
EXPERT_DIM = D_MODEL // 4
RMS_EPS = 1e-6
A_GN_EPS = 64e-5
A_DECAY_SCALE = math.exp(-0.5)
LOG2_E = math.log2(math.e)
SCORE_SCALE = HEAD_DIM ** -0.5 * LOG2_E
VMEM_LIMIT_BYTES = 56 * 1024 * 1024


def _params(*sem):
    return pltpu.CompilerParams(dimension_semantics=sem, vmem_limit_bytes=VMEM_LIMIT_BYTES)


def _dot(a, b, precision=None):
    return lax.dot_general(a, b, (((1,), (0,)), ((), ())), precision=precision, preferred_element_type=F32)


def _dot_nt(a, b, precision=None):
    return lax.dot_general(a, b, (((1,), (1,)), ((), ())), precision=precision, preferred_element_type=F32)


def _bdot(a, b):
    return _dot(a.astype(BF16), b.astype(BF16))


def _bdot_nt(a, b):
    return _dot_nt(a.astype(BF16), b.astype(BF16))


def _dot_split(a, b, nt=False):
    dot = _dot_nt if nt else _dot
    a_hi, a_lo = _bf16_terms(a, 2)
    b_hi, b_lo = _bf16_terms(b, 2)
    return dot(a_hi, b_hi) + (dot(a_hi, b_lo) + dot(a_lo, b_hi))


def _bmm(a, b):
    return lax.dot_general(a.astype(BF16), b.astype(BF16), (((2,), (1,)), ((0,), (0,))),
                           preferred_element_type=F32)


def _bmm_nt(a, b):
    return lax.dot_general(a.astype(BF16), b.astype(BF16), (((2,), (2,)), ((0,), (0,))),
                           preferred_element_type=F32)


def _sigmoid(x):
    return 1.0 / (1.0 + jnp.exp(-x))


def _silu(x):
    return x * _sigmoid(x)


def _softplus(x):
    return jnp.maximum(x, 0.0) + jnp.log1p(jnp.exp(-jnp.abs(x)))


def _iota(shape, dim):
    return lax.broadcasted_iota(jnp.int32, shape, dim)


def _bf16_terms(x, terms):
    parts = []
    for _ in range(terms):
        part = x.astype(BF16)
        parts.append(part)
        x = x - part.astype(F32)
    return parts


def _dot_01(x, m, terms):
    out = None
    for part in _bf16_terms(x, terms):
        term = _dot(part, m)
        out = term if out is None else out + term
    return out


def _dot_01_left(m, x, terms):
    out = None
    for part in _bf16_terms(x, terms):
        term = _dot(m, part)
        out = term if out is None else out + term
    return out


def _head_ones():
    return (_iota((LANES, LANES), 0) // HEAD_DIM == _iota((LANES, LANES), 1) // HEAD_DIM).astype(BF16)


def _head_sum(x, ones):
    parts = [_dot_01(x[:, g * LANES:(g + 1) * LANES], ones, 2) for g in range(x.shape[1] // LANES)]
    return parts[0] if len(parts) == 1 else jnp.concatenate(parts, axis=1)


def _stack(x):
    lane = _iota(x.shape, 1)
    return jnp.concatenate([jnp.where(lane < HEAD_DIM, x, 0.0), jnp.where(lane >= HEAD_DIM, x, 0.0)], axis=0)


def _tiles(x, c):
    return jnp.stack([x[u * c:(u + 1) * c, j * LANES:(j + 1) * LANES]
                      for u in range(x.shape[0] // c) for j in range(N_PAIRS)])


def _untile(xs):
    return jnp.concatenate([jnp.concatenate([xs[u * N_PAIRS + j] for j in range(N_PAIRS)], axis=1)
                            for u in range(xs.shape[0] // N_PAIRS)], axis=0)


def _step_tiles(x, ch, bb, nch):
    parts = [x[(b * nch + ch) * N_PAIRS:(b * nch + ch + 1) * N_PAIRS] for b in range(bb)]
    return parts[0] if bb == 1 else jnp.concatenate(parts, axis=0)


def _tiles_from_steps(steps, bb, nch):
    parts = [steps[ch][b * N_PAIRS:(b + 1) * N_PAIRS] for b in range(bb) for ch in range(nch)]
    return parts[0] if len(parts) == 1 else jnp.concatenate(parts, axis=0)


def _chunk_cumsum(x, c):
    seg = min(x.shape[0], CUMSUM_ROWS)
    tril = _chunk_tril(seg, c)
    parts = [_dot_01_left(tril, x[s:s + seg], 3) for s in range(0, x.shape[0], seg)]
    return parts[0] if len(parts) == 1 else jnp.concatenate(parts, axis=0)


def _block_diag(x):
    x = x.astype(BF16)
    first = _iota(x.shape[1:], 1) < x.shape[2] // 2
    zero = jnp.zeros_like(x)
    return jnp.concatenate([jnp.where(first, x, zero), jnp.where(first, zero, x)], axis=1)


def _keep_head_blocks(x):
    same = _iota(x.shape[1:], 0) // HEAD_DIM == _iota(x.shape[1:], 1) // HEAD_DIM
    return jnp.where(same, x, 0.0)


def _transpose_tiles(xs):
    return jnp.stack([jnp.transpose(xs[g]) for g in range(xs.shape[0])])


def _chunk_last(x, c):
    return jnp.concatenate([jnp.broadcast_to(x[(u + 1) * c - 1:(u + 1) * c], (c, x.shape[1]))
                            for u in range(x.shape[0] // c)], axis=0)


def _column_scale(row_values, c):
    tiles = []
    for u in range(row_values.shape[0] // c):
        last = row_values[(u + 1) * c - 1:(u + 1) * c]
        for j in range(N_PAIRS):
            tiles.append(jnp.transpose(jnp.broadcast_to(last[:, j * LANES:(j + 1) * LANES], (LANES, LANES))))
    return jnp.stack(tiles)


def _chunk_tril(rows, c):
    row, col = _iota((rows, rows), 0), _iota((rows, rows), 1)
    return ((col <= row) & (row // c == col // c)).astype(BF16)


def _tri_masks(c):
    row, col = _iota((c, 2 * c), 0), _iota((c, 2 * c), 1) % c
    return col < row, col <= row


def _unit_lower_inverse(a, c):
    row, col = _iota((c, 2 * c), 0), _iota((c, 2 * c), 1) % c
    eye = (row == col).astype(F32)
    same = lambda s: row // s == col // s
    a0 = jnp.where(same(SUBLANES), a, 0.0)
    a2 = _bmm(a0, _block_diag(a0))
    a4 = _bmm(a2, _block_diag(a2))
    x = _bmm(_bmm(eye - a0, _block_diag(eye + a2)), _block_diag(eye + a4))
    s = SUBLANES
    while s < c:
        off = jnp.where(same(2 * s) & jnp.logical_not(same(s)), a, 0.0)
        x = x - _bmm(_bmm(x, _block_diag(off)), _block_diag(x))
        s *= 2
    return x


def _adaln_kernel(c_ref, w_ref, b_ref, o_ref):
    o_ref[0] = _dot_split(_silu(c_ref[...]), w_ref[0]) + b_ref[0]


def _adaln(c, w_ada, b_ada):
    depth, bsz, tn = w_ada.shape[0], c.shape[0], 768
    return pl.pallas_call(
        _adaln_kernel,
        grid=(depth, 6 * D_MODEL // tn),
        in_specs=[pl.BlockSpec((bsz, D_MODEL), lambda l, j: (0, 0)),
                  pl.BlockSpec((1, D_MODEL, tn), lambda l, j: (l, 0, j)),
                  pl.BlockSpec((1, 1, tn), lambda l, j: (l, 0, j))],
        out_specs=pl.BlockSpec((1, bsz, tn), lambda l, j: (l, 0, j)),
        out_shape=jax.ShapeDtypeStruct((depth, bsz, 6 * D_MODEL), F32),
        compiler_params=_params("arbitrary", "arbitrary"),
        name="adaln",
    )(c, w_ada, b_ada.reshape(depth, 1, 6 * D_MODEL))


def _norm_mod(x, norm_w, scale, shift):
    y = x * lax.rsqrt(jnp.mean(x * x, axis=-1, keepdims=True) + RMS_EPS)
    return y * norm_w * (1.0 + scale) + shift


def _inproj_kernel(x_ref, sh_ref, sc_ref, nw_ref, wa_ref, wb_ref, wc_ref, wab_ref, qn_ref, kn_ref,
                   pa_ref, q_ref, k_ref, v_ref, pc_ref, pab_ref):
    h = _norm_mod(x_ref[...], nw_ref[...], sc_ref[0], sh_ref[0]).astype(BF16)
    pa_ref[...] = _dot(h, wa_ref[...])
    pb = _dot(h, wb_ref[...])
    ones = _head_ones()

    def head_rms(y, w):
        return y * lax.rsqrt(_head_sum(y * y, ones) * (1.0 / HEAD_DIM) + RMS_EPS) * w

    q_ref[...] = (head_rms(pb[:, :MIX], qn_ref[...]) * SCORE_SCALE).astype(BF16)
    k_ref[...] = head_rms(pb[:, MIX:2 * MIX], kn_ref[...])
    v_ref[...] = pb[:, 2 * MIX:]
    pc_ref[...] = _dot(h, wc_ref[...])
    pab_ref[...] = _dot(h, wab_ref[...])


def _mod_spec(mod, tm, tiles_per_group):
    rows = mod.shape[1]
    return pl.BlockSpec((1, rows, D_MODEL), lambda i: (i // tiles_per_group, 0, 0))


def _const_spec(shape):
    zeros = (0,) * len(shape)
    return pl.BlockSpec(shape, lambda *_: zeros, pipeline_mode=pl.Buffered(1))


def _inproj(x, shift, scale, norm_w, wa, wb, wc, wab, q_norm, k_norm, tm, tiles_per_group):
    n = x.shape[0]
    row = lambda w: pl.BlockSpec((tm, w), lambda i: (i, 0))
    widths = (A_COLS, MIX, MIX, MIX, C_MAIN, LANES)
    return pl.pallas_call(
        _inproj_kernel,
        grid=(n // tm,),
        in_specs=[row(D_MODEL), _mod_spec(shift, tm, tiles_per_group), _mod_spec(scale, tm, tiles_per_group),
                  _const_spec((1, D_MODEL)), _const_spec(wa.shape), _const_spec(wb.shape), _const_spec(wc.shape),
                  _const_spec(wab.shape), _const_spec((1, MIX)), _const_spec((1, MIX))],
        out_specs=[row(w) for w in widths],
        out_shape=[jax.ShapeDtypeStruct((n, w), BF16 if i == 1 else F32) for i, w in enumerate(widths)],
        compiler_params=_params("arbitrary"),
        name="inproj",
    )(x, shift, scale, norm_w, wa, wb, wc, wab, q_norm, k_norm)


def _rwkv_kernel(p_ref, shift0_ref, st0_ref, mu_ref, w0_ref, a0_ref, wwa_ref, gup_ref, kk_ref, ka_ref, rk_ref,
                 gnw_ref, gnb_ref, o_ref, st_out_ref, ext_scr, st_scr, *, c):
    bb, tt, _ = p_ref.shape
    rows, nch, n = bb * tt, tt // c, 2 * c
    ci = pl.program_id(1)

    @pl.when(ci == 0)
    def _():
        ext_scr[:, SUBLANES - 1:SUBLANES, :] = shift0_ref[...]
        st_scr[...] = st0_ref[...].reshape(bb * N_PAIRS, LANES, LANES)

    ext_scr[:, SUBLANES:SUBLANES + tt, :] = p_ref[...]
    p = p_ref[...].reshape(rows, A_COLS)
    p_prev = ext_scr[:, SUBLANES - 1:SUBLANES - 1 + tt, :].reshape(rows, A_COLS)
    ext_scr[:, 0:SUBLANES, :] = ext_scr[:, tt:tt + SUBLANES, :]
    xs = p + (p_prev - p) * mu_ref[...]
    r, k, v = xs[:, :MIX], xs[:, MIX:2 * MIX], xs[:, 2 * MIX:3 * MIX]
    lora_in = xs[:, 3 * MIX:3 * MIX + LANES]
    gd = xs[:, 3 * MIX + LANES:]
    lane = _iota((1, LANES), 1)
    lora = _bdot(jnp.where(lane < HEAD_DIM, jnp.tanh(lora_in), lora_in), wwa_ref[...])
    log_w = -A_DECAY_SCALE * _sigmoid(w0_ref[...] + lora[:, :MIX])
    a = _sigmoid(a0_ref[...] + lora[:, MIX:])
    g = _bdot(_sigmoid(gd), gup_ref[...])
    ones = _head_ones()
    kk_raw = k * kk_ref[...]
    kk = kk_raw * lax.rsqrt(_head_sum(kk_raw * kk_raw, ones) + RMS_EPS)
    k = k * (1.0 + (a - 1.0) * ka_ref[...])
    b = kk * a
    cum = _chunk_cumsum(log_w, c)
    cum_last = _chunk_last(cum, c)
    e_cum, e_neg = jnp.exp(cum), jnp.exp(-cum)
    e_prev, e_tail = jnp.exp(cum - log_w), jnp.exp(cum_last - cum)
    strict, incl = _tri_masks(c)

    kk_g, r_g, v_g = _tiles(kk * e_prev, c), _tiles(r * e_cum, c), _tiles(v, c)
    amat = _bmm_nt(jnp.concatenate([kk_g, r_g], axis=1),
                   jnp.concatenate([_block_diag(_tiles(k * e_neg, c)), _block_diag(_tiles(b * e_neg, c))],
                                   axis=1))
    a_kk = jnp.where(strict, amat[:, :c, :n], 0.0)
    a_bk = jnp.where(strict, amat[:, :c, n:], 0.0)
    a_rk = jnp.where(incl, amat[:, c:, :n], 0.0)
    a_rb = jnp.where(incl, amat[:, c:, n:], 0.0)
    t_inv = _unit_lower_inverse(a_bk, c)
    av = _bmm(jnp.concatenate([a_kk, a_rk], axis=1), _block_diag(v_g))
    wu = _bmm(t_inv, jnp.concatenate([_block_diag(kk_g), _block_diag(av[:, :c])], axis=2))
    kd_v = _keep_head_blocks(_bmm(_transpose_tiles(_tiles(k * e_tail, c)), v_g))
    bd_t = _transpose_tiles(_tiles(b * e_tail, c))
    decay = _column_scale(e_cum, c)

    st = st_scr[...]
    o_steps = []
    for ch in range(nch):
        now = lambda x: _step_tiles(x, ch, bb, nch)
        wu_c = now(wu)
        zo = _bmm(jnp.concatenate([wu_c[:, :, :LANES], now(r_g)], axis=1), st)
        z = wu_c[:, :, LANES:] + zo[:, :c]
        o_steps.append(zo[:, c:] + now(av)[:, c:] - _bmm(now(a_rb), _block_diag(z)))
        st = now(decay) * st + now(kd_v) - _keep_head_blocks(_bmm(now(bd_t), z))
    st_scr[...] = st
    o = _untile(_tiles_from_steps(o_steps, bb, nch))

    mean = _head_sum(o, ones) * (1.0 / HEAD_DIM)
    cen = o - mean
    var = _head_sum(cen * cen, ones) * (1.0 / HEAD_DIM)
    o = cen * lax.rsqrt(var + A_GN_EPS) * gnw_ref[...] + gnb_ref[...]
    bonus = _head_sum(r * k * rk_ref[...], ones) * v
    o_ref[...] = ((o + bonus) * g).astype(BF16).reshape(bb, tt, MIX)
    st_out_ref[...] = st.reshape(bb, N_PAIRS, LANES, LANES)


def _mixer_tiling(bsz, t, c):
    tt = min(t, MIXER_SPAN)
    return min(bsz, MIXER_ROWS // tt), tt


def _rwkv(pa, shift0, st0, mu, w0, a0, wwa, gup, k_k, k_a, r_k, gn_w, gn_b, c):
    bsz, t, _ = pa.shape
    bb, tt = _mixer_tiling(bsz, t, c)
    vec = lambda w: _const_spec((1, w))
    state = pl.BlockSpec((bb, N_PAIRS, LANES, LANES), lambda b, i: (b, 0, 0, 0))
    return pl.pallas_call(
        functools.partial(_rwkv_kernel, c=c),
        grid=(bsz // bb, t // tt),
        in_specs=[pl.BlockSpec((bb, tt, A_COLS), lambda b, i: (b, i, 0)),
                  pl.BlockSpec((bb, 1, A_COLS), lambda b, i: (b, 0, 0)), state,
                  vec(A_COLS), vec(MIX), vec(MIX), _const_spec(wwa.shape), _const_spec(gup.shape),
                  vec(MIX), vec(MIX), vec(MIX), vec(MIX), vec(MIX)],
        out_specs=[pl.BlockSpec((bb, tt, MIX), lambda b, i: (b, i, 0)), state],
        out_shape=[jax.ShapeDtypeStruct((bsz, t, MIX), BF16),
                   jax.ShapeDtypeStruct((bsz, N_PAIRS, LANES, LANES), F32)],
        scratch_shapes=[pltpu.VMEM((bb, tt + SUBLANES, A_COLS), F32),
                        pltpu.VMEM((bb * N_PAIRS, LANES, LANES), F32)],
        compiler_params=_params("arbitrary", "arbitrary"),
        name="rwkv",
    )(pa, shift0, st0, mu, w0, a0, wwa, gup, k_k, k_a, r_k, gn_w, gn_b)


def _gdn_kernel(pc_ref, pab_ref, conv0_ref, st0_ref, convw_ref, alog_ref, dtb_ref, nw_ref, eab_ref,
                o_ref, st_out_ref, ext_scr, st_scr, *, c):
    bb, tt, _ = pc_ref.shape
    rows, nch, n = bb * tt, tt // c, 2 * c
    w3 = 3 * MIX
    ci = pl.program_id(1)

    @pl.when(ci == 0)
    def _():
        ext_scr[:, 0:SUBLANES, :] = conv0_ref[...]
        st_scr[...] = st0_ref[...].reshape(bb * N_PAIRS, LANES, LANES)

    ext_scr[:, SUBLANES:SUBLANES + tt, :] = pc_ref[:, :, :w3]
    conv = ext_scr[:, SUBLANES:SUBLANES + tt, :] * convw_ref[CONV_W - 1:CONV_W, :]
    for s in range(1, CONV_W):
        conv = conv + ext_scr[:, SUBLANES - s:SUBLANES - s + tt, :] * convw_ref[CONV_W - 1 - s:CONV_W - s, :]
    ext_scr[:, 0:SUBLANES, :] = ext_scr[:, tt:tt + SUBLANES, :]
    qkv = _silu(conv.reshape(rows, w3))
    ones = _head_ones()
    l2n = lambda y: y * lax.rsqrt(_head_sum(y * y, ones) + RMS_EPS)
    q = l2n(qkv[:, :MIX]) * (HEAD_DIM ** -0.5)
    k = l2n(qkv[:, MIX:2 * MIX])
    v = qkv[:, 2 * MIX:]
    zgate = pc_ref[:, :, w3:].reshape(rows, MIX)
    ab = _dot_01(pab_ref[...].reshape(rows, LANES), eab_ref[...], 3)
    log_alpha = -jnp.exp(alog_ref[...]) * _softplus(ab[:, :MIX] + dtb_ref[...])
    beta = _sigmoid(ab[:, MIX:])
    gcum = _chunk_cumsum(log_alpha, c)
    eg = jnp.exp(gcum)
    bk = beta * k
    strict, incl = _tri_masks(c)

    g_t = _tiles(gcum, c)
    g_col = jnp.concatenate([g_t[:, :, :c], g_t[:, :, HEAD_DIM:HEAD_DIM + c]], axis=2)
    g_time = _transpose_tiles(g_t)
    g_row = jnp.concatenate([g_time[:, 0:1, :], g_time[:, HEAD_DIM:HEAD_DIM + 1, :]], axis=2)
    diff = g_col - g_row
    dec_s = jnp.where(strict, jnp.exp(jnp.where(strict, diff, 0.0)), 0.0)
    dec_i = jnp.where(incl, jnp.exp(jnp.where(incl, diff, 0.0)), 0.0)
    qg_g = _tiles(q * eg, c)
    kmat = _bmm_nt(jnp.concatenate([_tiles(bk, c), _tiles(q, c)], axis=1), _block_diag(_tiles(k, c)))
    attn = kmat[:, c:] * dec_i
    t_inv = _unit_lower_inverse(kmat[:, :c] * dec_s, c)
    sol = _bmm(t_inv, jnp.concatenate([_block_diag(_tiles(beta * v, c)), _block_diag(_tiles(bk * eg, c))],
                                      axis=2))
    kd_t = _transpose_tiles(_tiles(k * jnp.exp(_chunk_last(gcum, c) - gcum), c))
    decay = _column_scale(eg, c)

    st = st_scr[...]
    o_steps = []
    for ch in range(nch):
        now = lambda x: _step_tiles(x, ch, bb, nch)
        sol_c = now(sol)
        wq = _bmm(jnp.concatenate([sol_c[:, :, LANES:], now(qg_g)], axis=1), st)
        delta = sol_c[:, :, :LANES] - wq[:, :c]
        o_steps.append(wq[:, c:] + _bmm(now(attn), _block_diag(delta)))
        st = now(decay) * st + _keep_head_blocks(_bmm(now(kd_t), delta))
    st_scr[...] = st
    o = _untile(_tiles_from_steps(o_steps, bb, nch))

    o = o * lax.rsqrt(_head_sum(o * o, ones) * (1.0 / HEAD_DIM) + RMS_EPS) * nw_ref[...]
    o_ref[...] = (o * _silu(zgate)).astype(BF16).reshape(bb, tt, MIX)
    st_out_ref[...] = st.reshape(bb, N_PAIRS, LANES, LANES)


def _gdn(pc, pab, conv0, st0, conv_w, a_log, dt_bias, norm_w, eab, c):
    bsz, t, _ = pc.shape
    bb, tt = _mixer_tiling(bsz, t, c)
    vec = lambda w: _const_spec((1, w))
    state = pl.BlockSpec((bb, N_PAIRS, LANES, LANES), lambda b, i: (b, 0, 0, 0))
    return pl.pallas_call(
        functools.partial(_gdn_kernel, c=c),
        grid=(bsz // bb, t // tt),
        in_specs=[pl.BlockSpec((bb, tt, C_MAIN), lambda b, i: (b, i, 0)),
                  pl.BlockSpec((bb, tt, LANES), lambda b, i: (b, i, 0)),
                  pl.BlockSpec((bb, SUBLANES, 3 * MIX), lambda b, i: (b, 0, 0)), state,
                  _const_spec((CONV_W, 3 * MIX)), vec(MIX), vec(MIX), vec(MIX), _const_spec(eab.shape)],
        out_specs=[pl.BlockSpec((bb, tt, MIX), lambda b, i: (b, i, 0)), state],
        out_shape=[jax.ShapeDtypeStruct((bsz, t, MIX), BF16),
                   jax.ShapeDtypeStruct((bsz, N_PAIRS, LANES, LANES), F32)],
        scratch_shapes=[pltpu.VMEM((bb, tt + SUBLANES, 3 * MIX), F32),
                        pltpu.VMEM((bb * N_PAIRS, LANES, LANES), F32)],
        compiler_params=_params("arbitrary", "arbitrary"),
        name="gdn",
    )(pc, pab, conv0, st0, conv_w, a_log, dt_bias, norm_w, eab)


def _band_core(q, k_g, v_g, bias):
    bb, cq, _ = q.shape
    wn = k_g.shape[1]
    q_g = jnp.stack([_stack(q[b, :, j * LANES:(j + 1) * LANES]) for b in range(bb) for j in range(N_PAIRS)])
    s = _bmm_nt(q_g, k_g)
    s = (s.reshape(bb, N_PAIRS, 2 * cq, wn) + bias).reshape(bb * N_PAIRS, 2 * cq, wn)
    e = jnp.exp2(s - jnp.max(s, axis=-1, keepdims=True))
    pv = _bmm(e, v_g) / jnp.sum(e, axis=-1, keepdims=True)
    lane = _iota((cq, LANES), 1)
    return jnp.stack([
        jnp.concatenate([jnp.where(lane < HEAD_DIM, pv[b * N_PAIRS + j, :cq], pv[b * N_PAIRS + j, cq:])
                         for j in range(N_PAIRS)], axis=1) for b in range(bb)]).astype(BF16)


def _band_prompt_kernel(q_ref, k_ref, v_ref, bias_ref, o_ref):
    cq = q_ref.shape[1]
    wn = BAND + cq
    first = pl.program_id(1) * cq - BAND
    start = pl.multiple_of(jnp.maximum(first, 0), LANES)
    shift = pl.multiple_of(start - first, LANES)
    window = lambda ref: jnp.stack([ref[0, pl.ds(start, wn), j * LANES:(j + 1) * LANES]
                                    for j in range(N_PAIRS)]).astype(BF16)
    o_ref[...] = _band_core(q_ref[...], window(k_ref), window(v_ref), bias_ref[:, :, pl.ds(shift, wn)])


def _band_prompt(q, k, v, table):
    bsz, t, _ = q.shape
    cq = BAND_QUERIES
    span = 2 * BAND + cq
    q_chunk = jnp.arange(cq)[:, None] // CHUNK
    k_chunk = jnp.arange(span)[None, :] // CHUNK
    in_band = (k_chunk >= q_chunk) & (k_chunk <= q_chunk + BAND // CHUNK)
    bias = jnp.where(in_band, _rel_bias(table, cq, span) * LOG2_E, -1e30).reshape(N_PAIRS, 2 * cq, span)
    full = pl.BlockSpec((1, t, MIX), lambda b, i: (b, 0, 0))
    return pl.pallas_call(
        _band_prompt_kernel,
        grid=(bsz, t // cq),
        in_specs=[pl.BlockSpec((1, cq, MIX), lambda b, i: (b, i, 0)), full, full, _const_spec(bias.shape)],
        out_specs=pl.BlockSpec((1, cq, MIX), lambda b, i: (b, i, 0)),
        out_shape=jax.ShapeDtypeStruct((bsz, t, MIX), BF16),
        compiler_params=_params("arbitrary", "arbitrary"),
        name="band_prompt",
    )(q, k, v, bias)


def _band_sample_kernel(q_ref, kp_ref, vp_ref, kn_ref, vn_ref, bias_ref, o_ref):
    bb = q_ref.shape[0]
    keys = lambda past, new: jnp.stack([
        jnp.concatenate([past[b, :, j * LANES:(j + 1) * LANES], new[b, :, j * LANES:(j + 1) * LANES]], axis=0)
        for b in range(bb) for j in range(N_PAIRS)]).astype(BF16)
    o_ref[...] = _band_core(q_ref[...], keys(kp_ref, kn_ref), keys(vp_ref, vn_ref), bias_ref[...])


def _band_sample(q, k, v, caches, table):
    k_past, v_past, layer = caches
    bsz, t, _ = q.shape
    past = k_past.shape[2]
    bb = min(bsz, BAND_SAMPLE_ROWS // t)
    bias = (_rel_bias(table, t, past + t) * LOG2_E).reshape(N_PAIRS, 2 * t, past + t)
    new = pl.BlockSpec((bb, t, MIX), lambda b: (b, 0, 0))
    old = pl.BlockSpec((None, bb, past, MIX), lambda b: (layer, b, 0, 0))
    return pl.pallas_call(
        _band_sample_kernel,
        grid=(bsz // bb,),
        in_specs=[new, old, old, new, new, _const_spec(bias.shape)],
        out_specs=new,
        out_shape=jax.ShapeDtypeStruct((bsz, t, MIX), BF16),
        compiler_params=_params("arbitrary"),
        name="band_sample",
    )(q, k_past, v_past, k, v, bias)


def _merge_kernel(x_ref, oa_ref, ob_ref, oc_ref, sh_ref, sc_ref, gm_ref, sh2_ref, sc2_ref, nw_ref, nw2_ref,
                  wgate_ref, bgate_ref, wbr_ref, wout_ref, wrt_ref, rb_ref,
                  x1_ref, h2_ref, comb_ref):
    x = x_ref[...]
    h = _norm_mod(x, nw_ref[...], sc_ref[0], sh_ref[0]).astype(BF16)
    gates = _sigmoid(_dot(h, wgate_ref[...]) + bgate_ref[...])
    mixed = None
    for i, o_ref in enumerate((oa_ref, ob_ref, oc_ref)):
        term = gates[:, i * D_MODEL:(i + 1) * D_MODEL] * _bdot(o_ref[...], wbr_ref[i])
        mixed = term if mixed is None else mixed + term
    x1 = x + gm_ref[0] * _bdot(mixed, wout_ref[...])
    x1_ref[...] = x1
    h2 = _norm_mod(x1, nw2_ref[...], sc2_ref[0], sh2_ref[0])
    h2_ref[...] = h2.astype(BF16)

    scores = _sigmoid(_dot_split(wrt_ref[...], h2, nt=True))
    sel = scores + rb_ref[...]
    tm = scores.shape[1]
    per = N_EXPERTS // N_GROUPS
    best_val, best = None, None
    for g in range(N_GROUPS):
        rows = [sel[g * per + i:g * per + i + 1, :] for i in range(per)]
        top2 = None
        for i in range(per):
            for i2 in range(i + 1, per):
                pair = rows[i] + rows[i2]
                top2 = pair if top2 is None else jnp.maximum(top2, pair)
        if g == 0:
            best_val, best = top2, jnp.zeros((1, tm), jnp.int32)
        else:
            better = top2 > best_val
            best = jnp.where(better, g, best)
            best_val = jnp.where(better, top2, best_val)
    eidx = _iota((N_EXPERTS, tm), 0)
    cand = jnp.where(eidx // per == best, sel, -jnp.inf)
    m1 = jnp.max(cand, axis=0, keepdims=True)
    i1 = jnp.min(jnp.where(cand == m1, eidx, N_EXPERTS), axis=0, keepdims=True)
    cand2 = jnp.where(eidx == i1, -jnp.inf, cand)
    m2 = jnp.max(cand2, axis=0, keepdims=True)
    i2 = jnp.min(jnp.where(cand2 == m2, eidx, N_EXPERTS), axis=0, keepdims=True)
    w1 = jnp.sum(jnp.where(eidx == i1, scores, 0.0), axis=0, keepdims=True)
    w2 = jnp.sum(jnp.where(eidx == i2, scores, 0.0), axis=0, keepdims=True)
    den = w1 + w2
    comb_ref[...] = jnp.where(eidx == i1, w1 / den, 0.0) + jnp.where(eidx == i2, w2 / den, 0.0)


def _merge(x, oa, ob, oc, mods, norm_w, norm2_w, w_gate, b_gate, w_branch, w_out, w_router_t, router_bias,
           tm, tiles_per_group):
    n = x.shape[0]
    row = lambda w: pl.BlockSpec((tm, w), lambda i: (i, 0))
    mod_specs = [_mod_spec(m, tm, tiles_per_group) for m in mods]
    return pl.pallas_call(
        _merge_kernel,
        grid=(n // tm,),
        in_specs=[row(D_MODEL), row(MIX), row(MIX), row(MIX)] + mod_specs
                 + [_const_spec((1, D_MODEL)), _const_spec((1, D_MODEL)), _const_spec(w_gate.shape),
                    _const_spec(b_gate.shape), _const_spec(w_branch.shape), _const_spec(w_out.shape),
                    _const_spec(w_router_t.shape), _const_spec(router_bias.shape)],
        out_specs=[row(D_MODEL), row(D_MODEL), pl.BlockSpec((N_EXPERTS, tm), lambda i: (0, i))],
        out_shape=[jax.ShapeDtypeStruct((n, D_MODEL), F32), jax.ShapeDtypeStruct((n, D_MODEL), BF16),
                   jax.ShapeDtypeStruct((N_EXPERTS, n), F32)],
        compiler_params=_params("arbitrary"),
        name="merge",
    )(x, oa, ob, oc, *mods, norm_w, norm2_w, w_gate, b_gate, w_branch, w_out, w_router_t, router_bias)


def _moe_kernel(x1_ref, h2_ref, comb_ref, gf_ref, wg_ref, wu_ref, wd_ref, o_ref):
    h2 = h2_ref[...]
    comb = jnp.transpose(comb_ref[...])
    per = N_EXPERTS // N_GROUPS
    acc = None
    for g in range(N_GROUPS):
        hid = [(_silu(_dot(h2, wg_ref[e])) * _dot(h2, wu_ref[e]) * comb[:, e:e + 1]).astype(BF16)
               for e in range(g * per, (g + 1) * per)]
        term = _dot(jnp.concatenate(hid, axis=1), wd_ref[g])
        acc = term if acc is None else acc + term
    o_ref[...] = x1_ref[...] + gf_ref[0] * acc


def _moe(x1, h2, comb, g_ffn, wg, wu, wd, tm, tiles_per_group):
    n = x1.shape[0]
    row = lambda w: pl.BlockSpec((tm, w), lambda i: (i, 0))
    return pl.pallas_call(
        _moe_kernel,
        grid=(n // tm,),
        in_specs=[row(D_MODEL), row(D_MODEL), pl.BlockSpec((N_EXPERTS, tm), lambda i: (0, i)),
                  _mod_spec(g_ffn, tm, tiles_per_group),
                  _const_spec(wg.shape), _const_spec(wu.shape), _const_spec(wd.shape)],
        out_specs=row(D_MODEL),
        out_shape=jax.ShapeDtypeStruct((n, D_MODEL), F32),
        compiler_params=_params("arbitrary"),
        name="moe",
    )(x1, h2, comb, g_ffn, wg, wu, wd)


def _pair_state(s):
    bsz = s.shape[0]
    s = s.reshape(bsz, N_PAIRS, 2, HEAD_DIM, HEAD_DIM)
    z = jnp.zeros_like(s[:, :, 0])
    top = jnp.concatenate([s[:, :, 0], z], axis=-1)
    bot = jnp.concatenate([z, s[:, :, 1]], axis=-1)
    return jnp.concatenate([top, bot], axis=-2)


def _unpair_state(s):
    bsz = s.shape[0]
    return jnp.stack([s[:, :, :HEAD_DIM, :HEAD_DIM], s[:, :, HEAD_DIM:, HEAD_DIM:]], axis=2).reshape(
        bsz, N_HEADS, HEAD_DIM, HEAD_DIM)


def _rel_bias(table, cq, wn):
    span = cq - 1 + wn
    offs = jnp.arange(-(cq - 1), wn + 1)
    diag = table[:, jnp.clip(BAND - offs, -MAX_REL, MAX_REL) + MAX_REL].astype(F32)
    skew = jnp.tile(diag, (1, cq))[:, :cq * span].reshape(-1, cq, span)
    return skew[:, :, cq - 1:]


def _layer(x, mod, state, wts, tm):
    bsz, t, _ = x.shape
    n = bsz * t
    c = min(CHUNK, t)
    shift0, wkv0, caches, conv0, s0 = state
    if t % tm == 0:
        tiles_per_group = t // tm
        mods = [m.reshape(bsz, 1, D_MODEL) for m in jnp.split(mod, 6, axis=-1)]
    else:
        tiles_per_group = 1
        mods = [jnp.repeat(m, t, axis=0).reshape(n // tm, tm, D_MODEL) for m in jnp.split(mod, 6, axis=-1)]
    sh_mix, sc_mix, g_mix, sh_ffn, sc_ffn, g_ffn = mods
    xf = x.reshape(n, D_MODEL)

    pa, q, k, v, pc, pab = _inproj(xf, sh_mix, sc_mix, wts["norm_mix_w"], wts["wa"], wts["wb"], wts["wc"],
                                   wts["wab"], wts["q_norm"], wts["k_norm"], tm, tiles_per_group)
    pa = pa.reshape(bsz, t, A_COLS)
    q, k, v = (z.reshape(bsz, t, MIX) for z in (q, k, v))
    pc = pc.reshape(bsz, t, C_MAIN)

    oa, wkv = _rwkv(pa, shift0.reshape(bsz, 1, A_COLS), _pair_state(jnp.swapaxes(wkv0, -1, -2)),
                    wts["mu"], wts["w0"], wts["a0"], wts["wwa"], wts["g_up"], wts["k_k"], wts["k_a"], wts["r_k"],
                    wts["gn_w"], wts["gn_b"], c)
    new_wkv = jnp.swapaxes(_unpair_state(wkv), -1, -2)

    if caches is None:
        ob = _band_prompt(q, k, v, wts["rel_bias"])
        new_k, new_v = k[:, -BAND:], v[:, -BAND:]
    else:
        ob = _band_sample(q, k, v, caches, wts["rel_bias"])
        new_k, new_v = k, v

    conv_pad = jnp.pad(conv0, ((0, 0), (SUBLANES - (CONV_W - 1), 0), (0, 0)))
    oc, s_new = _gdn(pc, pab.reshape(bsz, t, LANES), conv_pad, _pair_state(s0), wts["conv_w"], wts["a_log"],
                     wts["dt_bias"], wts["gdn_norm_w"], wts["eab"], c)
    tail = min(t, CONV_W - 1)
    new_conv = jnp.concatenate([conv0, pc[:, t - tail:, :3 * MIX]], axis=1)[:, -(CONV_W - 1):]

    x1, h2, comb = _merge(xf, oa.reshape(n, MIX), ob.reshape(n, MIX), oc.reshape(n, MIX),
                          [sh_mix, sc_mix, g_mix, sh_ffn, sc_ffn], wts["norm_mix_w"], wts["norm_ffn_w"],
                          wts["w_gate"], wts["b_gate"], wts["w_branch"], wts["w_out"], wts["w_router_t"],
                          wts["router_bias"], tm, tiles_per_group)
    x2 = _moe(x1, h2, comb, g_ffn, wts["wg"], wts["wu"], wts["wd"], tm, tiles_per_group)
    heads = lambda z: z.reshape(bsz, -1, N_HEADS, HEAD_DIM)
    return x2.reshape(bsz, t, D_MODEL), (pa[:, -1], new_wkv, heads(new_k), heads(new_v), new_conv,
                                         _unpair_state(s_new))


def _prepare_layer(l, w_in, norm_mix_w, norm_ffn_w, rwkv_mu, rwkv_w0, rwkv_w_up, rwkv_a0, rwkv_a_up, rwkv_g_up,
                   rwkv_k_k, rwkv_k_a, rwkv_r_k, rwkv_gn_w, rwkv_gn_b, band_q_norm, band_k_norm, band_rel_bias,
                   gdn_conv_w, gdn_a_log, gdn_dt_bias, gdn_norm_w, w_branch, w_gate, b_gate, w_out,
                   w_router, router_bias, w_exp_gate, w_exp_up, w_exp_down):
    row = lambda z: z.reshape(1, -1).astype(F32)
    per_head = lambda z: jnp.repeat(z, HEAD_DIM).reshape(1, MIX)
    win = w_in[l].astype(BF16)
    c0 = A_COLS + B_COLS
    wab = jnp.pad(win[:, c0 + C_MAIN:], ((0, 0), (0, LANES - 2 * N_HEADS)))
    zeros = jnp.zeros((HEAD_DIM, MIX), F32)
    wwa = jnp.concatenate([jnp.concatenate([rwkv_w_up[l], zeros], axis=1),
                           jnp.concatenate([zeros, rwkv_a_up[l]], axis=1)], axis=0)
    head_of_lane = jnp.arange(MIX) // HEAD_DIM
    src = jnp.arange(LANES)[:, None]
    eab = jnp.concatenate([(src == head_of_lane[None, :]), (src == N_HEADS + head_of_lane[None, :])],
                          axis=1).astype(BF16)
    return dict(
        norm_mix_w=row(norm_mix_w[l]), norm_ffn_w=row(norm_ffn_w[l]),
        wa=win[:, :A_COLS], wb=win[:, A_COLS:c0], wc=win[:, c0:c0 + C_MAIN], wab=wab,
        q_norm=row(jnp.tile(band_q_norm[l], N_HEADS)), k_norm=row(jnp.tile(band_k_norm[l], N_HEADS)),
        mu=row(rwkv_mu[l]), w0=row(rwkv_w0[l]), a0=row(rwkv_a0[l]), wwa=wwa, g_up=rwkv_g_up[l],
        k_k=row(rwkv_k_k[l]), k_a=row(rwkv_k_a[l]), r_k=row(rwkv_r_k[l]), gn_w=row(rwkv_gn_w[l]),
        gn_b=row(rwkv_gn_b[l]), rel_bias=band_rel_bias[l],
        conv_w=gdn_conv_w[l], a_log=per_head(gdn_a_log[l]), dt_bias=per_head(gdn_dt_bias[l]),
        gdn_norm_w=row(jnp.tile(gdn_norm_w[l], N_HEADS)), eab=eab,
        w_gate=w_gate[l].astype(BF16), b_gate=row(b_gate[l]), w_branch=w_branch[l].astype(BF16),
        w_out=w_out[l].astype(BF16), w_router_t=jnp.transpose(w_router), router_bias=router_bias.reshape(-1, 1),
        wg=w_exp_gate[l].astype(BF16), wu=w_exp_up[l].astype(BF16),
        wd=w_exp_down[l].astype(BF16).reshape(N_GROUPS, -1, D_MODEL),
    )


def kernel(x_prompt, x_sample, c_prompt, c_sample, state_rwkv_shift, state_rwkv_wkv, cache_band_k, cache_band_v, state_gdn_conv, state_gdn_S, w_ada, b_ada, norm_mix_w, norm_ffn_w, w_in, rwkv_mu, rwkv_w0, rwkv_w_up, rwkv_a0, rwkv_a_up, rwkv_g_up, rwkv_k_k, rwkv_k_a, rwkv_r_k, rwkv_gn_w, rwkv_gn_b, band_q_norm, band_k_norm, band_rel_bias, gdn_conv_w, gdn_a_log, gdn_dt_bias, gdn_norm_w, w_branch, w_gate, b_gate, w_out, w_router, router_bias, w_exp_gate, w_exp_up, w_exp_down):
    depth = w_ada.shape[0]
    bsz = x_prompt.shape[0]
    mod = _adaln(jnp.concatenate([c_prompt, c_sample], axis=0), w_ada, b_ada)
    mod_p, mod_s = mod[:, :bsz], mod[:, bsz:]
    zero_state = (jnp.zeros((bsz, A_COLS), F32), jnp.zeros((bsz, N_HEADS, HEAD_DIM, HEAD_DIM), F32), None,
                  jnp.zeros((bsz, CONV_W - 1, 3 * MIX), F32), jnp.zeros((bsz, N_HEADS, HEAD_DIM, HEAD_DIM), F32))
    k_cache = cache_band_k.reshape(cache_band_k.shape[:3] + (MIX,))
    v_cache = cache_band_v.reshape(cache_band_v.shape[:3] + (MIX,))
    tm = TOKEN_TILE
    xp, xs = x_prompt, x_sample
    new_p, new_s = [], []
    for l in range(depth):
        wts = _prepare_layer(l, w_in, norm_mix_w, norm_ffn_w, rwkv_mu, rwkv_w0, rwkv_w_up, rwkv_a0, rwkv_a_up,
                             rwkv_g_up, rwkv_k_k, rwkv_k_a, rwkv_r_k, rwkv_gn_w, rwkv_gn_b, band_q_norm,
                             band_k_norm, band_rel_bias, gdn_conv_w, gdn_a_log, gdn_dt_bias, gdn_norm_w,
                             w_branch, w_gate, b_gate, w_out, w_router, router_bias, w_exp_gate, w_exp_up,
                             w_exp_down)
        xp, st_p = _layer(xp, mod_p[l], zero_state, wts, tm)
        xs, st_s = _layer(xs, mod_s[l], (state_rwkv_shift[l], state_rwkv_wkv[l], (k_cache, v_cache, l),
                                         state_gdn_conv[l], state_gdn_S[l]), wts, tm)
        new_p.append(st_p)
        new_s.append(st_s)
    p_out = [jnp.stack(z, axis=0) for z in zip(*new_p)]
    s_out = [jnp.stack(z, axis=0) for z in zip(*new_s)]
    return (xp, xs, *p_out, *s_out)
```

```python
import functools
import math

import jax
import jax.numpy as jnp
from jax import lax
from jax.experimental import pallas as pl
from jax.experimental.pallas import tpu as pltpu

F32 = jnp.float32
BF16 = jnp.bfloat16

D_MODEL = 1024
MIX = 512
HEAD_DIM = 64
N_HEADS = MIX // HEAD_DIM
LANES = 128
N_PAIRS = MIX // LANES
SUBLANES = 8
CHUNK = 64
TOKEN_TILE = 512
MIXER_ROWS = 512
MIXER_SPAN = 256
CUMSUM_ROWS = 256
BAND_SAMPLE_ROWS = 256
BAND_QUERIES = 2 * CHUNK
ADALN_TILE = 768
A_LORA_W, A_LORA_A, A_LORA_G = 64, 64, 128
A_COLS = 3 * MIX + A_LORA_W + A_LORA_A + A_LORA_G
B_COLS = 3 * MIX
C_MAIN = 4 * MIX
CONV_W = 4
BAND = 8 * CHUNK
MAX_REL = 2 * CHUNK
N_EXPERTS = 16
N_GROUPS = 4
RMS_EPS = 1e-6
A_GN_EPS = 64e-5
A_DECAY_SCALE = math.exp(-0.5)
LOG2_E = math.log2(math.e)
SCORE_SCALE = HEAD_DIM ** -0.5 * LOG2_E
VMEM_LIMIT_BYTES = 56 * 1024 * 1024


def _params(*sem):
    return pltpu.CompilerParams(dimension_semantics=sem, vmem_limit_bytes=VMEM_LIMIT_BYTES)


def _dot(a, b):
    return lax.dot_general(a, b, (((1,), (0,)), ((), ())), preferred_element_type=F32)


def _dot_nt(a, b):
    return lax.dot_general(a, b, (((1,), (1,)), ((), ())), preferred_element_type=F32)


def _bdot(a, b):
    return _dot(a.astype(BF16), b.astype(BF16))


def _dot_split(a, b, nt=False):
    dot = _dot_nt if nt else _dot
    a_hi, a_lo = _bf16_terms(a, 2)
    b_hi, b_lo = _bf16_terms(b, 2)
    return dot(a_hi, b_hi) + (dot(a_hi, b_lo) + dot(a_lo, b_hi))


def _bmm(a, b):
    return lax.dot_general(a.astype(BF16), b.astype(BF16), (((2,), (1,)), ((0,), (0,))),
                           preferred_element_type=F32)


def _bmm_nt(a, b):
    return lax.dot_general(a.astype(BF16), b.astype(BF16), (((2,), (2,)), ((0,), (0,))),
                           preferred_element_type=F32)


def _sigmoid(x):
    return 1.0 / (1.0 + jnp.exp(-x))


def _silu(x):
    return x * _sigmoid(x)


def _softplus(x):
    return jnp.maximum(x, 0.0) + jnp.log1p(jnp.exp(-jnp.abs(x)))


def _iota(shape, dim):
    return lax.broadcasted_iota(jnp.int32, shape, dim)


def _bf16_terms(x, terms):
    parts = []
    for _ in range(terms):
        part = x.astype(BF16)
        parts.append(part)
        x = x - part.astype(F32)
    return parts


def _dot_01(x, m, terms):
    out = None
    for part in _bf16_terms(x, terms):
        term = _dot(part, m)
        out = term if out is None else out + term
    return out


def _dot_01_left(m, x, terms):
    out = None
    for part in _bf16_terms(x, terms):
        term = _dot(m, part)
        out = term if out is None else out + term
    return out


def _head_ones():
    return (_iota((LANES, LANES), 0) // HEAD_DIM == _iota((LANES, LANES), 1) // HEAD_DIM).astype(BF16)


def _head_sum(x, ones):
    parts = [_dot_01(x[:, g * LANES:(g + 1) * LANES], ones, 2) for g in range(x.shape[1] // LANES)]
    return parts[0] if len(parts) == 1 else jnp.concatenate(parts, axis=1)


def _stack(x):
    lane = _iota(x.shape, 1)
    return jnp.concatenate([jnp.where(lane < HEAD_DIM, x, 0.0), jnp.where(lane >= HEAD_DIM, x, 0.0)], axis=0)


def _tiles(x, c):
    return jnp.stack([x[u * c:(u + 1) * c, j * LANES:(j + 1) * LANES]
                      for u in range(x.shape[0] // c) for j in range(N_PAIRS)])


def _untile(xs):
    return jnp.concatenate([jnp.concatenate([xs[u * N_PAIRS + j] for j in range(N_PAIRS)], axis=1)
                            for u in range(xs.shape[0] // N_PAIRS)], axis=0)


def _step_tiles(x, ch, bb, nch):
    parts = [x[(b * nch + ch) * N_PAIRS:(b * nch + ch + 1) * N_PAIRS] for b in range(bb)]
    return parts[0] if bb == 1 else jnp.concatenate(parts, axis=0)


def _tiles_from_steps(steps, bb, nch):
    parts = [steps[ch][b * N_PAIRS:(b + 1) * N_PAIRS] for b in range(bb) for ch in range(nch)]
    return parts[0] if len(parts) == 1 else jnp.concatenate(parts, axis=0)


def _chunk_cumsum(x, c):
    seg = min(x.shape[0], CUMSUM_ROWS)
    tril = _chunk_tril(seg, c)
    parts = [_dot_01_left(tril, x[s:s + seg], 3) for s in range(0, x.shape[0], seg)]
    return parts[0] if len(parts) == 1 else jnp.concatenate(parts, axis=0)


def _block_diag(x):
    x = x.astype(BF16)
    first = _iota(x.shape[1:], 1) < x.shape[2] // 2
    zero = jnp.zeros_like(x)
    return jnp.concatenate([jnp.where(first, x, zero), jnp.where(first, zero, x)], axis=1)


def _keep_head_blocks(x):
    same = _iota(x.shape[1:], 0) // HEAD_DIM == _iota(x.shape[1:], 1) // HEAD_DIM
    return jnp.where(same, x, 0.0)


def _transpose_tiles(xs):
    return jnp.stack([jnp.transpose(xs[g]) for g in range(xs.shape[0])])


def _chunk_last(x, c):
    return jnp.concatenate([jnp.broadcast_to(x[(u + 1) * c - 1:(u + 1) * c], (c, x.shape[1]))
                            for u in range(x.shape[0] // c)], axis=0)


def _column_scale(row_values, c):
    tiles = []
    for u in range(row_values.shape[0] // c):
        last = row_values[(u + 1) * c - 1:(u + 1) * c]
        for j in range(N_PAIRS):
            tiles.append(jnp.transpose(jnp.broadcast_to(last[:, j * LANES:(j + 1) * LANES], (LANES, LANES))))
    return jnp.stack(tiles)


def _chunk_tril(rows, c):
    row, col = _iota((rows, rows), 0), _iota((rows, rows), 1)
    return ((col <= row) & (row // c == col // c)).astype(BF16)


def _tri_masks(c):
    row, col = _iota((c, 2 * c), 0), _iota((c, 2 * c), 1) % c
    return col < row, col <= row


def _unit_lower_inverse(a, c):
    row, col = _iota((c, 2 * c), 0), _iota((c, 2 * c), 1) % c
    eye = (row == col).astype(F32)
    same = lambda s: row // s == col // s
    a0 = jnp.where(same(SUBLANES), a, 0.0)
    a2 = _bmm(a0, _block_diag(a0))
    a4 = _bmm(a2, _block_diag(a2))
    x = _bmm(_bmm(eye - a0, _block_diag(eye + a2)), _block_diag(eye + a4))
    s = SUBLANES
    while s < c:
        off = jnp.where(same(2 * s) & jnp.logical_not(same(s)), a, 0.0)
        x = x - _bmm(_bmm(x, _block_diag(off)), _block_diag(x))
        s *= 2
    return x


def _adaln_kernel(c_ref, w_ref, b_ref, o_ref):
    o_ref[0] = _dot_split(_silu(c_ref[...]), w_ref[0]) + b_ref[0]


def _adaln(c, w_ada, b_ada):
    depth, bsz, tn = w_ada.shape[0], c.shape[0], ADALN_TILE
    return pl.pallas_call(
        _adaln_kernel,
        grid=(depth, 6 * D_MODEL // tn),
        in_specs=[pl.BlockSpec((bsz, D_MODEL), lambda l, j: (0, 0)),
                  pl.BlockSpec((1, D_MODEL, tn), lambda l, j: (l, 0, j)),
                  pl.BlockSpec((1, 1, tn), lambda l, j: (l, 0, j))],
        out_specs=pl.BlockSpec((1, bsz, tn), lambda l, j: (l, 0, j)),
        out_shape=jax.ShapeDtypeStruct((depth, bsz, 6 * D_MODEL), F32),
        compiler_params=_params("arbitrary", "arbitrary"),
        name="adaln",
    )(c, w_ada, b_ada.reshape(depth, 1, 6 * D_MODEL))


def _norm_mod(x, norm_w, scale, shift):
    y = x * lax.rsqrt(jnp.mean(x * x, axis=-1, keepdims=True) + RMS_EPS)
    return y * norm_w * (1.0 + scale) + shift


def _inproj_kernel(x_ref, sh_ref, sc_ref, nw_ref, wa_ref, wb_ref, wc_ref, wab_ref, qn_ref, kn_ref,
                   pa_ref, q_ref, k_ref, v_ref, pc_ref, pab_ref):
    h = _norm_mod(x_ref[...], nw_ref[...], sc_ref[0], sh_ref[0]).astype(BF16)
    pa_ref[...] = _dot(h, wa_ref[...])
    pb = _dot(h, wb_ref[...])
    ones = _head_ones()

    def head_rms(y, w):
        return y * lax.rsqrt(_head_sum(y * y, ones) * (1.0 / HEAD_DIM) + RMS_EPS) * w

    q_ref[...] = (head_rms(pb[:, :MIX], qn_ref[...]) * SCORE_SCALE).astype(BF16)
    k_ref[...] = head_rms(pb[:, MIX:2 * MIX], kn_ref[...])
    v_ref[...] = pb[:, 2 * MIX:]
    pc_ref[...] = _dot(h, wc_ref[...])
    pab_ref[...] = _dot(h, wab_ref[...])


def _mod_spec(mod, tm, tiles_per_group):
    rows = mod.shape[1]
    return pl.BlockSpec((1, rows, D_MODEL), lambda i: (i // tiles_per_group, 0, 0))


def _const_spec(shape):
    zeros = (0,) * len(shape)
    return pl.BlockSpec(shape, lambda *_: zeros, pipeline_mode=pl.Buffered(1))


def _inproj(x, shift, scale, norm_w, wa, wb, wc, wab, q_norm, k_norm, tm, tiles_per_group):
    n = x.shape[0]
    row = lambda w: pl.BlockSpec((tm, w), lambda i: (i, 0))
    widths = (A_COLS, MIX, MIX, MIX, C_MAIN, LANES)
    return pl.pallas_call(
        _inproj_kernel,
        grid=(n // tm,),
        in_specs=[row(D_MODEL), _mod_spec(shift, tm, tiles_per_group), _mod_spec(scale, tm, tiles_per_group),
                  _const_spec((1, D_MODEL)), _const_spec(wa.shape), _const_spec(wb.shape), _const_spec(wc.shape),
                  _const_spec(wab.shape), _const_spec((1, MIX)), _const_spec((1, MIX))],
        out_specs=[row(w) for w in widths],
        out_shape=[jax.ShapeDtypeStruct((n, w), BF16 if i == 1 else F32) for i, w in enumerate(widths)],
        compiler_params=_params("arbitrary"),
        name="inproj",
    )(x, shift, scale, norm_w, wa, wb, wc, wab, q_norm, k_norm)


def _rwkv_kernel(p_ref, shift0_ref, st0_ref, mu_ref, w0_ref, a0_ref, wwa_ref, gup_ref, kk_ref, ka_ref, rk_ref,
                 gnw_ref, gnb_ref, o_ref, st_out_ref, ext_scr, st_scr, *, c):
    bb, tt, _ = p_ref.shape
    rows, nch, n = bb * tt, tt // c, 2 * c
    ci = pl.program_id(1)

    @pl.when(ci == 0)
    def _():
        ext_scr[:, SUBLANES - 1:SUBLANES, :] = shift0_ref[...]
        st_scr[...] = st0_ref[...].reshape(bb * N_PAIRS, LANES, LANES)

    ext_scr[:, SUBLANES:SUBLANES + tt, :] = p_ref[...]
    p = p_ref[...].reshape(rows, A_COLS)
    p_prev = ext_scr[:, SUBLANES - 1:SUBLANES - 1 + tt, :].reshape(rows, A_COLS)
    ext_scr[:, 0:SUBLANES, :] = ext_scr[:, tt:tt + SUBLANES, :]
    xs = p + (p_prev - p) * mu_ref[...]
    r, k, v = xs[:, :MIX], xs[:, MIX:2 * MIX], xs[:, 2 * MIX:3 * MIX]
    lora_in = xs[:, 3 * MIX:3 * MIX + LANES]
    gd = xs[:, 3 * MIX + LANES:]
    lane = _iota((1, LANES), 1)
    lora = _bdot(jnp.where(lane < HEAD_DIM, jnp.tanh(lora_in), lora_in), wwa_ref[...])
    log_w = -A_DECAY_SCALE * _sigmoid(w0_ref[...] + lora[:, :MIX])
    a = _sigmoid(a0_ref[...] + lora[:, MIX:])
    g = _bdot(_sigmoid(gd), gup_ref[...])
    ones = _head_ones()
    kk_raw = k * kk_ref[...]
    kk = kk_raw * lax.rsqrt(_head_sum(kk_raw * kk_raw, ones) + RMS_EPS)
    k = k * (1.0 + (a - 1.0) * ka_ref[...])
    b = kk * a
    cum = _chunk_cumsum(log_w, c)
    cum_last = _chunk_last(cum, c)
    e_cum, e_neg = jnp.exp(cum), jnp.exp(-cum)
    e_prev, e_tail = jnp.exp(cum - log_w), jnp.exp(cum_last - cum)
    strict, incl = _tri_masks(c)

    kk_g, r_g, v_g = _tiles(kk * e_prev, c), _tiles(r * e_cum, c), _tiles(v, c)
    amat = _bmm_nt(jnp.concatenate([kk_g, r_g], axis=1),
                   jnp.concatenate([_block_diag(_tiles(k * e_neg, c)), _block_diag(_tiles(b * e_neg, c))],
                                   axis=1))
    a_kk = jnp.where(strict, amat[:, :c, :n], 0.0)
    a_bk = jnp.where(strict, amat[:, :c, n:], 0.0)
    a_rk = jnp.where(incl, amat[:, c:, :n], 0.0)
    a_rb = jnp.where(incl, amat[:, c:, n:], 0.0)
    t_inv = _unit_lower_inverse(a_bk, c)
    av = _bmm(jnp.concatenate([a_kk, a_rk], axis=1), _block_diag(v_g))
    wu = _bmm(t_inv, jnp.concatenate([_block_diag(kk_g), _block_diag(av[:, :c])], axis=2))
    kd_v = _keep_head_blocks(_bmm(_transpose_tiles(_tiles(k * e_tail, c)), v_g))
    bd_t = _transpose_tiles(_tiles(b * e_tail, c))
    decay = _column_scale(e_cum, c)

    st = st_scr[...]
    o_steps = []
    for ch in range(nch):
        now = lambda x: _step_tiles(x, ch, bb, nch)
        wu_c = now(wu)
        zo = _bmm(jnp.concatenate([wu_c[:, :, :LANES], now(r_g)], axis=1), st)
        z = wu_c[:, :, LANES:] + zo[:, :c]
        o_steps.append(zo[:, c:] + now(av)[:, c:] - _bmm(now(a_rb), _block_diag(z)))
        st = now(decay) * st + now(kd_v) - _keep_head_blocks(_bmm(now(bd_t), z))
    st_scr[...] = st
    o = _untile(_tiles_from_steps(o_steps, bb, nch))

    mean = _head_sum(o, ones) * (1.0 / HEAD_DIM)
    cen = o - mean
    var = _head_sum(cen * cen, ones) * (1.0 / HEAD_DIM)
    o = cen * lax.rsqrt(var + A_GN_EPS) * gnw_ref[...] + gnb_ref[...]
    bonus = _head_sum(r * k * rk_ref[...], ones) * v
    o_ref[...] = ((o + bonus) * g).astype(BF16).reshape(bb, tt, MIX)
    st_out_ref[...] = st.reshape(bb, N_PAIRS, LANES, LANES)


def _mixer_tiling(bsz, t, c):
    tt = min(t, MIXER_SPAN)
    return min(bsz, MIXER_ROWS // tt), tt


def _rwkv(pa, shift0, st0, mu, w0, a0, wwa, gup, k_k, k_a, r_k, gn_w, gn_b, c):
    bsz, t, _ = pa.shape
    bb, tt = _mixer_tiling(bsz, t, c)
    vec = lambda w: _const_spec((1, w))
    state = pl.BlockSpec((bb, N_PAIRS, LANES, LANES), lambda b, i: (b, 0, 0, 0))
    return pl.pallas_call(
        functools.partial(_rwkv_kernel, c=c),
        grid=(bsz // bb, t // tt),
        in_specs=[pl.BlockSpec((bb, tt, A_COLS), lambda b, i: (b, i, 0)),
                  pl.BlockSpec((bb, 1, A_COLS), lambda b, i: (b, 0, 0)), state,
                  vec(A_COLS), vec(MIX), vec(MIX), _const_spec(wwa.shape), _const_spec(gup.shape),
                  vec(MIX), vec(MIX), vec(MIX), vec(MIX), vec(MIX)],
        out_specs=[pl.BlockSpec((bb, tt, MIX), lambda b, i: (b, i, 0)), state],
        out_shape=[jax.ShapeDtypeStruct((bsz, t, MIX), BF16),
                   jax.ShapeDtypeStruct((bsz, N_PAIRS, LANES, LANES), F32)],
        scratch_shapes=[pltpu.VMEM((bb, tt + SUBLANES, A_COLS), F32),
                        pltpu.VMEM((bb * N_PAIRS, LANES, LANES), F32)],
        compiler_params=_params("arbitrary", "arbitrary"),
        name="rwkv",
    )(pa, shift0, st0, mu, w0, a0, wwa, gup, k_k, k_a, r_k, gn_w, gn_b)


def _gdn_kernel(pc_ref, pab_ref, conv0_ref, st0_ref, convw_ref, alog_ref, dtb_ref, nw_ref, eab_ref,
                o_ref, st_out_ref, ext_scr, st_scr, *, c):
    bb, tt, _ = pc_ref.shape
    rows, nch, n = bb * tt, tt // c, 2 * c
    w3 = 3 * MIX
    ci = pl.program_id(1)

    @pl.when(ci == 0)
    def _():
        ext_scr[:, 0:SUBLANES, :] = conv0_ref[...]
        st_scr[...] = st0_ref[...].reshape(bb * N_PAIRS, LANES, LANES)

    ext_scr[:, SUBLANES:SUBLANES + tt, :] = pc_ref[:, :, :w3]
    conv = ext_scr[:, SUBLANES:SUBLANES + tt, :] * convw_ref[CONV_W - 1:CONV_W, :]
    for s in range(1, CONV_W):
        conv = conv + ext_scr[:, SUBLANES - s:SUBLANES - s + tt, :] * convw_ref[CONV_W - 1 - s:CONV_W - s, :]
    ext_scr[:, 0:SUBLANES, :] = ext_scr[:, tt:tt + SUBLANES, :]
    qkv = _silu(conv.reshape(rows, w3))
    ones = _head_ones()
    l2n = lambda y: y * lax.rsqrt(_head_sum(y * y, ones) + RMS_EPS)
    q = l2n(qkv[:, :MIX]) * (HEAD_DIM ** -0.5)
    k = l2n(qkv[:, MIX:2 * MIX])
    v = qkv[:, 2 * MIX:]
    zgate = pc_ref[:, :, w3:].reshape(rows, MIX)
    ab = _dot_01(pab_ref[...].reshape(rows, LANES), eab_ref[...], 3)
    log_alpha = -jnp.exp(alog_ref[...]) * _softplus(ab[:, :MIX] + dtb_ref[...])
    beta = _sigmoid(ab[:, MIX:])
    gcum = _chunk_cumsum(log_alpha, c)
    eg = jnp.exp(gcum)
    bk = beta * k
    strict, incl = _tri_masks(c)

    g_t = _tiles(gcum, c)
    g_col = jnp.concatenate([g_t[:, :, :c], g_t[:, :, HEAD_DIM:HEAD_DIM + c]], axis=2)
    g_time = _transpose_tiles(g_t)
    g_row = jnp.concatenate([g_time[:, 0:1, :], g_time[:, HEAD_DIM:HEAD_DIM + 1, :]], axis=2)
    diff = g_col - g_row
    dec_s = jnp.where(strict, jnp.exp(jnp.where(strict, diff, 0.0)), 0.0)
    dec_i = jnp.where(incl, jnp.exp(jnp.where(incl, diff, 0.0)), 0.0)
    qg_g = _tiles(q * eg, c)
    kmat = _bmm_nt(jnp.concatenate([_tiles(bk, c), _tiles(q, c)], axis=1), _block_diag(_tiles(k, c)))
    attn = kmat[:, c:] * dec_i
    t_inv = _unit_lower_inverse(kmat[:, :c] * dec_s, c)
    sol = _bmm(t_inv, jnp.concatenate([_block_diag(_tiles(beta * v, c)), _block_diag(_tiles(bk * eg, c))],
                                      axis=2))
    kd_t = _transpose_tiles(_tiles(k * jnp.exp(_chunk_last(gcum, c) - gcum), c))
    decay = _column_scale(eg, c)

    st = st_scr[...]
    o_steps = []
    for ch in range(nch):
        now = lambda x: _step_tiles(x, ch, bb, nch)
        sol_c = now(sol)
        wq = _bmm(jnp.concatenate([sol_c[:, :, LANES:], now(qg_g)], axis=1), st)
        delta = sol_c[:, :, :LANES] - wq[:, :c]
        o_steps.append(wq[:, c:] + _bmm(now(attn), _block_diag(delta)))
        st = now(decay) * st + _keep_head_blocks(_bmm(now(kd_t), delta))
    st_scr[...] = st
    o = _untile(_tiles_from_steps(o_steps, bb, nch))

    o = o * lax.rsqrt(_head_sum(o * o, ones) * (1.0 / HEAD_DIM) + RMS_EPS) * nw_ref[...]
    o_ref[...] = (o * _silu(zgate)).astype(BF16).reshape(bb, tt, MIX)
    st_out_ref[...] = st.reshape(bb, N_PAIRS, LANES, LANES)


def _gdn(pc, pab, conv0, st0, conv_w, a_log, dt_bias, norm_w, eab, c):
    bsz, t, _ = pc.shape
    bb, tt = _mixer_tiling(bsz, t, c)
    vec = lambda w: _const_spec((1, w))
    state = pl.BlockSpec((bb, N_PAIRS, LANES, LANES), lambda b, i: (b, 0, 0, 0))
    return pl.pallas_call(
        functools.partial(_gdn_kernel, c=c),
        grid=(bsz // bb, t // tt),
        in_specs=[pl.BlockSpec((bb, tt, C_MAIN), lambda b, i: (b, i, 0)),
                  pl.BlockSpec((bb, tt, LANES), lambda b, i: (b, i, 0)),
                  pl.BlockSpec((bb, SUBLANES, 3 * MIX), lambda b, i: (b, 0, 0)), state,
                  _const_spec((CONV_W, 3 * MIX)), vec(MIX), vec(MIX), vec(MIX), _const_spec(eab.shape)],
        out_specs=[pl.BlockSpec((bb, tt, MIX), lambda b, i: (b, i, 0)), state],
        out_shape=[jax.ShapeDtypeStruct((bsz, t, MIX), BF16),
                   jax.ShapeDtypeStruct((bsz, N_PAIRS, LANES, LANES), F32)],
        scratch_shapes=[pltpu.VMEM((bb, tt + SUBLANES, 3 * MIX), F32),
                        pltpu.VMEM((bb * N_PAIRS, LANES, LANES), F32)],
        compiler_params=_params("arbitrary", "arbitrary"),
        name="gdn",
    )(pc, pab, conv0, st0, conv_w, a_log, dt_bias, norm_w, eab)


def _band_core(q, k_g, v_g, bias):
    bb, cq, _ = q.shape
    wn = k_g.shape[1]
    q_g = jnp.stack([_stack(q[b, :, j * LANES:(j + 1) * LANES]) for b in range(bb) for j in range(N_PAIRS)])
    s = _bmm_nt(q_g, k_g)
    s = (s.reshape(bb, N_PAIRS, 2 * cq, wn) + bias).reshape(bb * N_PAIRS, 2 * cq, wn)
    e = jnp.exp2(s - jnp.max(s, axis=-1, keepdims=True))
    pv = _bmm(e, v_g) / jnp.sum(e, axis=-1, keepdims=True)
    lane = _iota((cq, LANES), 1)
    return jnp.stack([
        jnp.concatenate([jnp.where(lane < HEAD_DIM, pv[b * N_PAIRS + j, :cq], pv[b * N_PAIRS + j, cq:])
                         for j in range(N_PAIRS)], axis=1) for b in range(bb)]).astype(BF16)


def _band_prompt_kernel(q_ref, k_ref, v_ref, bias_ref, o_ref):
    cq = q_ref.shape[1]
    wn = BAND + cq
    first = pl.program_id(1) * cq - BAND
    start = pl.multiple_of(jnp.maximum(first, 0), LANES)
    shift = pl.multiple_of(start - first, LANES)
    window = lambda ref: jnp.stack([ref[0, pl.ds(start, wn), j * LANES:(j + 1) * LANES]
                                    for j in range(N_PAIRS)]).astype(BF16)
    o_ref[...] = _band_core(q_ref[...], window(k_ref), window(v_ref), bias_ref[:, :, pl.ds(shift, wn)])


def _band_prompt(q, k, v, table):
    bsz, t, _ = q.shape
    cq = BAND_QUERIES
    span = 2 * BAND + cq
    q_chunk = jnp.arange(cq)[:, None] // CHUNK
    k_chunk = jnp.arange(span)[None, :] // CHUNK
    in_band = (k_chunk >= q_chunk) & (k_chunk <= q_chunk + BAND // CHUNK)
    bias = jnp.where(in_band, _rel_bias(table, cq, span) * LOG2_E, -1e30).reshape(N_PAIRS, 2 * cq, span)
    full = pl.BlockSpec((1, t, MIX), lambda b, i: (b, 0, 0))
    return pl.pallas_call(
        _band_prompt_kernel,
        grid=(bsz, t // cq),
        in_specs=[pl.BlockSpec((1, cq, MIX), lambda b, i: (b, i, 0)), full, full, _const_spec(bias.shape)],
        out_specs=pl.BlockSpec((1, cq, MIX), lambda b, i: (b, i, 0)),
        out_shape=jax.ShapeDtypeStruct((bsz, t, MIX), BF16),
        compiler_params=_params("arbitrary", "arbitrary"),
        name="band_prompt",
    )(q, k, v, bias)


def _band_sample_kernel(q_ref, kp_ref, vp_ref, kn_ref, vn_ref, bias_ref, o_ref):
    bb = q_ref.shape[0]
    keys = lambda past, new: jnp.stack([
        jnp.concatenate([past[b, :, j * LANES:(j + 1) * LANES], new[b, :, j * LANES:(j + 1) * LANES]], axis=0)
        for b in range(bb) for j in range(N_PAIRS)]).astype(BF16)
    o_ref[...] = _band_core(q_ref[...], keys(kp_ref, kn_ref), keys(vp_ref, vn_ref), bias_ref[...])


def _band_sample(q, k, v, caches, table):
    k_past, v_past, layer = caches
    bsz, t, _ = q.shape
    past = k_past.shape[2]
    bb = min(bsz, BAND_SAMPLE_ROWS // t)
    bias = (_rel_bias(table, t, past + t) * LOG2_E).reshape(N_PAIRS, 2 * t, past + t)
    new = pl.BlockSpec((bb, t, MIX), lambda b: (b, 0, 0))
    old = pl.BlockSpec((None, bb, past, MIX), lambda b: (layer, b, 0, 0))
    return pl.pallas_call(
        _band_sample_kernel,
        grid=(bsz // bb,),
        in_specs=[new, old, old, new, new, _const_spec(bias.shape)],
        out_specs=new,
        out_shape=jax.ShapeDtypeStruct((bsz, t, MIX), BF16),
        compiler_params=_params("arbitrary"),
        name="band_sample",
    )(q, k_past, v_past, k, v, bias)


def _merge_kernel(x_ref, oa_ref, ob_ref, oc_ref, sh_ref, sc_ref, gm_ref, sh2_ref, sc2_ref, nw_ref, nw2_ref,
                  wgate_ref, bgate_ref, wbr_ref, wout_ref, wrt_ref, rb_ref,
                  x1_ref, h2_ref, comb_ref):
    x = x_ref[...]
    h = _norm_mod(x, nw_ref[...], sc_ref[0], sh_ref[0]).astype(BF16)
    gates = _sigmoid(_dot(h, wgate_ref[...]) + bgate_ref[...])
    mixed = None
    for i, o_ref in enumerate((oa_ref, ob_ref, oc_ref)):
        term = gates[:, i * D_MODEL:(i + 1) * D_MODEL] * _bdot(o_ref[...], wbr_ref[i])
        mixed = term if mixed is None else mixed + term
    x1 = x + gm_ref[0] * _bdot(mixed, wout_ref[...])
    x1_ref[...] = x1
    h2 = _norm_mod(x1, nw2_ref[...], sc2_ref[0], sh2_ref[0])
    h2_ref[...] = h2.astype(BF16)

    scores = _sigmoid(_dot_split(wrt_ref[...], h2, nt=True))
    sel = scores + rb_ref[...]
    tm = scores.shape[1]
    per = N_EXPERTS // N_GROUPS
    best_val, best = None, None
    for g in range(N_GROUPS):
        rows = [sel[g * per + i:g * per + i + 1, :] for i in range(per)]
        top2 = None
        for i in range(per):
            for i2 in range(i + 1, per):
                pair = rows[i] + rows[i2]
                top2 = pair if top2 is None else jnp.maximum(top2, pair)
        if g == 0:
            best_val, best = top2, jnp.zeros((1, tm), jnp.int32)
        else:
            better = top2 > best_val
            best = jnp.where(better, g, best)
            best_val = jnp.where(better, top2, best_val)
    eidx = _iota((N_EXPERTS, tm), 0)
    cand = jnp.where(eidx // per == best, sel, -jnp.inf)
    m1 = jnp.max(cand, axis=0, keepdims=True)
    i1 = jnp.min(jnp.where(cand == m1, eidx, N_EXPERTS), axis=0, keepdims=True)
    cand2 = jnp.where(eidx == i1, -jnp.inf, cand)
    m2 = jnp.max(cand2, axis=0, keepdims=True)
    i2 = jnp.min(jnp.where(cand2 == m2, eidx, N_EXPERTS), axis=0, keepdims=True)
    w1 = jnp.sum(jnp.where(eidx == i1, scores, 0.0), axis=0, keepdims=True)
    w2 = jnp.sum(jnp.where(eidx == i2, scores, 0.0), axis=0, keepdims=True)
    den = w1 + w2
    comb_ref[...] = jnp.where(eidx == i1, w1 / den, 0.0) + jnp.where(eidx == i2, w2 / den, 0.0)


def _merge(x, oa, ob, oc, mods, norm_w, norm2_w, w_gate, b_gate, w_branch, w_out, w_router_t, router_bias,
           tm, tiles_per_group):
    n = x.shape[0]
    row = lambda w: pl.BlockSpec((tm, w), lambda i: (i, 0))
    mod_specs = [_mod_spec(m, tm, tiles_per_group) for m in mods]
    return pl.pallas_call(
        _merge_kernel,
        grid=(n // tm,),
        in_specs=[row(D_MODEL), row(MIX), row(MIX), row(MIX)] + mod_specs
                 + [_const_spec((1, D_MODEL)), _const_spec((1, D_MODEL)), _const_spec(w_gate.shape),
                    _const_spec(b_gate.shape), _const_spec(w_branch.shape), _const_spec(w_out.shape),
                    _const_spec(w_router_t.shape), _const_spec(router_bias.shape)],
        out_specs=[row(D_MODEL), row(D_MODEL), pl.BlockSpec((N_EXPERTS, tm), lambda i: (0, i))],
        out_shape=[jax.ShapeDtypeStruct((n, D_MODEL), F32), jax.ShapeDtypeStruct((n, D_MODEL), BF16),
                   jax.ShapeDtypeStruct((N_EXPERTS, n), F32)],
        compiler_params=_params("arbitrary"),
        name="merge",
    )(x, oa, ob, oc, *mods, norm_w, norm2_w, w_gate, b_gate, w_branch, w_out, w_router_t, router_bias)


def _moe_kernel(x1_ref, h2_ref, comb_ref, gf_ref, wg_ref, wu_ref, wd_ref, o_ref):
    h2 = h2_ref[...]
    comb = jnp.transpose(comb_ref[...])
    per = N_EXPERTS // N_GROUPS
    acc = None
    for g in range(N_GROUPS):
        hid = [(_silu(_dot(h2, wg_ref[e])) * _dot(h2, wu_ref[e]) * comb[:, e:e + 1]).astype(BF16)
               for e in range(g * per, (g + 1) * per)]
        term = _dot(jnp.concatenate(hid, axis=1), wd_ref[g])
        acc = term if acc is None else acc + term
    o_ref[...] = x1_ref[...] + gf_ref[0] * acc


def _moe(x1, h2, comb, g_ffn, wg, wu, wd, tm, tiles_per_group):
    n = x1.shape[0]
    row = lambda w: pl.BlockSpec((tm, w), lambda i: (i, 0))
    return pl.pallas_call(
        _moe_kernel,
        grid=(n // tm,),
        in_specs=[row(D_MODEL), row(D_MODEL), pl.BlockSpec((N_EXPERTS, tm), lambda i: (0, i)),
                  _mod_spec(g_ffn, tm, tiles_per_group),
                  _const_spec(wg.shape), _const_spec(wu.shape), _const_spec(wd.shape)],
        out_specs=row(D_MODEL),
        out_shape=jax.ShapeDtypeStruct((n, D_MODEL), F32),
        compiler_params=_params("arbitrary"),
        name="moe",
    )(x1, h2, comb, g_ffn, wg, wu, wd)


def _pair_state(s):
    bsz = s.shape[0]
    s = s.reshape(bsz, N_PAIRS, 2, HEAD_DIM, HEAD_DIM)
    z = jnp.zeros_like(s[:, :, 0])
    top = jnp.concatenate([s[:, :, 0], z], axis=-1)
    bot = jnp.concatenate([z, s[:, :, 1]], axis=-1)
    return jnp.concatenate([top, bot], axis=-2)


def _unpair_state(s):
    bsz = s.shape[0]
    return jnp.stack([s[:, :, :HEAD_DIM, :HEAD_DIM], s[:, :, HEAD_DIM:, HEAD_DIM:]], axis=2).reshape(
        bsz, N_HEADS, HEAD_DIM, HEAD_DIM)


def _rel_bias(table, cq, wn):
    span = cq - 1 + wn
    offs = jnp.arange(-(cq - 1), wn + 1)
    diag = table[:, jnp.clip(BAND - offs, -MAX_REL, MAX_REL) + MAX_REL].astype(F32)
    skew = jnp.tile(diag, (1, cq))[:, :cq * span].reshape(-1, cq, span)
    return skew[:, :, cq - 1:]


def _layer(x, mod, state, wts, tm):
    bsz, t, _ = x.shape
    n = bsz * t
    c = min(CHUNK, t)
    shift0, wkv0, caches, conv0, s0 = state
    if t % tm == 0:
        tiles_per_group = t // tm
        mods = [m.reshape(bsz, 1, D_MODEL) for m in jnp.split(mod, 6, axis=-1)]
    else:
        tiles_per_group = 1
        mods = [jnp.repeat(m, t, axis=0).reshape(n // tm, tm, D_MODEL) for m in jnp.split(mod, 6, axis=-1)]
    sh_mix, sc_mix, g_mix, sh_ffn, sc_ffn, g_ffn = mods
    xf = x.reshape(n, D_MODEL)

    pa, q, k, v, pc, pab = _inproj(xf, sh_mix, sc_mix, wts["norm_mix_w"], wts["wa"], wts["wb"], wts["wc"],
                                   wts["wab"], wts["q_norm"], wts["k_norm"], tm, tiles_per_group)
    pa = pa.reshape(bsz, t, A_COLS)
    q, k, v = (z.reshape(bsz, t, MIX) for z in (q, k, v))
    pc = pc.reshape(bsz, t, C_MAIN)

    oa, wkv = _rwkv(pa, shift0.reshape(bsz, 1, A_COLS), _pair_state(jnp.swapaxes(wkv0, -1, -2)),
                    wts["mu"], wts["w0"], wts["a0"], wts["wwa"], wts["g_up"], wts["k_k"], wts["k_a"], wts["r_k"],
                    wts["gn_w"], wts["gn_b"], c)
    new_wkv = jnp.swapaxes(_unpair_state(wkv), -1, -2)

    if caches is None:
        ob = _band_prompt(q, k, v, wts["rel_bias"])
        new_k, new_v = k[:, -BAND:], v[:, -BAND:]
    else:
        ob = _band_sample(q, k, v, caches, wts["rel_bias"])
        new_k, new_v = k, v

    conv_pad = jnp.pad(conv0, ((0, 0), (SUBLANES - (CONV_W - 1), 0), (0, 0)))
    oc, s_new = _gdn(pc, pab.reshape(bsz, t, LANES), conv_pad, _pair_state(s0), wts["conv_w"], wts["a_log"],
                     wts["dt_bias"], wts["gdn_norm_w"], wts["eab"], c)
    tail = min(t, CONV_W - 1)
    new_conv = jnp.concatenate([conv0, pc[:, t - tail:, :3 * MIX]], axis=1)[:, -(CONV_W - 1):]

    x1, h2, comb = _merge(xf, oa.reshape(n, MIX), ob.reshape(n, MIX), oc.reshape(n, MIX),
                          [sh_mix, sc_mix, g_mix, sh_ffn, sc_ffn], wts["norm_mix_w"], wts["norm_ffn_w"],
                          wts["w_gate"], wts["b_gate"], wts["w_branch"], wts["w_out"], wts["w_router_t"],
                          wts["router_bias"], tm, tiles_per_group)
    x2 = _moe(x1, h2, comb, g_ffn, wts["wg"], wts["wu"], wts["wd"], tm, tiles_per_group)
    heads = lambda z: z.reshape(bsz, -1, N_HEADS, HEAD_DIM)
    return x2.reshape(bsz, t, D_MODEL), (pa[:, -1], new_wkv, heads(new_k), heads(new_v), new_conv,
                                         _unpair_state(s_new))


def _prepare_layer(l, w_in, norm_mix_w, norm_ffn_w, rwkv_mu, rwkv_w0, rwkv_w_up, rwkv_a0, rwkv_a_up, rwkv_g_up,
                   rwkv_k_k, rwkv_k_a, rwkv_r_k, rwkv_gn_w, rwkv_gn_b, band_q_norm, band_k_norm, band_rel_bias,
                   gdn_conv_w, gdn_a_log, gdn_dt_bias, gdn_norm_w, w_branch, w_gate, b_gate, w_out,
                   w_router, router_bias, w_exp_gate, w_exp_up, w_exp_down):
    row = lambda z: z.reshape(1, -1).astype(F32)
    per_head = lambda z: jnp.repeat(z, HEAD_DIM).reshape(1, MIX)
    win = w_in[l].astype(BF16)
    c0 = A_COLS + B_COLS
    wab = jnp.pad(win[:, c0 + C_MAIN:], ((0, 0), (0, LANES - 2 * N_HEADS)))
    zeros = jnp.zeros((HEAD_DIM, MIX), F32)
    wwa = jnp.concatenate([jnp.concatenate([rwkv_w_up[l], zeros], axis=1),
                           jnp.concatenate([zeros, rwkv_a_up[l]], axis=1)], axis=0)
    head_of_lane = jnp.arange(MIX) // HEAD_DIM
    src = jnp.arange(LANES)[:, None]
    eab = jnp.concatenate([(src == head_of_lane[None, :]), (src == N_HEADS + head_of_lane[None, :])],
                          axis=1).astype(BF16)
    return dict(
        norm_mix_w=row(norm_mix_w[l]), norm_ffn_w=row(norm_ffn_w[l]),
        wa=win[:, :A_COLS], wb=win[:, A_COLS:c0], wc=win[:, c0:c0 + C_MAIN], wab=wab,
        q_norm=row(jnp.tile(band_q_norm[l], N_HEADS)), k_norm=row(jnp.tile(band_k_norm[l], N_HEADS)),
        mu=row(rwkv_mu[l]), w0=row(rwkv_w0[l]), a0=row(rwkv_a0[l]), wwa=wwa, g_up=rwkv_g_up[l],
        k_k=row(rwkv_k_k[l]), k_a=row(rwkv_k_a[l]), r_k=row(rwkv_r_k[l]), gn_w=row(rwkv_gn_w[l]),
        gn_b=row(rwkv_gn_b[l]), rel_bias=band_rel_bias[l],
        conv_w=gdn_conv_w[l], a_log=per_head(gdn_a_log[l]), dt_bias=per_head(gdn_dt_bias[l]),
        gdn_norm_w=row(jnp.tile(gdn_norm_w[l], N_HEADS)), eab=eab,
        w_gate=w_gate[l].astype(BF16), b_gate=row(b_gate[l]), w_branch=w_branch[l].astype(BF16),
        w_out=w_out[l].astype(BF16), w_router_t=jnp.transpose(w_router), router_bias=router_bias.reshape(-1, 1),
        wg=w_exp_gate[l].astype(BF16), wu=w_exp_up[l].astype(BF16),
        wd=w_exp_down[l].astype(BF16).reshape(N_GROUPS, -1, D_MODEL),
    )


def kernel(x_prompt, x_sample, c_prompt, c_sample, state_rwkv_shift, state_rwkv_wkv, cache_band_k, cache_band_v, state_gdn_conv, state_gdn_S, w_ada, b_ada, norm_mix_w, norm_ffn_w, w_in, rwkv_mu, rwkv_w0, rwkv_w_up, rwkv_a0, rwkv_a_up, rwkv_g_up, rwkv_k_k, rwkv_k_a, rwkv_r_k, rwkv_gn_w, rwkv_gn_b, band_q_norm, band_k_norm, band_rel_bias, gdn_conv_w, gdn_a_log, gdn_dt_bias, gdn_norm_w, w_branch, w_gate, b_gate, w_out, w_router, router_bias, w_exp_gate, w_exp_up, w_exp_down):
    depth = w_ada.shape[0]
    bsz = x_prompt.shape[0]
    mod = _adaln(jnp.concatenate([c_prompt, c_sample], axis=0), w_ada, b_ada)
    mod_p, mod_s = mod[:, :bsz], mod[:, bsz:]
    zero_state = (jnp.zeros((bsz, A_COLS), F32), jnp.zeros((bsz, N_HEADS, HEAD_DIM, HEAD_DIM), F32), None,
                  jnp.zeros((bsz, CONV_W - 1, 3 * MIX), F32), jnp.zeros((bsz, N_HEADS, HEAD_DIM, HEAD_DIM), F32))
    k_cache = cache_band_k.reshape(cache_band_k.shape[:3] + (MIX,))
    v_cache = cache_band_v.reshape(cache_band_v.shape[:3] + (MIX,))
    tm = TOKEN_TILE
    xp, xs = x_prompt, x_sample
    new_p, new_s = [], []
    for l in range(depth):
        wts = _prepare_layer(l, w_in, norm_mix_w, norm_ffn_w, rwkv_mu, rwkv_w0, rwkv_w_up, rwkv_a0, rwkv_a_up,
                             rwkv_g_up, rwkv_k_k, rwkv_k_a, rwkv_r_k, rwkv_gn_w, rwkv_gn_b, band_q_norm,
                             band_k_norm, band_rel_bias, gdn_conv_w, gdn_a_log, gdn_dt_bias, gdn_norm_w,
                             w_branch, w_gate, b_gate, w_out, w_router, router_bias, w_exp_gate, w_exp_up,
                             w_exp_down)
        xp, st_p = _layer(xp, mod_p[l], zero_state, wts, tm)
        xs, st_s = _layer(xs, mod_s[l], (state_rwkv_shift[l], state_rwkv_wkv[l], (k_cache, v_cache, l),
                                         state_gdn_conv[l], state_gdn_S[l]), wts, tm)
        new_p.append(st_p)
        new_s.append(st_s)
    p_out = [jnp.stack(z, axis=0) for z in zip(*new_p)]
    s_out = [jnp.stack(z, axis=0) for z in zip(*new_s)]
    return (xp, xs, *p_out, *s_out)
```

```python
import functools
import math

import jax
import jax.numpy as jnp
from jax import lax
from jax.experimental import pallas as pl
from jax.experimental.pallas import tpu as pltpu

F32 = jnp.float32
BF16 = jnp.bfloat16

D_MODEL = 1024
MIX = 512
HEAD_DIM = 64
N_HEADS = MIX // HEAD_DIM
LANES = 128
N_PAIRS = MIX // LANES
SUBLANES = 8
CHUNK = 64
TOKEN_TILE = 512
MIXER_ROWS = 512
MIXER_SPAN = 256
CUMSUM_ROWS = 256
BAND_SAMPLE_ROWS = 256
BAND_QUERIES = 2 * CHUNK
BAND_PROMPT_ROWS = 2
ADALN_TILE = 768
A_LORA_W, A_LORA_A, A_LORA_G = 64, 64, 128
A_COLS = 3 * MIX + A_LORA_W + A_LORA_A + A_LORA_G
B_COLS = 3 * MIX
C_MAIN = 4 * MIX
CONV_W = 4
BAND = 8 * CHUNK
MAX_REL = 2 * CHUNK
N_EXPERTS = 16
N_GROUPS = 4
RMS_EPS = 1e-6
A_GN_EPS = 64e-5
A_DECAY_SCALE = math.exp(-0.5)
LOG2_E = math.log2(math.e)
SCORE_SCALE = HEAD_DIM ** -0.5 * LOG2_E
VMEM_LIMIT_BYTES = 56 * 1024 * 1024


def _params(*sem):
    return pltpu.CompilerParams(dimension_semantics=sem, vmem_limit_bytes=VMEM_LIMIT_BYTES)


def _dot(a, b):
    return lax.dot_general(a, b, (((1,), (0,)), ((), ())), preferred_element_type=F32)


def _dot_nt(a, b):
    return lax.dot_general(a, b, (((1,), (1,)), ((), ())), preferred_element_type=F32)


def _bdot(a, b):
    return _dot(a.astype(BF16), b.astype(BF16))


def _dot_split(a, b, nt=False):
    dot = _dot_nt if nt else _dot
    a_hi, a_lo = _bf16_terms(a, 2)
    b_hi, b_lo = _bf16_terms(b, 2)
    return dot(a_hi, b_hi) + (dot(a_hi, b_lo) + dot(a_lo, b_hi))


def _bmm(a, b):
    return lax.dot_general(a.astype(BF16), b.astype(BF16), (((2,), (1,)), ((0,), (0,))),
                           preferred_element_type=F32)


def _bmm_nt(a, b):
    return lax.dot_general(a.astype(BF16), b.astype(BF16), (((2,), (2,)), ((0,), (0,))),
                           preferred_element_type=F32)


def _sigmoid(x):
    return 1.0 / (1.0 + jnp.exp(-x))


def _silu(x):
    return x * _sigmoid(x)


def _softplus(x):
    return jnp.maximum(x, 0.0) + jnp.log1p(jnp.exp(-jnp.abs(x)))


def _iota(shape, dim):
    return lax.broadcasted_iota(jnp.int32, shape, dim)


def _bf16_terms(x, terms):
    parts = []
    for _ in range(terms):
        part = x.astype(BF16)
        parts.append(part)
        x = x - part.astype(F32)
    return parts


def _dot_01(x, m, terms):
    out = None
    for part in _bf16_terms(x, terms):
        term = _dot(part, m)
        out = term if out is None else out + term
    return out


def _dot_01_left(m, x, terms):
    out = None
    for part in _bf16_terms(x, terms):
        term = _dot(m, part)
        out = term if out is None else out + term
    return out


def _head_ones():
    return (_iota((LANES, LANES), 0) // HEAD_DIM == _iota((LANES, LANES), 1) // HEAD_DIM).astype(BF16)


def _head_sum(x, ones):
    parts = [_dot_01(x[:, g * LANES:(g + 1) * LANES], ones, 2) for g in range(x.shape[1] // LANES)]
    return parts[0] if len(parts) == 1 else jnp.concatenate(parts, axis=1)


def _stack(x):
    lane = _iota(x.shape, 1)
    return jnp.concatenate([jnp.where(lane < HEAD_DIM, x, 0.0), jnp.where(lane >= HEAD_DIM, x, 0.0)], axis=0)


def _tiles(x, c):
    return jnp.stack([x[u * c:(u + 1) * c, j * LANES:(j + 1) * LANES]
                      for u in range(x.shape[0] // c) for j in range(N_PAIRS)])


def _untile(xs):
    return jnp.concatenate([jnp.concatenate([xs[u * N_PAIRS + j] for j in range(N_PAIRS)], axis=1)
                            for u in range(xs.shape[0] // N_PAIRS)], axis=0)


def _step_tiles(x, ch, bb, nch):
    parts = [x[(b * nch + ch) * N_PAIRS:(b * nch + ch + 1) * N_PAIRS] for b in range(bb)]
    return parts[0] if bb == 1 else jnp.concatenate(parts, axis=0)


def _tiles_from_steps(steps, bb, nch):
    parts = [steps[ch][b * N_PAIRS:(b + 1) * N_PAIRS] for b in range(bb) for ch in range(nch)]
    return parts[0] if len(parts) == 1 else jnp.concatenate(parts, axis=0)


def _chunk_cumsum(x, c):
    seg = min(x.shape[0], CUMSUM_ROWS)
    tril = _chunk_tril(seg, c)
    parts = [_dot_01_left(tril, x[s:s + seg], 3) for s in range(0, x.shape[0], seg)]
    return parts[0] if len(parts) == 1 else jnp.concatenate(parts, axis=0)


def _block_diag(x):
    x = x.astype(BF16)
    first = _iota(x.shape[1:], 1) < x.shape[2] // 2
    zero = jnp.zeros_like(x)
    return jnp.concatenate([jnp.where(first, x, zero), jnp.where(first, zero, x)], axis=1)


def _keep_head_blocks(x):
    same = _iota(x.shape[1:], 0) // HEAD_DIM == _iota(x.shape[1:], 1) // HEAD_DIM
    return jnp.where(same, x, 0.0)


def _transpose_tiles(xs):
    return jnp.stack([jnp.transpose(xs[g]) for g in range(xs.shape[0])])


def _chunk_last(x, c):
    return jnp.concatenate([jnp.broadcast_to(x[(u + 1) * c - 1:(u + 1) * c], (c, x.shape[1]))
                            for u in range(x.shape[0] // c)], axis=0)


def _column_scale(row_values, c):
    tiles = []
    for u in range(row_values.shape[0] // c):
        last = row_values[(u + 1) * c - 1:(u + 1) * c]
        for j in range(N_PAIRS):
            tiles.append(jnp.transpose(jnp.broadcast_to(last[:, j * LANES:(j + 1) * LANES], (LANES, LANES))))
    return jnp.stack(tiles)


def _chunk_tril(rows, c):
    row, col = _iota((rows, rows), 0), _iota((rows, rows), 1)
    return ((col <= row) & (row // c == col // c)).astype(BF16)


def _tri_masks(c):
    row, col = _iota((c, 2 * c), 0), _iota((c, 2 * c), 1) % c
    return col < row, col <= row


def _unit_lower_inverse(a, c):
    row, col = _iota((c, 2 * c), 0), _iota((c, 2 * c), 1) % c
    eye = (row == col).astype(F32)
    same = lambda s: row // s == col // s
    a0 = jnp.where(same(SUBLANES), a, 0.0)
    a2 = _bmm(a0, _block_diag(a0))
    a4 = _bmm(a2, _block_diag(a2))
    x = _bmm(_bmm(eye - a0, _block_diag(eye + a2)), _block_diag(eye + a4))
    s = SUBLANES
    while s < c:
        off = jnp.where(same(2 * s) & jnp.logical_not(same(s)), a, 0.0)
        x = x - _bmm(_bmm(x, _block_diag(off)), _block_diag(x))
        s *= 2
    return x


def _adaln_kernel(c_ref, w_ref, b_ref, o_ref):
    o_ref[0] = _dot_split(_silu(c_ref[...]), w_ref[0]) + b_ref[0]


def _adaln(c, w_ada, b_ada):
    depth, bsz, tn = w_ada.shape[0], c.shape[0], ADALN_TILE
    return pl.pallas_call(
        _adaln_kernel,
        grid=(depth, 6 * D_MODEL // tn),
        in_specs=[pl.BlockSpec((bsz, D_MODEL), lambda l, j: (0, 0)),
                  pl.BlockSpec((1, D_MODEL, tn), lambda l, j: (l, 0, j)),
                  pl.BlockSpec((1, 1, tn), lambda l, j: (l, 0, j))],
        out_specs=pl.BlockSpec((1, bsz, tn), lambda l, j: (l, 0, j)),
        out_shape=jax.ShapeDtypeStruct((depth, bsz, 6 * D_MODEL), F32),
        compiler_params=_params("arbitrary", "arbitrary"),
        name="adaln",
    )(c, w_ada, b_ada.reshape(depth, 1, 6 * D_MODEL))


def _norm_mod(x, norm_w, scale, shift):
    y = x * lax.rsqrt(jnp.mean(x * x, axis=-1, keepdims=True) + RMS_EPS)
    return y * norm_w * (1.0 + scale) + shift


def _inproj_kernel(x_ref, sh_ref, sc_ref, nw_ref, wa_ref, wb_ref, wc_ref, wab_ref, qn_ref, kn_ref,
                   pa_ref, q_ref, k_ref, v_ref, kb_ref, vb_ref, pc_ref, pab_ref):
    h = _norm_mod(x_ref[...], nw_ref[...], sc_ref[0], sh_ref[0]).astype(BF16)
    pa_ref[...] = _dot(h, wa_ref[...])
    pb = _dot(h, wb_ref[...])
    ones = _head_ones()

    def head_rms(y, w):
        return y * lax.rsqrt(_head_sum(y * y, ones) * (1.0 / HEAD_DIM) + RMS_EPS) * w

    q_ref[...] = (head_rms(pb[:, :MIX], qn_ref[...]) * SCORE_SCALE).astype(BF16)
    k = head_rms(pb[:, MIX:2 * MIX], kn_ref[...])
    v = pb[:, 2 * MIX:]
    k_ref[...], v_ref[...] = k, v
    kb_ref[...], vb_ref[...] = k.astype(BF16), v.astype(BF16)
    pc_ref[...] = _dot(h, wc_ref[...])
    pab_ref[...] = _dot(h, wab_ref[...])


def _mod_spec(mod, tm, tiles_per_group):
    rows = mod.shape[1]
    return pl.BlockSpec((1, rows, D_MODEL), lambda i: (i // tiles_per_group, 0, 0))


def _const_spec(shape):
    zeros = (0,) * len(shape)
    return pl.BlockSpec(shape, lambda *_: zeros, pipeline_mode=pl.Buffered(1))


def _inproj(x, shift, scale, norm_w, wa, wb, wc, wab, q_norm, k_norm, tm, tiles_per_group, kv_tail_only):
    n = x.shape[0]
    row = lambda w: pl.BlockSpec((tm, w), lambda i: (i, 0))
    if kv_tail_only:
        kv_rows, kv = n // tiles_per_group, pl.BlockSpec((tm, MIX), lambda i: (i // tiles_per_group, 0))
    else:
        kv_rows, kv = n, row(MIX)
    shape = lambda rows, w, dtype: jax.ShapeDtypeStruct((rows, w), dtype)
    return pl.pallas_call(
        _inproj_kernel,
        grid=(n // tm,),
        in_specs=[row(D_MODEL), _mod_spec(shift, tm, tiles_per_group), _mod_spec(scale, tm, tiles_per_group),
                  _const_spec((1, D_MODEL)), _const_spec(wa.shape), _const_spec(wb.shape), _const_spec(wc.shape),
                  _const_spec(wab.shape), _const_spec((1, MIX)), _const_spec((1, MIX))],
        out_specs=[row(A_COLS), row(MIX), kv, kv, row(MIX), row(MIX), row(C_MAIN), row(LANES)],
        out_shape=[shape(n, A_COLS, F32), shape(n, MIX, BF16), shape(kv_rows, MIX, F32), shape(kv_rows, MIX, F32),
                   shape(n, MIX, BF16), shape(n, MIX, BF16), shape(n, C_MAIN, F32), shape(n, LANES, F32)],
        compiler_params=_params("arbitrary"),
        name="inproj",
    )(x, shift, scale, norm_w, wa, wb, wc, wab, q_norm, k_norm)


def _rwkv_kernel(p_ref, shift0_ref, st0_ref, mu_ref, w0_ref, a0_ref, wwa_ref, gup_ref, kk_ref, ka_ref, rk_ref,
                 gnw_ref, gnb_ref, o_ref, st_out_ref, ext_scr, st_scr, *, c):
    bb, tt, _ = p_ref.shape
    rows, nch, n = bb * tt, tt // c, 2 * c
    ci = pl.program_id(1)

    @pl.when(ci == 0)
    def _():
        ext_scr[:, SUBLANES - 1:SUBLANES, :] = shift0_ref[...]
        st_scr[...] = st0_ref[...].reshape(bb * N_PAIRS, LANES, LANES)

    ext_scr[:, SUBLANES:SUBLANES + tt, :] = p_ref[...]
    p = p_ref[...].reshape(rows, A_COLS)
    p_prev = ext_scr[:, SUBLANES - 1:SUBLANES - 1 + tt, :].reshape(rows, A_COLS)
    ext_scr[:, 0:SUBLANES, :] = ext_scr[:, tt:tt + SUBLANES, :]
    xs = p + (p_prev - p) * mu_ref[...]
    r, k, v = xs[:, :MIX], xs[:, MIX:2 * MIX], xs[:, 2 * MIX:3 * MIX]
    lora_in = xs[:, 3 * MIX:3 * MIX + LANES]
    gd = xs[:, 3 * MIX + LANES:]
    lane = _iota((1, LANES), 1)
    lora = _bdot(jnp.where(lane < HEAD_DIM, jnp.tanh(lora_in), lora_in), wwa_ref[...])
    log_w = -A_DECAY_SCALE * _sigmoid(w0_ref[...] + lora[:, :MIX])
    a = _sigmoid(a0_ref[...] + lora[:, MIX:])
    g = _bdot(_sigmoid(gd), gup_ref[...])
    ones = _head_ones()
    kk_raw = k * kk_ref[...]
    kk = kk_raw * lax.rsqrt(_head_sum(kk_raw * kk_raw, ones) + RMS_EPS)
    k = k * (1.0 + (a - 1.0) * ka_ref[...])
    b = kk * a
    cum = _chunk_cumsum(log_w, c)
    cum_last = _chunk_last(cum, c)
    e_cum, e_neg = jnp.exp(cum), jnp.exp(-cum)
    e_prev, e_tail = jnp.exp(cum - log_w), jnp.exp(cum_last - cum)
    strict, incl = _tri_masks(c)

    kk_g, r_g, v_g = _tiles(kk * e_prev, c), _tiles(r * e_cum, c), _tiles(v, c)
    amat = _bmm_nt(jnp.concatenate([kk_g, r_g], axis=1),
                   jnp.concatenate([_block_diag(_tiles(k * e_neg, c)), _block_diag(_tiles(b * e_neg, c))],
                                   axis=1))
    a_kk = jnp.where(strict, amat[:, :c, :n], 0.0)
    a_bk = jnp.where(strict, amat[:, :c, n:], 0.0)
    a_rk = jnp.where(incl, amat[:, c:, :n], 0.0)
    a_rb = jnp.where(incl, amat[:, c:, n:], 0.0)
    t_inv = _unit_lower_inverse(a_bk, c)
    av = _bmm(jnp.concatenate([a_kk, a_rk], axis=1), _block_diag(v_g))
    wu = _bmm(t_inv, jnp.concatenate([_block_diag(kk_g), _block_diag(av[:, :c])], axis=2))
    kd_v = _keep_head_blocks(_bmm(_transpose_tiles(_tiles(k * e_tail, c)), v_g))
    bd_t = _transpose_tiles(_tiles(b * e_tail, c))
    decay = _column_scale(e_cum, c)

    st = st_scr[...]
    o_steps = []
    for ch in range(nch):
        now = lambda x: _step_tiles(x, ch, bb, nch)
        wu_c = now(wu)
        zo = _bmm(jnp.concatenate([wu_c[:, :, :LANES], now(r_g)], axis=1), st)
        z = wu_c[:, :, LANES:] + zo[:, :c]
        o_steps.append(zo[:, c:] + now(av)[:, c:] - _bmm(now(a_rb), _block_diag(z)))
        st = now(decay) * st + now(kd_v) - _keep_head_blocks(_bmm(now(bd_t), z))
    st_scr[...] = st
    o = _untile(_tiles_from_steps(o_steps, bb, nch))

    mean = _head_sum(o, ones) * (1.0 / HEAD_DIM)
    cen = o - mean
    var = _head_sum(cen * cen, ones) * (1.0 / HEAD_DIM)
    o = cen * lax.rsqrt(var + A_GN_EPS) * gnw_ref[...] + gnb_ref[...]
    bonus = _head_sum(r * k * rk_ref[...], ones) * v
    o_ref[...] = ((o + bonus) * g).astype(BF16).reshape(bb, tt, MIX)
    st_out_ref[...] = st.reshape(bb, N_PAIRS, LANES, LANES)


def _mixer_tiling(bsz, t, c):
    tt = min(t, MIXER_SPAN)
    return min(bsz, MIXER_ROWS // tt), tt


def _rwkv(pa, shift0, st0, mu, w0, a0, wwa, gup, k_k, k_a, r_k, gn_w, gn_b, c):
    bsz, t, _ = pa.shape
    bb, tt = _mixer_tiling(bsz, t, c)
    vec = lambda w: _const_spec((1, w))
    state = pl.BlockSpec((bb, N_PAIRS, LANES, LANES), lambda b, i: (b, 0, 0, 0))
    return pl.pallas_call(
        functools.partial(_rwkv_kernel, c=c),
        grid=(bsz // bb, t // tt),
        in_specs=[pl.BlockSpec((bb, tt, A_COLS), lambda b, i: (b, i, 0)),
                  pl.BlockSpec((bb, 1, A_COLS), lambda b, i: (b, 0, 0)), state,
                  vec(A_COLS), vec(MIX), vec(MIX), _const_spec(wwa.shape), _const_spec(gup.shape),
                  vec(MIX), vec(MIX), vec(MIX), vec(MIX), vec(MIX)],
        out_specs=[pl.BlockSpec((bb, tt, MIX), lambda b, i: (b, i, 0)), state],
        out_shape=[jax.ShapeDtypeStruct((bsz, t, MIX), BF16),
                   jax.ShapeDtypeStruct((bsz, N_PAIRS, LANES, LANES), F32)],
        scratch_shapes=[pltpu.VMEM((bb, tt + SUBLANES, A_COLS), F32),
                        pltpu.VMEM((bb * N_PAIRS, LANES, LANES), F32)],
        compiler_params=_params("arbitrary", "arbitrary"),
        name="rwkv",
    )(pa, shift0, st0, mu, w0, a0, wwa, gup, k_k, k_a, r_k, gn_w, gn_b)


def _gdn_kernel(pc_ref, pab_ref, conv0_ref, st0_ref, convw_ref, alog_ref, dtb_ref, nw_ref, eab_ref,
                o_ref, st_out_ref, ext_scr, st_scr, *, c):
    bb, tt, _ = pc_ref.shape
    rows, nch, n = bb * tt, tt // c, 2 * c
    w3 = 3 * MIX
    ci = pl.program_id(1)

    @pl.when(ci == 0)
    def _():
        ext_scr[:, 0:SUBLANES, :] = conv0_ref[...]
        st_scr[...] = st0_ref[...].reshape(bb * N_PAIRS, LANES, LANES)

    ext_scr[:, SUBLANES:SUBLANES + tt, :] = pc_ref[:, :, :w3]
    conv = ext_scr[:, SUBLANES:SUBLANES + tt, :] * convw_ref[CONV_W - 1:CONV_W, :]
    for s in range(1, CONV_W):
        conv = conv + ext_scr[:, SUBLANES - s:SUBLANES - s + tt, :] * convw_ref[CONV_W - 1 - s:CONV_W - s, :]
    ext_scr[:, 0:SUBLANES, :] = ext_scr[:, tt:tt + SUBLANES, :]
    qkv = _silu(conv.reshape(rows, w3))
    ones = _head_ones()
    l2n = lambda y: y * lax.rsqrt(_head_sum(y * y, ones) + RMS_EPS)
    q = l2n(qkv[:, :MIX]) * (HEAD_DIM ** -0.5)
    k = l2n(qkv[:, MIX:2 * MIX])
    v = qkv[:, 2 * MIX:]
    zgate = pc_ref[:, :, w3:].reshape(rows, MIX)
    ab = _dot_01(pab_ref[...].reshape(rows, LANES), eab_ref[...], 3)
    log_alpha = -jnp.exp(alog_ref[...]) * _softplus(ab[:, :MIX] + dtb_ref[...])
    beta = _sigmoid(ab[:, MIX:])
    gcum = _chunk_cumsum(log_alpha, c)
    eg = jnp.exp(gcum)
    bk = beta * k
    strict, incl = _tri_masks(c)

    g_t = _tiles(gcum, c)
    g_col = jnp.concatenate([g_t[:, :, :c], g_t[:, :, HEAD_DIM:HEAD_DIM + c]], axis=2)
    g_time = _transpose_tiles(g_t)
    g_row = jnp.concatenate([g_time[:, 0:1, :], g_time[:, HEAD_DIM:HEAD_DIM + 1, :]], axis=2)
    diff = g_col - g_row
    dec_s = jnp.where(strict, jnp.exp(jnp.where(strict, diff, 0.0)), 0.0)
    dec_i = jnp.where(incl, jnp.exp(jnp.where(incl, diff, 0.0)), 0.0)
    qg_g = _tiles(q * eg, c)
    kmat = _bmm_nt(jnp.concatenate([_tiles(bk, c), _tiles(q, c)], axis=1), _block_diag(_tiles(k, c)))
    attn = kmat[:, c:] * dec_i
    t_inv = _unit_lower_inverse(kmat[:, :c] * dec_s, c)
    sol = _bmm(t_inv, jnp.concatenate([_block_diag(_tiles(beta * v, c)), _block_diag(_tiles(bk * eg, c))],
                                      axis=2))
    kd_t = _transpose_tiles(_tiles(k * jnp.exp(_chunk_last(gcum, c) - gcum), c))
    decay = _column_scale(eg, c)

    st = st_scr[...]
    o_steps = []
    for ch in range(nch):
        now = lambda x: _step_tiles(x, ch, bb, nch)
        sol_c = now(sol)
        wq = _bmm(jnp.concatenate([sol_c[:, :, LANES:], now(qg_g)], axis=1), st)
        delta = sol_c[:, :, :LANES] - wq[:, :c]
        o_steps.append(wq[:, c:] + _bmm(now(attn), _block_diag(delta)))
        st = now(decay) * st + _keep_head_blocks(_bmm(now(kd_t), delta))
    st_scr[...] = st
    o = _untile(_tiles_from_steps(o_steps, bb, nch))

    o = o * lax.rsqrt(_head_sum(o * o, ones) * (1.0 / HEAD_DIM) + RMS_EPS) * nw_ref[...]
    o_ref[...] = (o * _silu(zgate)).astype(BF16).reshape(bb, tt, MIX)
    st_out_ref[...] = st.reshape(bb, N_PAIRS, LANES, LANES)


def _gdn(pc, pab, conv0, st0, conv_w, a_log, dt_bias, norm_w, eab, c):
    bsz, t, _ = pc.shape
    bb, tt = _mixer_tiling(bsz, t, c)
    vec = lambda w: _const_spec((1, w))
    state = pl.BlockSpec((bb, N_PAIRS, LANES, LANES), lambda b, i: (b, 0, 0, 0))
    return pl.pallas_call(
        functools.partial(_gdn_kernel, c=c),
        grid=(bsz // bb, t // tt),
        in_specs=[pl.BlockSpec((bb, tt, C_MAIN), lambda b, i: (b, i, 0)),
                  pl.BlockSpec((bb, tt, LANES), lambda b, i: (b, i, 0)),
                  pl.BlockSpec((bb, SUBLANES, 3 * MIX), lambda b, i: (b, 0, 0)), state,
                  _const_spec((CONV_W, 3 * MIX)), vec(MIX), vec(MIX), vec(MIX), _const_spec(eab.shape)],
        out_specs=[pl.BlockSpec((bb, tt, MIX), lambda b, i: (b, i, 0)), state],
        out_shape=[jax.ShapeDtypeStruct((bsz, t, MIX), BF16),
                   jax.ShapeDtypeStruct((bsz, N_PAIRS, LANES, LANES), F32)],
        scratch_shapes=[pltpu.VMEM((bb, tt + SUBLANES, 3 * MIX), F32),
                        pltpu.VMEM((bb * N_PAIRS, LANES, LANES), F32)],
        compiler_params=_params("arbitrary", "arbitrary"),
        name="gdn",
    )(pc, pab, conv0, st0, conv_w, a_log, dt_bias, norm_w, eab)


def _band_core(q, k_g, v_g, bias):
    bb, cq, _ = q.shape
    wn = k_g.shape[1]
    q_g = jnp.stack([_stack(q[b, :, j * LANES:(j + 1) * LANES]) for b in range(bb) for j in range(N_PAIRS)])
    s = _bmm_nt(q_g, k_g)
    s = (s.reshape(bb, N_PAIRS, 2 * cq, wn) + bias).reshape(bb * N_PAIRS, 2 * cq, wn)
    e = jnp.exp2(s - jnp.max(s, axis=-1, keepdims=True))
    pv = _bmm(e, v_g) / jnp.sum(e, axis=-1, keepdims=True)
    lane = _iota((cq, LANES), 1)
    return jnp.stack([
        jnp.concatenate([jnp.where(lane < HEAD_DIM, pv[b * N_PAIRS + j, :cq], pv[b * N_PAIRS + j, cq:])
                         for j in range(N_PAIRS)], axis=1) for b in range(bb)]).astype(BF16)


def _band_prompt_kernel(q_ref, k_ref, v_ref, bias_ref, o_ref):
    bb, cq, _ = q_ref.shape
    wn = BAND + cq
    first = pl.program_id(1) * cq - BAND
    start = pl.multiple_of(jnp.maximum(first, 0), LANES)
    shift = pl.multiple_of(start - first, LANES)
    window = lambda ref: jnp.stack([ref[b, pl.ds(start, wn), j * LANES:(j + 1) * LANES]
                                    for b in range(bb) for j in range(N_PAIRS)])
    o_ref[...] = _band_core(q_ref[...], window(k_ref), window(v_ref), bias_ref[:, :, pl.ds(shift, wn)])


def _band_prompt(q, k, v, table):
    bsz, t, _ = q.shape
    cq = BAND_QUERIES
    span = 2 * BAND + cq
    q_chunk = jnp.arange(cq)[:, None] // CHUNK
    k_chunk = jnp.arange(span)[None, :] // CHUNK
    in_band = (k_chunk >= q_chunk) & (k_chunk <= q_chunk + BAND // CHUNK)
    bias = jnp.where(in_band, _rel_bias(table, cq, span) * LOG2_E, -1e30).reshape(N_PAIRS, 2 * cq, span)
    bb = min(bsz, BAND_PROMPT_ROWS)
    full = pl.BlockSpec((bb, t, MIX), lambda b, i: (b, 0, 0))
    return pl.pallas_call(
        _band_prompt_kernel,
        grid=(bsz // bb, t // cq),
        in_specs=[pl.BlockSpec((bb, cq, MIX), lambda b, i: (b, i, 0)), full, full, _const_spec(bias.shape)],
        out_specs=pl.BlockSpec((bb, cq, MIX), lambda b, i: (b, i, 0)),
        out_shape=jax.ShapeDtypeStruct((bsz, t, MIX), BF16),
        compiler_params=_params("arbitrary", "arbitrary"),
        name="band_prompt",
    )(q, k, v, bias)


def _band_sample_kernel(q_ref, kp_ref, vp_ref, kn_ref, vn_ref, bias_ref, o_ref):
    bb = q_ref.shape[0]
    keys = lambda past, new: jnp.stack([
        jnp.concatenate([past[b, :, j * LANES:(j + 1) * LANES], new[b, :, j * LANES:(j + 1) * LANES]], axis=0)
        for b in range(bb) for j in range(N_PAIRS)]).astype(BF16)
    o_ref[...] = _band_core(q_ref[...], keys(kp_ref, kn_ref), keys(vp_ref, vn_ref), bias_ref[...])


def _band_sample(q, k, v, caches, table):
    k_past, v_past, layer = caches
    bsz, t, _ = q.shape
    past = k_past.shape[2]
    bb = min(bsz, BAND_SAMPLE_ROWS // t)
    bias = (_rel_bias(table, t, past + t) * LOG2_E).reshape(N_PAIRS, 2 * t, past + t)
    new = pl.BlockSpec((bb, t, MIX), lambda b: (b, 0, 0))
    old = pl.BlockSpec((None, bb, past, MIX), lambda b: (layer, b, 0, 0))
    return pl.pallas_call(
        _band_sample_kernel,
        grid=(bsz // bb,),
        in_specs=[new, old, old, new, new, _const_spec(bias.shape)],
        out_specs=new,
        out_shape=jax.ShapeDtypeStruct((bsz, t, MIX), BF16),
        compiler_params=_params("arbitrary"),
        name="band_sample",
    )(q, k_past, v_past, k, v, bias)


def _merge_kernel(x_ref, oa_ref, ob_ref, oc_ref, sh_ref, sc_ref, gm_ref, sh2_ref, sc2_ref, nw_ref, nw2_ref,
                  wgate_ref, bgate_ref, wbr_ref, wout_ref, wrt_ref, rb_ref,
                  x1_ref, h2_ref, comb_ref):
    x = x_ref[...]
    h = _norm_mod(x, nw_ref[...], sc_ref[0], sh_ref[0]).astype(BF16)
    gates = _sigmoid(_dot(h, wgate_ref[...]) + bgate_ref[...])
    mixed = None
    for i, o_ref in enumerate((oa_ref, ob_ref, oc_ref)):
        term = gates[:, i * D_MODEL:(i + 1) * D_MODEL] * _bdot(o_ref[...], wbr_ref[i])
        mixed = term if mixed is None else mixed + term
    x1 = x + gm_ref[0] * _bdot(mixed, wout_ref[...])
    x1_ref[...] = x1
    h2 = _norm_mod(x1, nw2_ref[...], sc2_ref[0], sh2_ref[0])
    h2_ref[...] = h2.astype(BF16)

    scores = _sigmoid(_dot_split(wrt_ref[...], h2, nt=True))
    sel = scores + rb_ref[...]
    tm = scores.shape[1]
    per = N_EXPERTS // N_GROUPS
    best_val, best = None, None
    for g in range(N_GROUPS):
        rows = [sel[g * per + i:g * per + i + 1, :] for i in range(per)]
        top2 = None
        for i in range(per):
            for i2 in range(i + 1, per):
                pair = rows[i] + rows[i2]
                top2 = pair if top2 is None else jnp.maximum(top2, pair)
        if g == 0:
            best_val, best = top2, jnp.zeros((1, tm), jnp.int32)
        else:
            better = top2 > best_val
            best = jnp.where(better, g, best)
            best_val = jnp.where(better, top2, best_val)
    eidx = _iota((N_EXPERTS, tm), 0)
    cand = jnp.where(eidx // per == best, sel, -jnp.inf)
    m1 = jnp.max(cand, axis=0, keepdims=True)
    i1 = jnp.min(jnp.where(cand == m1, eidx, N_EXPERTS), axis=0, keepdims=True)
    cand2 = jnp.where(eidx == i1, -jnp.inf, cand)
    m2 = jnp.max(cand2, axis=0, keepdims=True)
    i2 = jnp.min(jnp.where(cand2 == m2, eidx, N_EXPERTS), axis=0, keepdims=True)
    w1 = jnp.sum(jnp.where(eidx == i1, scores, 0.0), axis=0, keepdims=True)
    w2 = jnp.sum(jnp.where(eidx == i2, scores, 0.0), axis=0, keepdims=True)
    den = w1 + w2
    comb_ref[...] = jnp.where(eidx == i1, w1 / den, 0.0) + jnp.where(eidx == i2, w2 / den, 0.0)


def _merge(x, oa, ob, oc, mods, norm_w, norm2_w, w_gate, b_gate, w_branch, w_out, w_router_t, router_bias,
           tm, tiles_per_group):
    n = x.shape[0]
    row = lambda w: pl.BlockSpec((tm, w), lambda i: (i, 0))
    mod_specs = [_mod_spec(m, tm, tiles_per_group) for m in mods]
    return pl.pallas_call(
        _merge_kernel,
        grid=(n // tm,),
        in_specs=[row(D_MODEL), row(MIX), row(MIX), row(MIX)] + mod_specs
                 + [_const_spec((1, D_MODEL)), _const_spec((1, D_MODEL)), _const_spec(w_gate.shape),
                    _const_spec(b_gate.shape), _const_spec(w_branch.shape), _const_spec(w_out.shape),
                    _const_spec(w_router_t.shape), _const_spec(router_bias.shape)],
        out_specs=[row(D_MODEL), row(D_MODEL), pl.BlockSpec((N_EXPERTS, tm), lambda i: (0, i))],
        out_shape=[jax.ShapeDtypeStruct((n, D_MODEL), F32), jax.ShapeDtypeStruct((n, D_MODEL), BF16),
                   jax.ShapeDtypeStruct((N_EXPERTS, n), F32)],
        compiler_params=_params("arbitrary"),
        name="merge",
    )(x, oa, ob, oc, *mods, norm_w, norm2_w, w_gate, b_gate, w_branch, w_out, w_router_t, router_bias)


def _moe_kernel(x1_ref, h2_ref, comb_ref, gf_ref, wg_ref, wu_ref, wd_ref, o_ref):
    h2 = h2_ref[...]
    comb = jnp.transpose(comb_ref[...])
    per = N_EXPERTS // N_GROUPS
    acc = None
    for g in range(N_GROUPS):
        hid = [(_silu(_dot(h2, wg_ref[e])) * _dot(h2, wu_ref[e]) * comb[:, e:e + 1]).astype(BF16)
               for e in range(g * per, (g + 1) * per)]
        term = _dot(jnp.concatenate(hid, axis=1), wd_ref[g])
        acc = term if acc is None else acc + term
    o_ref[...] = x1_ref[...] + gf_ref[0] * acc


def _moe(x1, h2, comb, g_ffn, wg, wu, wd, tm, tiles_per_group):
    n = x1.shape[0]
    row = lambda w: pl.BlockSpec((tm, w), lambda i: (i, 0))
    return pl.pallas_call(
        _moe_kernel,
        grid=(n // tm,),
        in_specs=[row(D_MODEL), row(D_MODEL), pl.BlockSpec((N_EXPERTS, tm), lambda i: (0, i)),
                  _mod_spec(g_ffn, tm, tiles_per_group),
                  _const_spec(wg.shape), _const_spec(wu.shape), _const_spec(wd.shape)],
        out_specs=row(D_MODEL),
        out_shape=jax.ShapeDtypeStruct((n, D_MODEL), F32),
        compiler_params=_params("arbitrary"),
        name="moe",
    )(x1, h2, comb, g_ffn, wg, wu, wd)


def _pair_state(s):
    bsz = s.shape[0]
    s = s.reshape(bsz, N_PAIRS, 2, HEAD_DIM, HEAD_DIM)
    z = jnp.zeros_like(s[:, :, 0])
    top = jnp.concatenate([s[:, :, 0], z], axis=-1)
    bot = jnp.concatenate([z, s[:, :, 1]], axis=-1)
    return jnp.concatenate([top, bot], axis=-2)


def _unpair_state(s):
    bsz = s.shape[0]
    return jnp.stack([s[:, :, :HEAD_DIM, :HEAD_DIM], s[:, :, HEAD_DIM:, HEAD_DIM:]], axis=2).reshape(
        bsz, N_HEADS, HEAD_DIM, HEAD_DIM)


def _rel_bias(table, cq, wn):
    span = cq - 1 + wn
    offs = jnp.arange(-(cq - 1), wn + 1)
    diag = table[:, jnp.clip(BAND - offs, -MAX_REL, MAX_REL) + MAX_REL].astype(F32)
    skew = jnp.tile(diag, (1, cq))[:, :cq * span].reshape(-1, cq, span)
    return skew[:, :, cq - 1:]


def _layer(x, mod, state, wts, tm):
    bsz, t, _ = x.shape
    n = bsz * t
    c = min(CHUNK, t)
    shift0, wkv0, caches, conv0, s0 = state
    if t % tm == 0:
        tiles_per_group = t // tm
        mods = [m.reshape(bsz, 1, D_MODEL) for m in jnp.split(mod, 6, axis=-1)]
    else:
        tiles_per_group = 1
        mods = [jnp.repeat(m, t, axis=0).reshape(n // tm, tm, D_MODEL) for m in jnp.split(mod, 6, axis=-1)]
    sh_mix, sc_mix, g_mix, sh_ffn, sc_ffn, g_ffn = mods
    xf = x.reshape(n, D_MODEL)

    prompt = caches is None
    kv_tail_only = prompt and tm == BAND and t % tm == 0
    pa, q, k, v, kb, vb, pc, pab = _inproj(xf, sh_mix, sc_mix, wts["norm_mix_w"], wts["wa"], wts["wb"], wts["wc"],
                                           wts["wab"], wts["q_norm"], wts["k_norm"], tm, tiles_per_group,
                                           kv_tail_only)
    pa = pa.reshape(bsz, t, A_COLS)
    q, kb, vb = (z.reshape(bsz, t, MIX) for z in (q, kb, vb))
    k, v = (z.reshape(bsz, -1, MIX) for z in (k, v))
    pc = pc.reshape(bsz, t, C_MAIN)

    oa, wkv = _rwkv(pa, shift0.reshape(bsz, 1, A_COLS), _pair_state(jnp.swapaxes(wkv0, -1, -2)),
                    wts["mu"], wts["w0"], wts["a0"], wts["wwa"], wts["g_up"], wts["k_k"], wts["k_a"], wts["r_k"],
                    wts["gn_w"], wts["gn_b"], c)
    new_wkv = jnp.swapaxes(_unpair_state(wkv), -1, -2)

    if prompt:
        ob = _band_prompt(q, kb, vb, wts["rel_bias"])
        new_k, new_v = k[:, -BAND:], v[:, -BAND:]
    else:
        ob = _band_sample(q, k, v, caches, wts["rel_bias"])
        new_k, new_v = k, v

    conv_pad = jnp.pad(conv0, ((0, 0), (SUBLANES - (CONV_W - 1), 0), (0, 0)))
    oc, s_new = _gdn(pc, pab.reshape(bsz, t, LANES), conv_pad, _pair_state(s0), wts["conv_w"], wts["a_log"],
                     wts["dt_bias"], wts["gdn_norm_w"], wts["eab"], c)
    tail = min(t, CONV_W - 1)
    new_conv = jnp.concatenate([conv0, pc[:, t - tail:, :3 * MIX]], axis=1)[:, -(CONV_W - 1):]

    x1, h2, comb = _merge(xf, oa.reshape(n, MIX), ob.reshape(n, MIX), oc.reshape(n, MIX),
                          [sh_mix, sc_mix, g_mix, sh_ffn, sc_ffn], wts["norm_mix_w"], wts["norm_ffn_w"],
                          wts["w_gate"], wts["b_gate"], wts["w_branch"], wts["w_out"], wts["w_router_t"],
                          wts["router_bias"], tm, tiles_per_group)
    x2 = _moe(x1, h2, comb, g_ffn, wts["wg"], wts["wu"], wts["wd"], tm, tiles_per_group)
    heads = lambda z: z.reshape(bsz, -1, N_HEADS, HEAD_DIM)
    return x2.reshape(bsz, t, D_MODEL), (pa[:, -1], new_wkv, heads(new_k), heads(new_v), new_conv,
                                         _unpair_state(s_new))


def _prepare_layer(l, w_in, norm_mix_w, norm_ffn_w, rwkv_mu, rwkv_w0, rwkv_w_up, rwkv_a0, rwkv_a_up, rwkv_g_up,
                   rwkv_k_k, rwkv_k_a, rwkv_r_k, rwkv_gn_w, rwkv_gn_b, band_q_norm, band_k_norm, band_rel_bias,
                   gdn_conv_w, gdn_a_log, gdn_dt_bias, gdn_norm_w, w_branch, w_gate, b_gate, w_out,
                   w_router, router_bias, w_exp_gate, w_exp_up, w_exp_down):
    row = lambda z: z.reshape(1, -1).astype(F32)
    per_head = lambda z: jnp.repeat(z, HEAD_DIM).reshape(1, MIX)
    win = w_in[l].astype(BF16)
    c0 = A_COLS + B_COLS
    wab = jnp.pad(win[:, c0 + C_MAIN:], ((0, 0), (0, LANES - 2 * N_HEADS)))
    zeros = jnp.zeros((HEAD_DIM, MIX), F32)
    wwa = jnp.concatenate([jnp.concatenate([rwkv_w_up[l], zeros], axis=1),
                           jnp.concatenate([zeros, rwkv_a_up[l]], axis=1)], axis=0)
    head_of_lane = jnp.arange(MIX) // HEAD_DIM
    src = jnp.arange(LANES)[:, None]
    eab = jnp.concatenate([(src == head_of_lane[None, :]), (src == N_HEADS + head_of_lane[None, :])],
                          axis=1).astype(BF16)
    return dict(
        norm_mix_w=row(norm_mix_w[l]), norm_ffn_w=row(norm_ffn_w[l]),
        wa=win[:, :A_COLS], wb=win[:, A_COLS:c0], wc=win[:, c0:c0 + C_MAIN], wab=wab,
        q_norm=row(jnp.tile(band_q_norm[l], N_HEADS)), k_norm=row(jnp.tile(band_k_norm[l], N_HEADS)),
        mu=row(rwkv_mu[l]), w0=row(rwkv_w0[l]), a0=row(rwkv_a0[l]), wwa=wwa, g_up=rwkv_g_up[l],
        k_k=row(rwkv_k_k[l]), k_a=row(rwkv_k_a[l]), r_k=row(rwkv_r_k[l]), gn_w=row(rwkv_gn_w[l]),
        gn_b=row(rwkv_gn_b[l]), rel_bias=band_rel_bias[l],
        conv_w=gdn_conv_w[l], a_log=per_head(gdn_a_log[l]), dt_bias=per_head(gdn_dt_bias[l]),
        gdn_norm_w=row(jnp.tile(gdn_norm_w[l], N_HEADS)), eab=eab,
        w_gate=w_gate[l].astype(BF16), b_gate=row(b_gate[l]), w_branch=w_branch[l].astype(BF16),
        w_out=w_out[l].astype(BF16), w_router_t=jnp.transpose(w_router), router_bias=router_bias.reshape(-1, 1),
        wg=w_exp_gate[l].astype(BF16), wu=w_exp_up[l].astype(BF16),
        wd=w_exp_down[l].astype(BF16).reshape(N_GROUPS, -1, D_MODEL),
    )


def kernel(x_prompt, x_sample, c_prompt, c_sample, state_rwkv_shift, state_rwkv_wkv, cache_band_k, cache_band_v, state_gdn_conv, state_gdn_S, w_ada, b_ada, norm_mix_w, norm_ffn_w, w_in, rwkv_mu, rwkv_w0, rwkv_w_up, rwkv_a0, rwkv_a_up, rwkv_g_up, rwkv_k_k, rwkv_k_a, rwkv_r_k, rwkv_gn_w, rwkv_gn_b, band_q_norm, band_k_norm, band_rel_bias, gdn_conv_w, gdn_a_log, gdn_dt_bias, gdn_norm_w, w_branch, w_gate, b_gate, w_out, w_router, router_bias, w_exp_gate, w_exp_up, w_exp_down):
    depth = w_ada.shape[0]
    bsz = x_prompt.shape[0]
    mod = _adaln(jnp.concatenate([c_prompt, c_sample], axis=0), w_ada, b_ada)
    mod_p, mod_s = mod[:, :bsz], mod[:, bsz:]
    zero_state = (jnp.zeros((bsz, A_COLS), F32), jnp.zeros((bsz, N_HEADS, HEAD_DIM, HEAD_DIM), F32), None,
                  jnp.zeros((bsz, CONV_W - 1, 3 * MIX), F32), jnp.zeros((bsz, N_HEADS, HEAD_DIM, HEAD_DIM), F32))
    k_cache = cache_band_k.reshape(cache_band_k.shape[:3] + (MIX,))
    v_cache = cache_band_v.reshape(cache_band_v.shape[:3] + (MIX,))
    tm = TOKEN_TILE
    xp, xs = x_prompt, x_sample
    new_p, new_s = [], []
    for l in range(depth):
        wts = _prepare_layer(l, w_in, norm_mix_w, norm_ffn_w, rwkv_mu, rwkv_w0, rwkv_w_up, rwkv_a0, rwkv_a_up,
                             rwkv_g_up, rwkv_k_k, rwkv_k_a, rwkv_r_k, rwkv_gn_w, rwkv_gn_b, band_q_norm,
                             band_k_norm, band_rel_bias, gdn_conv_w, gdn_a_log, gdn_dt_bias, gdn_norm_w,
                             w_branch, w_gate, b_gate, w_out, w_router, router_bias, w_exp_gate, w_exp_up,
                             w_exp_down)
        xp, st_p = _layer(xp, mod_p[l], zero_state, wts, tm)
        xs, st_s = _layer(xs, mod_s[l], (state_rwkv_shift[l], state_rwkv_wkv[l], (k_cache, v_cache, l),
                                         state_gdn_conv[l], state_gdn_S[l]), wts, tm)
        new_p.append(st_p)
        new_s.append(st_s)
    p_out = [jnp.stack(z, axis=0) for z in zip(*new_p)]
    s_out = [jnp.stack(z, axis=0) for z in zip(*new_s)]
    return (xp, xs, *p_out, *s_out)
```

```python
import functools
import math

import jax
import jax.numpy as jnp
from jax import lax
from jax.experimental import pallas as pl
from jax.experimental.pallas import tpu as pltpu

F32 = jnp.float32
BF16 = jnp.bfloat16

D_MODEL = 1024
MIX = 512
HEAD_DIM = 64
N_HEADS = MIX // HEAD_DIM
LANES = 128
N_PAIRS = MIX // LANES
SUBLANES = 8
CHUNK = 64
TOKEN_TILE = 512
MIXER_ROWS = 512
MIXER_SPAN = 256
CUMSUM_ROWS = 256
BAND_SAMPLE_ROWS = 256
BAND_QUERIES = 2 * CHUNK
BAND_PROMPT_ROWS = 2
ADALN_TILE = 768
A_LORA_W, A_LORA_A, A_LORA_G = 64, 64, 128
A_COLS = 3 * MIX + A_LORA_W + A_LORA_A + A_LORA_G
B_COLS = 3 * MIX
C_MAIN = 4 * MIX
CONV_W = 4
BAND = 8 * CHUNK
MAX_REL = 2 * CHUNK
N_EXPERTS = 16
N_GROUPS = 4
RMS_EPS = 1e-6
A_GN_EPS = 64e-5
A_DECAY_SCALE = math.exp(-0.5)
LOG2_E = math.log2(math.e)
SCORE_SCALE = HEAD_DIM ** -0.5 * LOG2_E
VMEM_LIMIT_BYTES = 56 * 1024 * 1024


def _params(*sem):
    return pltpu.CompilerParams(dimension_semantics=sem, vmem_limit_bytes=VMEM_LIMIT_BYTES)


def _dot(a, b):
    return lax.dot_general(a, b, (((1,), (0,)), ((), ())), preferred_element_type=F32)


def _dot_nt(a, b):
    return lax.dot_general(a, b, (((1,), (1,)), ((), ())), preferred_element_type=F32)


def _bdot(a, b):
    return _dot(a.astype(BF16), b.astype(BF16))


def _dot_split(a, b, nt=False):
    dot = _dot_nt if nt else _dot
    a_hi, a_lo = _bf16_terms(a, 2)
    b_hi, b_lo = _bf16_terms(b, 2)
    return dot(a_hi, b_hi) + (dot(a_hi, b_lo) + dot(a_lo, b_hi))


def _bmm(a, b):
    return lax.dot_general(a.astype(BF16), b.astype(BF16), (((2,), (1,)), ((0,), (0,))),
                           preferred_element_type=F32)


def _bmm_nt(a, b):
    return lax.dot_general(a.astype(BF16), b.astype(BF16), (((2,), (2,)), ((0,), (0,))),
                           preferred_element_type=F32)


def _sigmoid(x):
    return 0.5 * jnp.tanh(0.5 * x) + 0.5


def _silu(x):
    return x * _sigmoid(x)


def _softplus(x):
    return jnp.maximum(x, 0.0) + jnp.log1p(jnp.exp(-jnp.abs(x)))


def _iota(shape, dim):
    return lax.broadcasted_iota(jnp.int32, shape, dim)


def _bf16_terms(x, terms):
    parts = []
    for _ in range(terms):
        part = x.astype(BF16)
        parts.append(part)
        x = x - part.astype(F32)
    return parts


def _dot_01(x, m, terms):
    out = None
    for part in _bf16_terms(x, terms):
        term = _dot(part, m)
        out = term if out is None else out + term
    return out


def _dot_01_left(m, x, terms):
    out = None
    for part in _bf16_terms(x, terms):
        term = _dot(m, part)
        out = term if out is None else out + term
    return out


def _head_ones():
    return (_iota((LANES, LANES), 0) // HEAD_DIM == _iota((LANES, LANES), 1) // HEAD_DIM).astype(BF16)


def _head_sum(x, ones):
    parts = [_dot_01(x[:, g * LANES:(g + 1) * LANES], ones, 2) for g in range(x.shape[1] // LANES)]
    return parts[0] if len(parts) == 1 else jnp.concatenate(parts, axis=1)


def _stack(x):
    lane = _iota(x.shape, 1)
    return jnp.concatenate([jnp.where(lane < HEAD_DIM, x, 0.0), jnp.where(lane >= HEAD_DIM, x, 0.0)], axis=0)


def _tiles(x, c):
    return jnp.stack([x[u * c:(u + 1) * c, j * LANES:(j + 1) * LANES]
                      for u in range(x.shape[0] // c) for j in range(N_PAIRS)])


def _untile(xs):
    return jnp.concatenate([jnp.concatenate([xs[u * N_PAIRS + j] for j in range(N_PAIRS)], axis=1)
                            for u in range(xs.shape[0] // N_PAIRS)], axis=0)


def _step_tiles(x, ch, bb, nch):
    parts = [x[(b * nch + ch) * N_PAIRS:(b * nch + ch + 1) * N_PAIRS] for b in range(bb)]
    return parts[0] if bb == 1 else jnp.concatenate(parts, axis=0)


def _tiles_from_steps(steps, bb, nch):
    parts = [steps[ch][b * N_PAIRS:(b + 1) * N_PAIRS] for b in range(bb) for ch in range(nch)]
    return parts[0] if len(parts) == 1 else jnp.concatenate(parts, axis=0)


def _chunk_cumsum(x, c):
    seg = min(x.shape[0], CUMSUM_ROWS)
    tril = _chunk_tril(seg, c)
    parts = [_dot_01_left(tril, x[s:s + seg], 3) for s in range(0, x.shape[0], seg)]
    return parts[0] if len(parts) == 1 else jnp.concatenate(parts, axis=0)


def _block_diag(x):
    x = x.astype(BF16)
    first = _iota(x.shape[1:], 1) < x.shape[2] // 2
    zero = jnp.zeros_like(x)
    return jnp.concatenate([jnp.where(first, x, zero), jnp.where(first, zero, x)], axis=1)


def _keep_head_blocks(x):
    same = _iota(x.shape[1:], 0) // HEAD_DIM == _iota(x.shape[1:], 1) // HEAD_DIM
    return jnp.where(same, x, 0.0)


def _transpose_tiles(xs):
    return jnp.stack([jnp.transpose(xs[g]) for g in range(xs.shape[0])])


def _chunk_last(x, c):
    return jnp.concatenate([jnp.broadcast_to(x[(u + 1) * c - 1:(u + 1) * c], (c, x.shape[1]))
                            for u in range(x.shape[0] // c)], axis=0)


def _column_scale(row_values, c):
    tiles = []
    for u in range(row_values.shape[0] // c):
        last = row_values[(u + 1) * c - 1:(u + 1) * c]
        for j in range(N_PAIRS):
            tiles.append(jnp.transpose(jnp.broadcast_to(last[:, j * LANES:(j + 1) * LANES], (LANES, LANES))))
    return jnp.stack(tiles)


def _chunk_tril(rows, c):
    row, col = _iota((rows, rows), 0), _iota((rows, rows), 1)
    return ((col <= row) & (row // c == col // c)).astype(BF16)


def _tri_masks(c):
    row, col = _iota((c, 2 * c), 0), _iota((c, 2 * c), 1) % c
    return col < row, col <= row


def _unit_lower_inverse(a, c):
    row, col = _iota((c, 2 * c), 0), _iota((c, 2 * c), 1) % c
    eye = (row == col).astype(F32)
    same = lambda s: row // s == col // s
    a0 = jnp.where(same(SUBLANES), a, 0.0)
    a2 = _bmm(a0, _block_diag(a0))
    a4 = _bmm(a2, _block_diag(a2))
    x = _bmm(_bmm(eye - a0, _block_diag(eye + a2)), _block_diag(eye + a4))
    s = SUBLANES
    while s < c:
        off = jnp.where(same(2 * s) & jnp.logical_not(same(s)), a, 0.0)
        x = x - _bmm(_bmm(x, _block_diag(off)), _block_diag(x))
        s *= 2
    return x


def _adaln_kernel(c_ref, w_ref, b_ref, o_ref):
    o_ref[0] = _dot_split(_silu(c_ref[...]), w_ref[0]) + b_ref[0]


def _adaln(c, w_ada, b_ada):
    depth, bsz, tn = w_ada.shape[0], c.shape[0], ADALN_TILE
    return pl.pallas_call(
        _adaln_kernel,
        grid=(depth, 6 * D_MODEL // tn),
        in_specs=[pl.BlockSpec((bsz, D_MODEL), lambda l, j: (0, 0)),
                  pl.BlockSpec((1, D_MODEL, tn), lambda l, j: (l, 0, j)),
                  pl.BlockSpec((1, 1, tn), lambda l, j: (l, 0, j))],
        out_specs=pl.BlockSpec((1, bsz, tn), lambda l, j: (l, 0, j)),
        out_shape=jax.ShapeDtypeStruct((depth, bsz, 6 * D_MODEL), F32),
        compiler_params=_params("arbitrary", "arbitrary"),
        name="adaln",
    )(c, w_ada, b_ada.reshape(depth, 1, 6 * D_MODEL))


def _norm_mod(x, norm_w, scale, shift):
    y = x * lax.rsqrt(jnp.mean(x * x, axis=-1, keepdims=True) + RMS_EPS)
    return y * norm_w * (1.0 + scale) + shift


def _inproj_kernel(x_ref, sh_ref, sc_ref, nw_ref, wa_ref, wb_ref, wc_ref, wab_ref, qn_ref, kn_ref,
                   pa_ref, q_ref, k_ref, v_ref, kb_ref, vb_ref, pc_ref, pab_ref):
    h = _norm_mod(x_ref[...], nw_ref[...], sc_ref[0], sh_ref[0]).astype(BF16)
    pa_ref[...] = _dot(h, wa_ref[...])
    pb = _dot(h, wb_ref[...])
    ones = _head_ones()

    def head_rms(y, w):
        return y * lax.rsqrt(_head_sum(y * y, ones) * (1.0 / HEAD_DIM) + RMS_EPS) * w

    q_ref[...] = (head_rms(pb[:, :MIX], qn_ref[...]) * SCORE_SCALE).astype(BF16)
    k = head_rms(pb[:, MIX:2 * MIX], kn_ref[...])
    v = pb[:, 2 * MIX:]
    k_ref[...], v_ref[...] = k, v
    kb_ref[...], vb_ref[...] = k.astype(BF16), v.astype(BF16)
    pc_ref[...] = _dot(h, wc_ref[...])
    pab_ref[...] = _dot(h, wab_ref[...])


def _mod_spec(mod, tm, tiles_per_group):
    rows = mod.shape[1]
    return pl.BlockSpec((1, rows, D_MODEL), lambda i: (i // tiles_per_group, 0, 0))


def _const_spec(shape):
    zeros = (0,) * len(shape)
    return pl.BlockSpec(shape, lambda *_: zeros, pipeline_mode=pl.Buffered(1))


def _inproj(x, shift, scale, norm_w, wa, wb, wc, wab, q_norm, k_norm, tm, tiles_per_group, kv_tail_only):
    n = x.shape[0]
    row = lambda w: pl.BlockSpec((tm, w), lambda i: (i, 0))
    if kv_tail_only:
        kv_rows, kv = n // tiles_per_group, pl.BlockSpec((tm, MIX), lambda i: (i // tiles_per_group, 0))
    else:
        kv_rows, kv = n, row(MIX)
    shape = lambda rows, w, dtype: jax.ShapeDtypeStruct((rows, w), dtype)
    return pl.pallas_call(
        _inproj_kernel,
        grid=(n // tm,),
        in_specs=[row(D_MODEL), _mod_spec(shift, tm, tiles_per_group), _mod_spec(scale, tm, tiles_per_group),
                  _const_spec((1, D_MODEL)), _const_spec(wa.shape), _const_spec(wb.shape), _const_spec(wc.shape),
                  _const_spec(wab.shape), _const_spec((1, MIX)), _const_spec((1, MIX))],
        out_specs=[row(A_COLS), row(MIX), kv, kv, row(MIX), row(MIX), row(C_MAIN), row(LANES)],
        out_shape=[shape(n, A_COLS, F32), shape(n, MIX, BF16), shape(kv_rows, MIX, F32), shape(kv_rows, MIX, F32),
                   shape(n, MIX, BF16), shape(n, MIX, BF16), shape(n, C_MAIN, F32), shape(n, LANES, F32)],
        compiler_params=_params("arbitrary"),
        name="inproj",
    )(x, shift, scale, norm_w, wa, wb, wc, wab, q_norm, k_norm)


def _rwkv_kernel(p_ref, shift0_ref, st0_ref, mu_ref, w0_ref, a0_ref, wwa_ref, gup_ref, kk_ref, ka_ref, rk_ref,
                 gnw_ref, gnb_ref, o_ref, st_out_ref, ext_scr, st_scr, *, c):
    bb, tt, _ = p_ref.shape
    rows, nch, n = bb * tt, tt // c, 2 * c
    ci = pl.program_id(1)

    @pl.when(ci == 0)
    def _():
        ext_scr[:, SUBLANES - 1:SUBLANES, :] = shift0_ref[...]
        st_scr[...] = st0_ref[...].reshape(bb * N_PAIRS, LANES, LANES)

    ext_scr[:, SUBLANES:SUBLANES + tt, :] = p_ref[...]
    p = p_ref[...].reshape(rows, A_COLS)
    p_prev = ext_scr[:, SUBLANES - 1:SUBLANES - 1 + tt, :].reshape(rows, A_COLS)
    ext_scr[:, 0:SUBLANES, :] = ext_scr[:, tt:tt + SUBLANES, :]
    xs = p + (p_prev - p) * mu_ref[...]
    r, k, v = xs[:, :MIX], xs[:, MIX:2 * MIX], xs[:, 2 * MIX:3 * MIX]
    lora_in = xs[:, 3 * MIX:3 * MIX + LANES]
    gd = xs[:, 3 * MIX + LANES:]
    lane = _iota((1, LANES), 1)
    lora = _bdot(jnp.where(lane < HEAD_DIM, jnp.tanh(lora_in), lora_in), wwa_ref[...])
    log_w = -A_DECAY_SCALE * _sigmoid(w0_ref[...] + lora[:, :MIX])
    a = _sigmoid(a0_ref[...] + lora[:, MIX:])
    g = _bdot(_sigmoid(gd), gup_ref[...])
    ones = _head_ones()
    kk_raw = k * kk_ref[...]
    kk = kk_raw * lax.rsqrt(_head_sum(kk_raw * kk_raw, ones) + RMS_EPS)
    k = k * (1.0 + (a - 1.0) * ka_ref[...])
    b = kk * a
    cum = _chunk_cumsum(log_w, c)
    cum_last = _chunk_last(cum, c)
    e_cum, e_neg = jnp.exp(cum), jnp.exp(-cum)
    e_prev, e_tail = jnp.exp(cum - log_w), jnp.exp(cum_last - cum)
    strict, incl = _tri_masks(c)

    kk_g, r_g, v_g = _tiles(kk * e_prev, c), _tiles(r * e_cum, c), _tiles(v, c)
    amat = _bmm_nt(jnp.concatenate([kk_g, r_g], axis=1),
                   jnp.concatenate([_block_diag(_tiles(k * e_neg, c)), _block_diag(_tiles(b * e_neg, c))],
                                   axis=1))
    a_kk = jnp.where(strict, amat[:, :c, :n], 0.0)
    a_bk = jnp.where(strict, amat[:, :c, n:], 0.0)
    a_rk = jnp.where(incl, amat[:, c:, :n], 0.0)
    a_rb = jnp.where(incl, amat[:, c:, n:], 0.0)
    t_inv = _unit_lower_inverse(a_bk, c)
    av = _bmm(jnp.concatenate([a_kk, a_rk], axis=1), _block_diag(v_g))
    wu = _bmm(t_inv, jnp.concatenate([_block_diag(kk_g), _block_diag(av[:, :c])], axis=2))
    kd_v = _keep_head_blocks(_bmm(_transpose_tiles(_tiles(k * e_tail, c)), v_g))
    bd_t = _transpose_tiles(_tiles(b * e_tail, c))
    decay = _column_scale(e_cum, c)

    st = st_scr[...]
    o_steps = []
    for ch in range(nch):
        now = lambda x: _step_tiles(x, ch, bb, nch)
        wu_c = now(wu)
        zo = _bmm(jnp.concatenate([wu_c[:, :, :LANES], now(r_g)], axis=1), st)
        z = wu_c[:, :, LANES:] + zo[:, :c]
        o_steps.append(zo[:, c:] + now(av)[:, c:] - _bmm(now(a_rb), _block_diag(z)))
        st = now(decay) * st + now(kd_v) - _keep_head_blocks(_bmm(now(bd_t), z))
    st_scr[...] = st
    o = _untile(_tiles_from_steps(o_steps, bb, nch))

    mean = _head_sum(o, ones) * (1.0 / HEAD_DIM)
    cen = o - mean
    var = _head_sum(cen * cen, ones) * (1.0 / HEAD_DIM)
    o = cen * lax.rsqrt(var + A_GN_EPS) * gnw_ref[...] + gnb_ref[...]
    bonus = _head_sum(r * k * rk_ref[...], ones) * v
    o_ref[...] = ((o + bonus) * g).astype(BF16).reshape(bb, tt, MIX)
    st_out_ref[...] = st.reshape(bb, N_PAIRS, LANES, LANES)


def _mixer_tiling(bsz, t, c):
    tt = min(t, MIXER_SPAN)
    return min(bsz, MIXER_ROWS // tt), tt


def _rwkv(pa, shift0, st0, mu, w0, a0, wwa, gup, k_k, k_a, r_k, gn_w, gn_b, c):
    bsz, t, _ = pa.shape
    bb, tt = _mixer_tiling(bsz, t, c)
    vec = lambda w: _const_spec((1, w))
    state = pl.BlockSpec((bb, N_PAIRS, LANES, LANES), lambda b, i: (b, 0, 0, 0))
    return pl.pallas_call(
        functools.partial(_rwkv_kernel, c=c),
        grid=(bsz // bb, t // tt),
        in_specs=[pl.BlockSpec((bb, tt, A_COLS), lambda b, i: (b, i, 0)),
                  pl.BlockSpec((bb, 1, A_COLS), lambda b, i: (b, 0, 0)), state,
                  vec(A_COLS), vec(MIX), vec(MIX), _const_spec(wwa.shape), _const_spec(gup.shape),
                  vec(MIX), vec(MIX), vec(MIX), vec(MIX), vec(MIX)],
        out_specs=[pl.BlockSpec((bb, tt, MIX), lambda b, i: (b, i, 0)), state],
        out_shape=[jax.ShapeDtypeStruct((bsz, t, MIX), BF16),
                   jax.ShapeDtypeStruct((bsz, N_PAIRS, LANES, LANES), F32)],
        scratch_shapes=[pltpu.VMEM((bb, tt + SUBLANES, A_COLS), F32),
                        pltpu.VMEM((bb * N_PAIRS, LANES, LANES), F32)],
        compiler_params=_params("arbitrary", "arbitrary"),
        name="rwkv",
    )(pa, shift0, st0, mu, w0, a0, wwa, gup, k_k, k_a, r_k, gn_w, gn_b)


def _gdn_kernel(pc_ref, pab_ref, conv0_ref, st0_ref, convw_ref, alog_ref, dtb_ref, nw_ref, eab_ref,
                o_ref, st_out_ref, ext_scr, st_scr, *, c):
    bb, tt, _ = pc_ref.shape
    rows, nch, n = bb * tt, tt // c, 2 * c
    w3 = 3 * MIX
    ci = pl.program_id(1)

    @pl.when(ci == 0)
    def _():
        ext_scr[:, 0:SUBLANES, :] = conv0_ref[...]
        st_scr[...] = st0_ref[...].reshape(bb * N_PAIRS, LANES, LANES)

    ext_scr[:, SUBLANES:SUBLANES + tt, :] = pc_ref[:, :, :w3]
    conv = ext_scr[:, SUBLANES:SUBLANES + tt, :] * convw_ref[CONV_W - 1:CONV_W, :]
    for s in range(1, CONV_W):
        conv = conv + ext_scr[:, SUBLANES - s:SUBLANES - s + tt, :] * convw_ref[CONV_W - 1 - s:CONV_W - s, :]
    ext_scr[:, 0:SUBLANES, :] = ext_scr[:, tt:tt + SUBLANES, :]
    qkv = _silu(conv.reshape(rows, w3))
    ones = _head_ones()
    l2n = lambda y: y * lax.rsqrt(_head_sum(y * y, ones) + RMS_EPS)
    q = l2n(qkv[:, :MIX]) * (HEAD_DIM ** -0.5)
    k = l2n(qkv[:, MIX:2 * MIX])
    v = qkv[:, 2 * MIX:]
    zgate = pc_ref[:, :, w3:].reshape(rows, MIX)
    ab = _dot_01(pab_ref[...].reshape(rows, LANES), eab_ref[...], 3)
    log_alpha = -jnp.exp(alog_ref[...]) * _softplus(ab[:, :MIX] + dtb_ref[...])
    beta = _sigmoid(ab[:, MIX:])
    gcum = _chunk_cumsum(log_alpha, c)
    eg = jnp.exp(gcum)
    bk = beta * k
    strict, incl = _tri_masks(c)

    g_t = _tiles(gcum, c)
    g_col = jnp.concatenate([g_t[:, :, :c], g_t[:, :, HEAD_DIM:HEAD_DIM + c]], axis=2)
    g_time = _transpose_tiles(g_t)
    g_row = jnp.concatenate([g_time[:, 0:1, :], g_time[:, HEAD_DIM:HEAD_DIM + 1, :]], axis=2)
    diff = g_col - g_row
    dec_s = jnp.where(strict, jnp.exp(jnp.where(strict, diff, 0.0)), 0.0)
    dec_i = jnp.where(incl, jnp.exp(jnp.where(incl, diff, 0.0)), 0.0)
    qg_g = _tiles(q * eg, c)
    kmat = _bmm_nt(jnp.concatenate([_tiles(bk, c), _tiles(q, c)], axis=1), _block_diag(_tiles(k, c)))
    attn = kmat[:, c:] * dec_i
    t_inv = _unit_lower_inverse(kmat[:, :c] * dec_s, c)
    sol = _bmm(t_inv, jnp.concatenate([_block_diag(_tiles(beta * v, c)), _block_diag(_tiles(bk * eg, c))],
                                      axis=2))
    kd_t = _transpose_tiles(_tiles(k * jnp.exp(_chunk_last(gcum, c) - gcum), c))
    decay = _column_scale(eg, c)

    st = st_scr[...]
    o_steps = []
    for ch in range(nch):
        now = lambda x: _step_tiles(x, ch, bb, nch)
        sol_c = now(sol)
        wq = _bmm(jnp.concatenate([sol_c[:, :, LANES:], now(qg_g)], axis=1), st)
        delta = sol_c[:, :, :LANES] - wq[:, :c]
        o_steps.append(wq[:, c:] + _bmm(now(attn), _block_diag(delta)))
        st = now(decay) * st + _keep_head_blocks(_bmm(now(kd_t), delta))
    st_scr[...] = st
    o = _untile(_tiles_from_steps(o_steps, bb, nch))

    o = o * lax.rsqrt(_head_sum(o * o, ones) * (1.0 / HEAD_DIM) + RMS_EPS) * nw_ref[...]
    o_ref[...] = (o * _silu(zgate)).astype(BF16).reshape(bb, tt, MIX)
    st_out_ref[...] = st.reshape(bb, N_PAIRS, LANES, LANES)


def _gdn(pc, pab, conv0, st0, conv_w, a_log, dt_bias, norm_w, eab, c):
    bsz, t, _ = pc.shape
    bb, tt = _mixer_tiling(bsz, t, c)
    vec = lambda w: _const_spec((1, w))
    state = pl.BlockSpec((bb, N_PAIRS, LANES, LANES), lambda b, i: (b, 0, 0, 0))
    return pl.pallas_call(
        functools.partial(_gdn_kernel, c=c),
        grid=(bsz // bb, t // tt),
        in_specs=[pl.BlockSpec((bb, tt, C_MAIN), lambda b, i: (b, i, 0)),
                  pl.BlockSpec((bb, tt, LANES), lambda b, i: (b, i, 0)),
                  pl.BlockSpec((bb, SUBLANES, 3 * MIX), lambda b, i: (b, 0, 0)), state,
                  _const_spec((CONV_W, 3 * MIX)), vec(MIX), vec(MIX), vec(MIX), _const_spec(eab.shape)],
        out_specs=[pl.BlockSpec((bb, tt, MIX), lambda b, i: (b, i, 0)), state],
        out_shape=[jax.ShapeDtypeStruct((bsz, t, MIX), BF16),
                   jax.ShapeDtypeStruct((bsz, N_PAIRS, LANES, LANES), F32)],
        scratch_shapes=[pltpu.VMEM((bb, tt + SUBLANES, 3 * MIX), F32),
                        pltpu.VMEM((bb * N_PAIRS, LANES, LANES), F32)],
        compiler_params=_params("arbitrary", "arbitrary"),
        name="gdn",
    )(pc, pab, conv0, st0, conv_w, a_log, dt_bias, norm_w, eab)


def _band_core(q, k_g, v_g, bias):
    bb, cq, _ = q.shape
    wn = k_g.shape[1]
    q_g = jnp.stack([_stack(q[b, :, j * LANES:(j + 1) * LANES]) for b in range(bb) for j in range(N_PAIRS)])
    s = _bmm_nt(q_g, k_g)
    s = (s.reshape(bb, N_PAIRS, 2 * cq, wn) + bias).reshape(bb * N_PAIRS, 2 * cq, wn)
    e = jnp.exp2(s - jnp.max(s, axis=-1, keepdims=True))
    pv = _bmm(e, v_g) / jnp.sum(e, axis=-1, keepdims=True)
    lane = _iota((cq, LANES), 1)
    return jnp.stack([
        jnp.concatenate([jnp.where(lane < HEAD_DIM, pv[b * N_PAIRS + j, :cq], pv[b * N_PAIRS + j, cq:])
                         for j in range(N_PAIRS)], axis=1) for b in range(bb)]).astype(BF16)


def _band_prompt_kernel(q_ref, k_ref, v_ref, bias_ref, o_ref):
    bb, cq, _ = q_ref.shape
    wn = BAND + cq
    first = pl.program_id(1) * cq - BAND
    start = pl.multiple_of(jnp.maximum(first, 0), LANES)
    shift = pl.multiple_of(start - first, LANES)
    window = lambda ref: jnp.stack([ref[b, pl.ds(start, wn), j * LANES:(j + 1) * LANES]
                                    for b in range(bb) for j in range(N_PAIRS)])
    o_ref[...] = _band_core(q_ref[...], window(k_ref), window(v_ref), bias_ref[:, :, pl.ds(shift, wn)])


def _band_prompt(q, k, v, table):
    bsz, t, _ = q.shape
    cq = BAND_QUERIES
    span = 2 * BAND + cq
    q_chunk = jnp.arange(cq)[:, None] // CHUNK
    k_chunk = jnp.arange(span)[None, :] // CHUNK
    in_band = (k_chunk >= q_chunk) & (k_chunk <= q_chunk + BAND // CHUNK)
    bias = jnp.where(in_band, _rel_bias(table, cq, span) * LOG2_E, -1e30).reshape(N_PAIRS, 2 * cq, span)
    bb = min(bsz, BAND_PROMPT_ROWS)
    full = pl.BlockSpec((bb, t, MIX), lambda b, i: (b, 0, 0))
    return pl.pallas_call(
        _band_prompt_kernel,
        grid=(bsz // bb, t // cq),
        in_specs=[pl.BlockSpec((bb, cq, MIX), lambda b, i: (b, i, 0)), full, full, _const_spec(bias.shape)],
        out_specs=pl.BlockSpec((bb, cq, MIX), lambda b, i: (b, i, 0)),
        out_shape=jax.ShapeDtypeStruct((bsz, t, MIX), BF16),
        compiler_params=_params("arbitrary", "arbitrary"),
        name="band_prompt",
    )(q, k, v, bias)


def _band_sample_kernel(q_ref, kp_ref, vp_ref, kn_ref, vn_ref, bias_ref, o_ref):
    bb = q_ref.shape[0]
    keys = lambda past, new: jnp.stack([
        jnp.concatenate([past[b, :, j * LANES:(j + 1) * LANES], new[b, :, j * LANES:(j + 1) * LANES]], axis=0)
        for b in range(bb) for j in range(N_PAIRS)]).astype(BF16)
    o_ref[...] = _band_core(q_ref[...], keys(kp_ref, kn_ref), keys(vp_ref, vn_ref), bias_ref[...])


def _band_sample(q, k, v, caches, table):
    k_past, v_past, layer = caches
    bsz, t, _ = q.shape
    past = k_past.shape[2]
    bb = min(bsz, BAND_SAMPLE_ROWS // t)
    bias = (_rel_bias(table, t, past + t) * LOG2_E).reshape(N_PAIRS, 2 * t, past + t)
    new = pl.BlockSpec((bb, t, MIX), lambda b: (b, 0, 0))
    old = pl.BlockSpec((None, bb, past, MIX), lambda b: (layer, b, 0, 0))
    return pl.pallas_call(
        _band_sample_kernel,
        grid=(bsz // bb,),
        in_specs=[new, old, old, new, new, _const_spec(bias.shape)],
        out_specs=new,
        out_shape=jax.ShapeDtypeStruct((bsz, t, MIX), BF16),
        compiler_params=_params("arbitrary"),
        name="band_sample",
    )(q, k_past, v_past, k, v, bias)


def _merge_kernel(x_ref, oa_ref, ob_ref, oc_ref, sh_ref, sc_ref, gm_ref, sh2_ref, sc2_ref, nw_ref, nw2_ref,
                  wgate_ref, bgate_ref, wbr_ref, wout_ref, wrt_ref, rb_ref,
                  x1_ref, h2_ref, comb_ref):
    x = x_ref[...]
    h = _norm_mod(x, nw_ref[...], sc_ref[0], sh_ref[0]).astype(BF16)
    gates = _sigmoid(_dot(h, wgate_ref[...]) + bgate_ref[...])
    mixed = None
    for i, o_ref in enumerate((oa_ref, ob_ref, oc_ref)):
        term = gates[:, i * D_MODEL:(i + 1) * D_MODEL] * _bdot(o_ref[...], wbr_ref[i])
        mixed = term if mixed is None else mixed + term
    x1 = x + gm_ref[0] * _bdot(mixed, wout_ref[...])
    x1_ref[...] = x1
    h2 = _norm_mod(x1, nw2_ref[...], sc2_ref[0], sh2_ref[0])
    h2_ref[...] = h2.astype(BF16)

    scores = _sigmoid(_dot_split(wrt_ref[...], h2, nt=True))
    sel = scores + rb_ref[...]
    tm = scores.shape[1]
    per = N_EXPERTS // N_GROUPS
    best_val, best = None, None
    for g in range(N_GROUPS):
        rows = [sel[g * per + i:g * per + i + 1, :] for i in range(per)]
        top2 = None
        for i in range(per):
            for i2 in range(i + 1, per):
                pair = rows[i] + rows[i2]
                top2 = pair if top2 is None else jnp.maximum(top2, pair)
        if g == 0:
            best_val, best = top2, jnp.zeros((1, tm), jnp.int32)
        else:
            better = top2 > best_val
            best = jnp.where(better, g, best)
            best_val = jnp.where(better, top2, best_val)
    eidx = _iota((N_EXPERTS, tm), 0)
    cand = jnp.where(eidx // per == best, sel, -jnp.inf)
    m1 = jnp.max(cand, axis=0, keepdims=True)
    i1 = jnp.min(jnp.where(cand == m1, eidx, N_EXPERTS), axis=0, keepdims=True)
    cand2 = jnp.where(eidx == i1, -jnp.inf, cand)
    m2 = jnp.max(cand2, axis=0, keepdims=True)
    i2 = jnp.min(jnp.where(cand2 == m2, eidx, N_EXPERTS), axis=0, keepdims=True)
    w1 = jnp.sum(jnp.where(eidx == i1, scores, 0.0), axis=0, keepdims=True)
    w2 = jnp.sum(jnp.where(eidx == i2, scores, 0.0), axis=0, keepdims=True)
    den = w1 + w2
    comb_ref[...] = jnp.where(eidx == i1, w1 / den, 0.0) + jnp.where(eidx == i2, w2 / den, 0.0)


def _merge(x, oa, ob, oc, mods, norm_w, norm2_w, w_gate, b_gate, w_branch, w_out, w_router_t, router_bias,
           tm, tiles_per_group):
    n = x.shape[0]
    row = lambda w: pl.BlockSpec((tm, w), lambda i: (i, 0))
    mod_specs = [_mod_spec(m, tm, tiles_per_group) for m in mods]
    return pl.pallas_call(
        _merge_kernel,
        grid=(n // tm,),
        in_specs=[row(D_MODEL), row(MIX), row(MIX), row(MIX)] + mod_specs
                 + [_const_spec((1, D_MODEL)), _const_spec((1, D_MODEL)), _const_spec(w_gate.shape),
                    _const_spec(b_gate.shape), _const_spec(w_branch.shape), _const_spec(w_out.shape),
                    _const_spec(w_router_t.shape), _const_spec(router_bias.shape)],
        out_specs=[row(D_MODEL), row(D_MODEL), pl.BlockSpec((N_EXPERTS, tm), lambda i: (0, i))],
        out_shape=[jax.ShapeDtypeStruct((n, D_MODEL), F32), jax.ShapeDtypeStruct((n, D_MODEL), BF16),
                   jax.ShapeDtypeStruct((N_EXPERTS, n), F32)],
        compiler_params=_params("arbitrary"),
        name="merge",
    )(x, oa, ob, oc, *mods, norm_w, norm2_w, w_gate, b_gate, w_branch, w_out, w_router_t, router_bias)


def _moe_kernel(x1_ref, h2_ref, comb_ref, gf_ref, wg_ref, wu_ref, wd_ref, o_ref):
    h2 = h2_ref[...]
    comb = jnp.transpose(comb_ref[...])
    per = N_EXPERTS // N_GROUPS
    acc = None
    for g in range(N_GROUPS):
        hid = [(_silu(_dot(h2, wg_ref[e])) * _dot(h2, wu_ref[e]) * comb[:, e:e + 1]).astype(BF16)
               for e in range(g * per, (g + 1) * per)]
        term = _dot(jnp.concatenate(hid, axis=1), wd_ref[g])
        acc = term if acc is None else acc + term
    o_ref[...] = x1_ref[...] + gf_ref[0] * acc


def _moe(x1, h2, comb, g_ffn, wg, wu, wd, tm, tiles_per_group):
    n = x1.shape[0]
    row = lambda w: pl.BlockSpec((tm, w), lambda i: (i, 0))
    return pl.pallas_call(
        _moe_kernel,
        grid=(n // tm,),
        in_specs=[row(D_MODEL), row(D_MODEL), pl.BlockSpec((N_EXPERTS, tm), lambda i: (0, i)),
                  _mod_spec(g_ffn, tm, tiles_per_group),
                  _const_spec(wg.shape), _const_spec(wu.shape), _const_spec(wd.shape)],
        out_specs=row(D_MODEL),
        out_shape=jax.ShapeDtypeStruct((n, D_MODEL), F32),
        compiler_params=_params("arbitrary"),
        name="moe",
    )(x1, h2, comb, g_ffn, wg, wu, wd)


def _pair_state(s):
    bsz = s.shape[0]
    s = s.reshape(bsz, N_PAIRS, 2, HEAD_DIM, HEAD_DIM)
    z = jnp.zeros_like(s[:, :, 0])
    top = jnp.concatenate([s[:, :, 0], z], axis=-1)
    bot = jnp.concatenate([z, s[:, :, 1]], axis=-1)
    return jnp.concatenate([top, bot], axis=-2)


def _unpair_state(s):
    bsz = s.shape[0]
    return jnp.stack([s[:, :, :HEAD_DIM, :HEAD_DIM], s[:, :, HEAD_DIM:, HEAD_DIM:]], axis=2).reshape(
        bsz, N_HEADS, HEAD_DIM, HEAD_DIM)


def _rel_bias(table, cq, wn):
    span = cq - 1 + wn
    offs = jnp.arange(-(cq - 1), wn + 1)
    diag = table[:, jnp.clip(BAND - offs, -MAX_REL, MAX_REL) + MAX_REL].astype(F32)
    skew = jnp.tile(diag, (1, cq))[:, :cq * span].reshape(-1, cq, span)
    return skew[:, :, cq - 1:]


def _layer(x, mod, state, wts, tm):
    bsz, t, _ = x.shape
    n = bsz * t
    c = min(CHUNK, t)
    shift0, wkv0, caches, conv0, s0 = state
    if t % tm == 0:
        tiles_per_group = t // tm
        mods = [m.reshape(bsz, 1, D_MODEL) for m in jnp.split(mod, 6, axis=-1)]
    else:
        tiles_per_group = 1
        mods = [jnp.repeat(m, t, axis=0).reshape(n // tm, tm, D_MODEL) for m in jnp.split(mod, 6, axis=-1)]
    sh_mix, sc_mix, g_mix, sh_ffn, sc_ffn, g_ffn = mods
    xf = x.reshape(n, D_MODEL)

    prompt = caches is None
    kv_tail_only = prompt and tm == BAND and t % tm == 0
    pa, q, k, v, kb, vb, pc, pab = _inproj(xf, sh_mix, sc_mix, wts["norm_mix_w"], wts["wa"], wts["wb"], wts["wc"],
                                           wts["wab"], wts["q_norm"], wts["k_norm"], tm, tiles_per_group,
                                           kv_tail_only)
    pa = pa.reshape(bsz, t, A_COLS)
    q, kb, vb = (z.reshape(bsz, t, MIX) for z in (q, kb, vb))
    k, v = (z.reshape(bsz, -1, MIX) for z in (k, v))
    pc = pc.reshape(bsz, t, C_MAIN)

    oa, wkv = _rwkv(pa, shift0.reshape(bsz, 1, A_COLS), _pair_state(jnp.swapaxes(wkv0, -1, -2)),
                    wts["mu"], wts["w0"], wts["a0"], wts["wwa"], wts["g_up"], wts["k_k"], wts["k_a"], wts["r_k"],
                    wts["gn_w"], wts["gn_b"], c)
    new_wkv = jnp.swapaxes(_unpair_state(wkv), -1, -2)

    if prompt:
        ob = _band_prompt(q, kb, vb, wts["rel_bias"])
        new_k, new_v = k[:, -BAND:], v[:, -BAND:]
    else:
        ob = _band_sample(q, k, v, caches, wts["rel_bias"])
        new_k, new_v = k, v

    conv_pad = jnp.pad(conv0, ((0, 0), (SUBLANES - (CONV_W - 1), 0), (0, 0)))
    oc, s_new = _gdn(pc, pab.reshape(bsz, t, LANES), conv_pad, _pair_state(s0), wts["conv_w"], wts["a_log"],
                     wts["dt_bias"], wts["gdn_norm_w"], wts["eab"], c)
    tail = min(t, CONV_W - 1)
    new_conv = jnp.concatenate([conv0, pc[:, t - tail:, :3 * MIX]], axis=1)[:, -(CONV_W - 1):]

    x1, h2, comb = _merge(xf, oa.reshape(n, MIX), ob.reshape(n, MIX), oc.reshape(n, MIX),
                          [sh_mix, sc_mix, g_mix, sh_ffn, sc_ffn], wts["norm_mix_w"], wts["norm_ffn_w"],
                          wts["w_gate"], wts["b_gate"], wts["w_branch"], wts["w_out"], wts["w_router_t"],
                          wts["router_bias"], tm, tiles_per_group)
    x2 = _moe(x1, h2, comb, g_ffn, wts["wg"], wts["wu"], wts["wd"], tm, tiles_per_group)
    heads = lambda z: z.reshape(bsz, -1, N_HEADS, HEAD_DIM)
    return x2.reshape(bsz, t, D_MODEL), (pa[:, -1], new_wkv, heads(new_k), heads(new_v), new_conv,
                                         _unpair_state(s_new))


def _prepare_layer(l, w_in, norm_mix_w, norm_ffn_w, rwkv_mu, rwkv_w0, rwkv_w_up, rwkv_a0, rwkv_a_up, rwkv_g_up,
                   rwkv_k_k, rwkv_k_a, rwkv_r_k, rwkv_gn_w, rwkv_gn_b, band_q_norm, band_k_norm, band_rel_bias,
                   gdn_conv_w, gdn_a_log, gdn_dt_bias, gdn_norm_w, w_branch, w_gate, b_gate, w_out,
                   w_router, router_bias, w_exp_gate, w_exp_up, w_exp_down):
    row = lambda z: z.reshape(1, -1).astype(F32)
    per_head = lambda z: jnp.repeat(z, HEAD_DIM).reshape(1, MIX)
    win = w_in[l].astype(BF16)
    c0 = A_COLS + B_COLS
    wab = jnp.pad(win[:, c0 + C_MAIN:], ((0, 0), (0, LANES - 2 * N_HEADS)))
    zeros = jnp.zeros((HEAD_DIM, MIX), F32)
    wwa = jnp.concatenate([jnp.concatenate([rwkv_w_up[l], zeros], axis=1),
                           jnp.concatenate([zeros, rwkv_a_up[l]], axis=1)], axis=0)
    head_of_lane = jnp.arange(MIX) // HEAD_DIM
    src = jnp.arange(LANES)[:, None]
    eab = jnp.concatenate([(src == head_of_lane[None, :]), (src == N_HEADS + head_of_lane[None, :])],
                          axis=1).astype(BF16)
    return dict(
        norm_mix_w=row(norm_mix_w[l]), norm_ffn_w=row(norm_ffn_w[l]),
        wa=win[:, :A_COLS], wb=win[:, A_COLS:c0], wc=win[:, c0:c0 + C_MAIN], wab=wab,
        q_norm=row(jnp.tile(band_q_norm[l], N_HEADS)), k_norm=row(jnp.tile(band_k_norm[l], N_HEADS)),
        mu=row(rwkv_mu[l]), w0=row(rwkv_w0[l]), a0=row(rwkv_a0[l]), wwa=wwa, g_up=rwkv_g_up[l],
        k_k=row(rwkv_k_k[l]), k_a=row(rwkv_k_a[l]), r_k=row(rwkv_r_k[l]), gn_w=row(rwkv_gn_w[l]),
        gn_b=row(rwkv_gn_b[l]), rel_bias=band_rel_bias[l],
        conv_w=gdn_conv_w[l], a_log=per_head(gdn_a_log[l]), dt_bias=per_head(gdn_dt_bias[l]),
        gdn_norm_w=row(jnp.tile(gdn_norm_w[l], N_HEADS)), eab=eab,
        w_gate=w_gate[l].astype(BF16), b_gate=row(b_gate[l]), w_branch=w_branch[l].astype(BF16),
        w_out=w_out[l].astype(BF16), w_router_t=jnp.transpose(w_router), router_bias=router_bias.reshape(-1, 1),
        wg=w_exp_gate[l].astype(BF16), wu=w_exp_up[l].astype(BF16),
        wd=w_exp_down[l].astype(BF16).reshape(N_GROUPS, -1, D_MODEL),
    )


def kernel(x_prompt, x_sample, c_prompt, c_sample, state_rwkv_shift, state_rwkv_wkv, cache_band_k, cache_band_v, state_gdn_conv, state_gdn_S, w_ada, b_ada, norm_mix_w, norm_ffn_w, w_in, rwkv_mu, rwkv_w0, rwkv_w_up, rwkv_a0, rwkv_a_up, rwkv_g_up, rwkv_k_k, rwkv_k_a, rwkv_r_k, rwkv_gn_w, rwkv_gn_b, band_q_norm, band_k_norm, band_rel_bias, gdn_conv_w, gdn_a_log, gdn_dt_bias, gdn_norm_w, w_branch, w_gate, b_gate, w_out, w_router, router_bias, w_exp_gate, w_exp_up, w_exp_down):
    depth = w_ada.shape[0]
    bsz = x_prompt.shape[0]
    mod = _adaln(jnp.concatenate([c_prompt, c_sample], axis=0), w_ada, b_ada)
    mod_p, mod_s = mod[:, :bsz], mod[:, bsz:]
    zero_state = (jnp.zeros((bsz, A_COLS), F32), jnp.zeros((bsz, N_HEADS, HEAD_DIM, HEAD_DIM), F32), None,
                  jnp.zeros((bsz, CONV_W - 1, 3 * MIX), F32), jnp.zeros((bsz, N_HEADS, HEAD_DIM, HEAD_DIM), F32))
    k_cache = cache_band_k.reshape(cache_band_k.shape[:3] + (MIX,))
    v_cache = cache_band_v.reshape(cache_band_v.shape[:3] + (MIX,))
    tm = TOKEN_TILE
    xp, xs = x_prompt, x_sample
    new_p, new_s = [], []
    for l in range(depth):
        wts = _prepare_layer(l, w_in, norm_mix_w, norm_ffn_w, rwkv_mu, rwkv_w0, rwkv_w_up, rwkv_a0, rwkv_a_up,
                             rwkv_g_up, rwkv_k_k, rwkv_k_a, rwkv_r_k, rwkv_gn_w, rwkv_gn_b, band_q_norm,
                             band_k_norm, band_rel_bias, gdn_conv_w, gdn_a_log, gdn_dt_bias, gdn_norm_w,
                             w_branch, w_gate, b_gate, w_out, w_router, router_bias, w_exp_gate, w_exp_up,
                             w_exp_down)
        xp, st_p = _layer(xp, mod_p[l], zero_state, wts, tm)
        xs, st_s = _layer(xs, mod_s[l], (state_rwkv_shift[l], state_rwkv_wkv[l], (k_cache, v_cache, l),
                                         state_gdn_conv[l], state_gdn_S[l]), wts, tm)
        new_p.append(st_p)
        new_s.append(st_s)
    p_out = [jnp.stack(z, axis=0) for z in zip(*new_p)]
    s_out = [jnp.stack(z, axis=0) for z in zip(*new_s)]
    return (xp, xs, *p_out, *s_out)
```

```python
import functools
import math

import jax
import jax.numpy as jnp
from jax import lax
from jax.experimental import pallas as pl
from jax.experimental.pallas import tpu as pltpu

F32 = jnp.float32
BF16 = jnp.bfloat16

D_MODEL = 1024
MIX = 512
HEAD_DIM = 64
N_HEADS = MIX // HEAD_DIM
LANES = 128
N_PAIRS = MIX // LANES
SUBLANES = 8
CHUNK = 64
TOKEN_TILE = 512
MIXER_ROWS = 512
MIXER_SPAN = 256
CUMSUM_ROWS = 256
BAND_SAMPLE_ROWS = 256
BAND_QUERIES = 2 * CHUNK
BAND_PROMPT_ROWS = 2
ADALN_TILE = 768
A_LORA_W, A_LORA_A, A_LORA_G = 64, 64, 128
A_COLS = 3 * MIX + A_LORA_W + A_LORA_A + A_LORA_G
B_COLS = 3 * MIX
C_MAIN = 4 * MIX
CONV_W = 4
BAND = 8 * CHUNK
MAX_REL = 2 * CHUNK
N_EXPERTS = 16
N_GROUPS = 4
RMS_EPS = 1e-6
A_GN_EPS = 64e-5
A_DECAY_SCALE = math.exp(-0.5)
LOG2_E = math.log2(math.e)
SCORE_SCALE = HEAD_DIM ** -0.5 * LOG2_E
VMEM_LIMIT_BYTES = 56 * 1024 * 1024


def _params(*sem):
    return pltpu.CompilerParams(dimension_semantics=sem, vmem_limit_bytes=VMEM_LIMIT_BYTES)


def _dot(a, b):
    return lax.dot_general(a, b, (((1,), (0,)), ((), ())), preferred_element_type=F32)


def _dot_nt(a, b):
    return lax.dot_general(a, b, (((1,), (1,)), ((), ())), preferred_element_type=F32)


def _bdot(a, b):
    return _dot(a.astype(BF16), b.astype(BF16))


def _dot_split(a, b, nt=False):
    dot = _dot_nt if nt else _dot
    a_hi, a_lo = _bf16_terms(a, 2)
    b_hi, b_lo = _bf16_terms(b, 2)
    return dot(a_hi, b_hi) + (dot(a_hi, b_lo) + dot(a_lo, b_hi))


def _bmm(a, b):
    return lax.dot_general(a.astype(BF16), b.astype(BF16), (((2,), (1,)), ((0,), (0,))),
                           preferred_element_type=F32)


def _bmm_nt(a, b):
    return lax.dot_general(a.astype(BF16), b.astype(BF16), (((2,), (2,)), ((0,), (0,))),
                           preferred_element_type=F32)


def _sigmoid(x):
    return 0.5 * jnp.tanh(0.5 * x) + 0.5


def _silu(x):
    return x * _sigmoid(x)


def _softplus(x):
    return jnp.maximum(x, 0.0) + jnp.log1p(jnp.exp(-jnp.abs(x)))


def _iota(shape, dim):
    return lax.broadcasted_iota(jnp.int32, shape, dim)


def _bf16_terms(x, terms):
    parts = []
    for _ in range(terms):
        part = x.astype(BF16)
        parts.append(part)
        x = x - part.astype(F32)
    return parts


def _dot_01(x, m, terms):
    out = None
    for part in _bf16_terms(x, terms):
        term = _dot(part, m)
        out = term if out is None else out + term
    return out


def _dot_01_left(m, x, terms):
    out = None
    for part in _bf16_terms(x, terms):
        term = _dot(m, part)
        out = term if out is None else out + term
    return out


def _head_ones():
    return (_iota((LANES, LANES), 0) // HEAD_DIM == _iota((LANES, LANES), 1) // HEAD_DIM).astype(BF16)


def _head_sum(x, ones):
    parts = [_dot_01(x[:, g * LANES:(g + 1) * LANES], ones, 2) for g in range(x.shape[1] // LANES)]
    return parts[0] if len(parts) == 1 else jnp.concatenate(parts, axis=1)


def _stack(x):
    lane = _iota(x.shape, 1)
    return jnp.concatenate([jnp.where(lane < HEAD_DIM, x, 0.0), jnp.where(lane >= HEAD_DIM, x, 0.0)], axis=0)


def _tiles(x, c):
    return jnp.stack([x[u * c:(u + 1) * c, j * LANES:(j + 1) * LANES]
                      for u in range(x.shape[0] // c) for j in range(N_PAIRS)])


def _untile(xs):
    return jnp.concatenate([jnp.concatenate([xs[u * N_PAIRS + j] for j in range(N_PAIRS)], axis=1)
                            for u in range(xs.shape[0] // N_PAIRS)], axis=0)


def _step_tiles(x, ch, bb, nch):
    parts = [x[(b * nch + ch) * N_PAIRS:(b * nch + ch + 1) * N_PAIRS] for b in range(bb)]
    return parts[0] if bb == 1 else jnp.concatenate(parts, axis=0)


def _tiles_from_steps(steps, bb, nch):
    parts = [steps[ch][b * N_PAIRS:(b + 1) * N_PAIRS] for b in range(bb) for ch in range(nch)]
    return parts[0] if len(parts) == 1 else jnp.concatenate(parts, axis=0)


def _chunk_cumsum(x, c):
    seg = min(x.shape[0], CUMSUM_ROWS)
    tril = _chunk_tril(seg, c)
    parts = [_dot_01_left(tril, x[s:s + seg], 3) for s in range(0, x.shape[0], seg)]
    return parts[0] if len(parts) == 1 else jnp.concatenate(parts, axis=0)


def _block_diag(x):
    x = x.astype(BF16)
    first = _iota(x.shape[1:], 1) < x.shape[2] // 2
    zero = jnp.zeros_like(x)
    return jnp.concatenate([jnp.where(first, x, zero), jnp.where(first, zero, x)], axis=1)


def _keep_head_blocks(x):
    same = _iota(x.shape[1:], 0) // HEAD_DIM == _iota(x.shape[1:], 1) // HEAD_DIM
    return jnp.where(same, x, 0.0)


def _transpose_tiles(xs):
    return jnp.stack([jnp.transpose(xs[g]) for g in range(xs.shape[0])])


def _chunk_last(x, c):
    return jnp.concatenate([jnp.broadcast_to(x[(u + 1) * c - 1:(u + 1) * c], (c, x.shape[1]))
                            for u in range(x.shape[0] // c)], axis=0)


def _column_scale(row_values, c):
    tiles = []
    for u in range(row_values.shape[0] // c):
        last = row_values[(u + 1) * c - 1:(u + 1) * c]
        for j in range(N_PAIRS):
            tiles.append(jnp.transpose(jnp.broadcast_to(last[:, j * LANES:(j + 1) * LANES], (LANES, LANES))))
    return jnp.stack(tiles)


def _chunk_tril(rows, c):
    row, col = _iota((rows, rows), 0), _iota((rows, rows), 1)
    return ((col <= row) & (row // c == col // c)).astype(BF16)


def _tri_masks(c):
    row, col = _iota((c, 2 * c), 0), _iota((c, 2 * c), 1) % c
    return col < row, col <= row


def _unit_lower_inverse(a, c):
    row, col = _iota((c, 2 * c), 0), _iota((c, 2 * c), 1) % c
    eye = (row == col).astype(F32)
    same = lambda s: row // s == col // s
    a0 = jnp.where(same(SUBLANES), a, 0.0)
    a2 = _bmm(a0, _block_diag(a0))
    a4 = _bmm(a2, _block_diag(a2))
    x = _bmm(_bmm(eye - a0, _block_diag(eye + a2)), _block_diag(eye + a4))
    s = SUBLANES
    while s < c:
        off = jnp.where(same(2 * s) & jnp.logical_not(same(s)), a, 0.0)
        x = x - _bmm(_bmm(x, _block_diag(off)), _block_diag(x))
        s *= 2
    return x


def _adaln_kernel(c_ref, w_ref, b_ref, o_ref):
    o_ref[0] = _dot_split(_silu(c_ref[...]), w_ref[0]) + b_ref[0]


def _adaln(c, w_ada, b_ada):
    depth, bsz, tn = w_ada.shape[0], c.shape[0], ADALN_TILE
    return pl.pallas_call(
        _adaln_kernel,
        grid=(depth, 6 * D_MODEL // tn),
        in_specs=[pl.BlockSpec((bsz, D_MODEL), lambda l, j: (0, 0)),
                  pl.BlockSpec((1, D_MODEL, tn), lambda l, j: (l, 0, j)),
                  pl.BlockSpec((1, 1, tn), lambda l, j: (l, 0, j))],
        out_specs=pl.BlockSpec((1, bsz, tn), lambda l, j: (l, 0, j)),
        out_shape=jax.ShapeDtypeStruct((depth, bsz, 6 * D_MODEL), F32),
        compiler_params=_params("arbitrary", "arbitrary"),
        name="adaln",
    )(c, w_ada, b_ada.reshape(depth, 1, 6 * D_MODEL))


def _norm_mod(x, norm_w, scale, shift):
    y = x * lax.rsqrt(jnp.mean(x * x, axis=-1, keepdims=True) + RMS_EPS)
    return y * norm_w * (1.0 + scale) + shift


def _inproj_kernel(x_ref, sh_ref, sc_ref, nw_ref, wa_ref, wb_ref, wc_ref, wab_ref, qn_ref, kn_ref,
                   pa_ref, q_ref, k_ref, v_ref, kb_ref, vb_ref, pc_ref, pab_ref):
    h = _norm_mod(x_ref[...], nw_ref[...], sc_ref[0], sh_ref[0]).astype(BF16)
    pa_ref[...] = _dot(h, wa_ref[...])
    pb = _dot(h, wb_ref[...])
    ones = _head_ones()

    def head_rms(y, w):
        return y * lax.rsqrt(_head_sum(y * y, ones) * (1.0 / HEAD_DIM) + RMS_EPS) * w

    q_ref[...] = (head_rms(pb[:, :MIX], qn_ref[...]) * SCORE_SCALE).astype(BF16)
    k = head_rms(pb[:, MIX:2 * MIX], kn_ref[...])
    v = pb[:, 2 * MIX:]
    k_ref[...], v_ref[...] = k, v
    kb_ref[...], vb_ref[...] = k.astype(BF16), v.astype(BF16)
    pc_ref[...] = _dot(h, wc_ref[...])
    pab_ref[...] = _dot(h, wab_ref[...])


def _mod_spec(mod, tm, tiles_per_group):
    rows = mod.shape[1]
    return pl.BlockSpec((1, rows, D_MODEL), lambda i: (i // tiles_per_group, 0, 0))


def _const_spec(shape):
    zeros = (0,) * len(shape)
    return pl.BlockSpec(shape, lambda *_: zeros, pipeline_mode=pl.Buffered(1))


def _inproj(x, shift, scale, norm_w, wa, wb, wc, wab, q_norm, k_norm, tm, tiles_per_group, kv_tail_only):
    n = x.shape[0]
    row = lambda w: pl.BlockSpec((tm, w), lambda i: (i, 0))
    if kv_tail_only:
        kv_rows, kv = n // tiles_per_group, pl.BlockSpec((tm, MIX), lambda i: (i // tiles_per_group, 0))
    else:
        kv_rows, kv = n, row(MIX)
    shape = lambda rows, w, dtype: jax.ShapeDtypeStruct((rows, w), dtype)
    return pl.pallas_call(
        _inproj_kernel,
        grid=(n // tm,),
        in_specs=[row(D_MODEL), _mod_spec(shift, tm, tiles_per_group), _mod_spec(scale, tm, tiles_per_group),
                  _const_spec((1, D_MODEL)), _const_spec(wa.shape), _const_spec(wb.shape), _const_spec(wc.shape),
                  _const_spec(wab.shape), _const_spec((1, MIX)), _const_spec((1, MIX))],
        out_specs=[row(A_COLS), row(MIX), kv, kv, row(MIX), row(MIX), row(C_MAIN), row(LANES)],
        out_shape=[shape(n, A_COLS, F32), shape(n, MIX, BF16), shape(kv_rows, MIX, F32), shape(kv_rows, MIX, F32),
                   shape(n, MIX, BF16), shape(n, MIX, BF16), shape(n, C_MAIN, F32), shape(n, LANES, F32)],
        compiler_params=_params("arbitrary"),
        name="inproj",
    )(x, shift, scale, norm_w, wa, wb, wc, wab, q_norm, k_norm)


def _rwkv_kernel(p_ref, shift0_ref, st0_ref, mu_ref, w0_ref, a0_ref, wwa_ref, gup_ref, kk_ref, ka_ref, rk_ref,
                 gnw_ref, gnb_ref, o_ref, st_out_ref, ext_scr, st_scr, *, c):
    bb, tt, _ = p_ref.shape
    rows, nch, n = bb * tt, tt // c, 2 * c
    ci = pl.program_id(1)

    @pl.when(ci == 0)
    def _():
        ext_scr[:, SUBLANES - 1:SUBLANES, :] = shift0_ref[...]
        st_scr[...] = st0_ref[...].reshape(bb * N_PAIRS, LANES, LANES)

    ext_scr[:, SUBLANES:SUBLANES + tt, :] = p_ref[...]
    p = p_ref[...].reshape(rows, A_COLS)
    p_prev = ext_scr[:, SUBLANES - 1:SUBLANES - 1 + tt, :].reshape(rows, A_COLS)
    ext_scr[:, 0:SUBLANES, :] = ext_scr[:, tt:tt + SUBLANES, :]
    xs = p + (p_prev - p) * mu_ref[...]
    r, k, v = xs[:, :MIX], xs[:, MIX:2 * MIX], xs[:, 2 * MIX:3 * MIX]
    lora_in = xs[:, 3 * MIX:3 * MIX + LANES]
    gd = xs[:, 3 * MIX + LANES:]
    lane = _iota((1, LANES), 1)
    lora = _bdot(jnp.where(lane < HEAD_DIM, jnp.tanh(lora_in), lora_in), wwa_ref[...])
    log_w = -A_DECAY_SCALE * _sigmoid(w0_ref[...] + lora[:, :MIX])
    a = _sigmoid(a0_ref[...] + lora[:, MIX:])
    g = _bdot(_sigmoid(gd), gup_ref[...])
    ones = _head_ones()
    kk_raw = k * kk_ref[...]
    kk = kk_raw * lax.rsqrt(_head_sum(kk_raw * kk_raw, ones) + RMS_EPS)
    k = k * (1.0 + (a - 1.0) * ka_ref[...])
    b = kk * a
    cum = _chunk_cumsum(log_w, c)
    cum_last = _chunk_last(cum, c)
    e_cum, e_neg = jnp.exp(cum), jnp.exp(-cum)
    e_prev, e_tail = jnp.exp(cum - log_w), jnp.exp(cum_last - cum)
    strict, incl = _tri_masks(c)

    kk_g, r_g, v_g = _tiles(kk * e_prev, c), _tiles(r * e_cum, c), _tiles(v, c)
    amat = _bmm_nt(jnp.concatenate([kk_g, r_g], axis=1),
                   jnp.concatenate([_block_diag(_tiles(k * e_neg, c)), _block_diag(_tiles(b * e_neg, c))],
                                   axis=1))
    a_kk = jnp.where(strict, amat[:, :c, :n], 0.0)
    a_bk = jnp.where(strict, amat[:, :c, n:], 0.0)
    a_rk = jnp.where(incl, amat[:, c:, :n], 0.0)
    a_rb = jnp.where(incl, amat[:, c:, n:], 0.0)
    t_inv = _unit_lower_inverse(a_bk, c)
    av = _bmm(jnp.concatenate([a_kk, a_rk], axis=1), _block_diag(v_g))
    wu = _bmm(t_inv, jnp.concatenate([_block_diag(kk_g), _block_diag(av[:, :c])], axis=2))
    kd_v = _keep_head_blocks(_bmm(_transpose_tiles(_tiles(k * e_tail, c)), v_g))
    bd_t = _transpose_tiles(_tiles(b * e_tail, c))
    decay = _column_scale(e_cum, c)

    st = st_scr[...]
    o_steps = []
    for ch in range(nch):
        now = lambda x: _step_tiles(x, ch, bb, nch)
        wu_c = now(wu)
        zo = _bmm(jnp.concatenate([wu_c[:, :, :LANES], now(r_g)], axis=1), st)
        z = wu_c[:, :, LANES:] + zo[:, :c]
        o_steps.append(zo[:, c:] + now(av)[:, c:] - _bmm(now(a_rb), _block_diag(z)))
        st = now(decay) * st + now(kd_v) - _keep_head_blocks(_bmm(now(bd_t), z))
    st_scr[...] = st
    o = _untile(_tiles_from_steps(o_steps, bb, nch))

    mean = _head_sum(o, ones) * (1.0 / HEAD_DIM)
    cen = o - mean
    var = _head_sum(cen * cen, ones) * (1.0 / HEAD_DIM)
    o = cen * lax.rsqrt(var + A_GN_EPS) * gnw_ref[...] + gnb_ref[...]
    bonus = _head_sum(r * k * rk_ref[...], ones) * v
    o_ref[...] = ((o + bonus) * g).astype(BF16).reshape(bb, tt, MIX)
    st_out_ref[...] = st.reshape(bb, N_PAIRS, LANES, LANES)


def _mixer_tiling(bsz, t, c):
    tt = min(t, MIXER_SPAN)
    return min(bsz, MIXER_ROWS // tt), tt


def _rwkv(pa, shift0, st0, mu, w0, a0, wwa, gup, k_k, k_a, r_k, gn_w, gn_b, c):
    bsz, t, _ = pa.shape
    bb, tt = _mixer_tiling(bsz, t, c)
    vec = lambda w: _const_spec((1, w))
    state = pl.BlockSpec((bb, N_PAIRS, LANES, LANES), lambda b, i: (b, 0, 0, 0))
    return pl.pallas_call(
        functools.partial(_rwkv_kernel, c=c),
        grid=(bsz // bb, t // tt),
        in_specs=[pl.BlockSpec((bb, tt, A_COLS), lambda b, i: (b, i, 0)),
                  pl.BlockSpec((bb, 1, A_COLS), lambda b, i: (b, 0, 0)), state,
                  vec(A_COLS), vec(MIX), vec(MIX), _const_spec(wwa.shape), _const_spec(gup.shape),
                  vec(MIX), vec(MIX), vec(MIX), vec(MIX), vec(MIX)],
        out_specs=[pl.BlockSpec((bb, tt, MIX), lambda b, i: (b, i, 0)), state],
        out_shape=[jax.ShapeDtypeStruct((bsz, t, MIX), BF16),
                   jax.ShapeDtypeStruct((bsz, N_PAIRS, LANES, LANES), F32)],
        scratch_shapes=[pltpu.VMEM((bb, tt + SUBLANES, A_COLS), F32),
                        pltpu.VMEM((bb * N_PAIRS, LANES, LANES), F32)],
        compiler_params=_params("arbitrary", "arbitrary"),
        name="rwkv",
    )(pa, shift0, st0, mu, w0, a0, wwa, gup, k_k, k_a, r_k, gn_w, gn_b)


def _gdn_kernel(pc_ref, pab_ref, conv0_ref, st0_ref, convw_ref, alog_ref, dtb_ref, nw_ref, eab_ref,
                o_ref, st_out_ref, ext_scr, st_scr, *, c):
    bb, tt, _ = pc_ref.shape
    rows, nch, n = bb * tt, tt // c, 2 * c
    w3 = 3 * MIX
    ci = pl.program_id(1)

    @pl.when(ci == 0)
    def _():
        ext_scr[:, 0:SUBLANES, :] = conv0_ref[...]
        st_scr[...] = st0_ref[...].reshape(bb * N_PAIRS, LANES, LANES)

    ext_scr[:, SUBLANES:SUBLANES + tt, :] = pc_ref[:, :, :w3]
    conv = ext_scr[:, SUBLANES:SUBLANES + tt, :] * convw_ref[CONV_W - 1:CONV_W, :]
    for s in range(1, CONV_W):
        conv = conv + ext_scr[:, SUBLANES - s:SUBLANES - s + tt, :] * convw_ref[CONV_W - 1 - s:CONV_W - s, :]
    ext_scr[:, 0:SUBLANES, :] = ext_scr[:, tt:tt + SUBLANES, :]
    qkv = _silu(conv.reshape(rows, w3))
    ones = _head_ones()
    l2n = lambda y: y * lax.rsqrt(_head_sum(y * y, ones) + RMS_EPS)
    q = l2n(qkv[:, :MIX]) * (HEAD_DIM ** -0.5)
    k = l2n(qkv[:, MIX:2 * MIX])
    v = qkv[:, 2 * MIX:]
    zgate = pc_ref[:, :, w3:].reshape(rows, MIX)
    ab = _dot_01(pab_ref[...].reshape(rows, LANES), eab_ref[...], 3)
    log_alpha = -jnp.exp(alog_ref[...]) * _softplus(ab[:, :MIX] + dtb_ref[...])
    beta = _sigmoid(ab[:, MIX:])
    gcum = _chunk_cumsum(log_alpha, c)
    eg = jnp.exp(gcum)
    bk = beta * k
    strict, incl = _tri_masks(c)

    g_t = _tiles(gcum, c)
    g_col = jnp.concatenate([g_t[:, :, :c], g_t[:, :, HEAD_DIM:HEAD_DIM + c]], axis=2)
    g_time = _transpose_tiles(g_t)
    g_row = jnp.concatenate([g_time[:, 0:1, :], g_time[:, HEAD_DIM:HEAD_DIM + 1, :]], axis=2)
    diff = g_col - g_row
    dec_s = jnp.where(strict, jnp.exp(jnp.where(strict, diff, 0.0)), 0.0)
    dec_i = jnp.where(incl, jnp.exp(jnp.where(incl, diff, 0.0)), 0.0)
    qg_g = _tiles(q * eg, c)
    kmat = _bmm_nt(jnp.concatenate([_tiles(bk, c), _tiles(q, c)], axis=1), _block_diag(_tiles(k, c)))
    attn = kmat[:, c:] * dec_i
    t_inv = _unit_lower_inverse(kmat[:, :c] * dec_s, c)
    sol = _bmm(t_inv, jnp.concatenate([_block_diag(_tiles(beta * v, c)), _block_diag(_tiles(bk * eg, c))],
                                      axis=2))
    kd_t = _transpose_tiles(_tiles(k * jnp.exp(_chunk_last(gcum, c) - gcum), c))
    decay = _column_scale(eg, c)

    st = st_scr[...]
    o_steps = []
    for ch in range(nch):
        now = lambda x: _step_tiles(x, ch, bb, nch)
        sol_c = now(sol)
        wq = _bmm(jnp.concatenate([sol_c[:, :, LANES:], now(qg_g)], axis=1), st)
        delta = sol_c[:, :, :LANES] - wq[:, :c]
        o_steps.append(wq[:, c:] + _bmm(now(attn), _block_diag(delta)))
        st = now(decay) * st + _keep_head_blocks(_bmm(now(kd_t), delta))
    st_scr[...] = st
    o = _untile(_tiles_from_steps(o_steps, bb, nch))

    o = o * lax.rsqrt(_head_sum(o * o, ones) * (1.0 / HEAD_DIM) + RMS_EPS) * nw_ref[...]
    o_ref[...] = (o * _silu(zgate)).astype(BF16).reshape(bb, tt, MIX)
    st_out_ref[...] = st.reshape(bb, N_PAIRS, LANES, LANES)


def _gdn(pc, pab, conv0, st0, conv_w, a_log, dt_bias, norm_w, eab, c):
    bsz, t, _ = pc.shape
    bb, tt = _mixer_tiling(bsz, t, c)
    vec = lambda w: _const_spec((1, w))
    state = pl.BlockSpec((bb, N_PAIRS, LANES, LANES), lambda b, i: (b, 0, 0, 0))
    return pl.pallas_call(
        functools.partial(_gdn_kernel, c=c),
        grid=(bsz // bb, t // tt),
        in_specs=[pl.BlockSpec((bb, tt, C_MAIN), lambda b, i: (b, i, 0)),
                  pl.BlockSpec((bb, tt, LANES), lambda b, i: (b, i, 0)),
                  pl.BlockSpec((bb, SUBLANES, 3 * MIX), lambda b, i: (b, 0, 0)), state,
                  _const_spec((CONV_W, 3 * MIX)), vec(MIX), vec(MIX), vec(MIX), _const_spec(eab.shape)],
        out_specs=[pl.BlockSpec((bb, tt, MIX), lambda b, i: (b, i, 0)), state],
        out_shape=[jax.ShapeDtypeStruct((bsz, t, MIX), BF16),
                   jax.ShapeDtypeStruct((bsz, N_PAIRS, LANES, LANES), F32)],
        scratch_shapes=[pltpu.VMEM((bb, tt + SUBLANES, 3 * MIX), F32),
                        pltpu.VMEM((bb * N_PAIRS, LANES, LANES), F32)],
        compiler_params=_params("arbitrary", "arbitrary"),
        name="gdn",
    )(pc, pab, conv0, st0, conv_w, a_log, dt_bias, norm_w, eab)


def _band_core(q, k_g, v_g, bias):
    bb, cq, _ = q.shape
    wn = k_g.shape[1]
    q_g = jnp.stack([_stack(q[b, :, j * LANES:(j + 1) * LANES]) for b in range(bb) for j in range(N_PAIRS)])
    s = _bmm_nt(q_g, k_g)
    s = (s.reshape(bb, N_PAIRS, 2 * cq, wn) + bias).reshape(bb * N_PAIRS, 2 * cq, wn)
    e = jnp.exp2(s - jnp.max(s, axis=-1, keepdims=True))
    pv = _bmm(e, v_g) / jnp.sum(e, axis=-1, keepdims=True)
    lane = _iota((cq, LANES), 1)
    return jnp.stack([
        jnp.concatenate([jnp.where(lane < HEAD_DIM, pv[b * N_PAIRS + j, :cq], pv[b * N_PAIRS + j, cq:])
                         for j in range(N_PAIRS)], axis=1) for b in range(bb)]).astype(BF16)


def _band_prompt_kernel(q_ref, k_ref, v_ref, bias_ref, o_ref):
    bb, cq, _ = q_ref.shape
    wn = BAND + cq
    first = pl.program_id(1) * cq - BAND
    start = pl.multiple_of(jnp.maximum(first, 0), LANES)
    shift = pl.multiple_of(start - first, LANES)
    window = lambda ref: jnp.stack([ref[b, pl.ds(start, wn), j * LANES:(j + 1) * LANES]
                                    for b in range(bb) for j in range(N_PAIRS)])
    o_ref[...] = _band_core(q_ref[...], window(k_ref), window(v_ref), bias_ref[:, :, pl.ds(shift, wn)])


def _band_prompt(q, k, v, table):
    bsz, t, _ = q.shape
    cq = BAND_QUERIES
    span = 2 * BAND + cq
    q_chunk = jnp.arange(cq)[:, None] // CHUNK
    k_chunk = jnp.arange(span)[None, :] // CHUNK
    in_band = (k_chunk >= q_chunk) & (k_chunk <= q_chunk + BAND // CHUNK)
    bias = jnp.where(in_band, _rel_bias(table, cq, span) * LOG2_E, -1e30).reshape(N_PAIRS, 2 * cq, span)
    bb = min(bsz, BAND_PROMPT_ROWS)
    full = pl.BlockSpec((bb, t, MIX), lambda b, i: (b, 0, 0))
    return pl.pallas_call(
        _band_prompt_kernel,
        grid=(bsz // bb, t // cq),
        in_specs=[pl.BlockSpec((bb, cq, MIX), lambda b, i: (b, i, 0)), full, full, _const_spec(bias.shape)],
        out_specs=pl.BlockSpec((bb, cq, MIX), lambda b, i: (b, i, 0)),
        out_shape=jax.ShapeDtypeStruct((bsz, t, MIX), BF16),
        compiler_params=_params("arbitrary", "arbitrary"),
        name="band_prompt",
    )(q, k, v, bias)


def _band_sample_kernel(q_ref, kp_ref, vp_ref, kn_ref, vn_ref, bias_ref, o_ref):
    bb = q_ref.shape[0]
    keys = lambda past, new: jnp.stack([
        jnp.concatenate([past[b, :, j * LANES:(j + 1) * LANES], new[b, :, j * LANES:(j + 1) * LANES]], axis=0)
        for b in range(bb) for j in range(N_PAIRS)]).astype(BF16)
    o_ref[...] = _band_core(q_ref[...], keys(kp_ref, kn_ref), keys(vp_ref, vn_ref), bias_ref[...])


def _band_sample(q, k, v, caches, table):
    k_past, v_past, layer = caches
    bsz, t, _ = q.shape
    past = k_past.shape[2]
    bb = min(bsz, BAND_SAMPLE_ROWS // t)
    bias = (_rel_bias(table, t, past + t) * LOG2_E).reshape(N_PAIRS, 2 * t, past + t)
    new = pl.BlockSpec((bb, t, MIX), lambda b: (b, 0, 0))
    old = pl.BlockSpec((None, bb, past, MIX), lambda b: (layer, b, 0, 0))
    return pl.pallas_call(
        _band_sample_kernel,
        grid=(bsz // bb,),
        in_specs=[new, old, old, new, new, _const_spec(bias.shape)],
        out_specs=new,
        out_shape=jax.ShapeDtypeStruct((bsz, t, MIX), BF16),
        compiler_params=_params("arbitrary"),
        name="band_sample",
    )(q, k_past, v_past, k, v, bias)


def _merge_kernel(x_ref, oa_ref, ob_ref, oc_ref, sh_ref, sc_ref, gm_ref, sh2_ref, sc2_ref, nw_ref, nw2_ref,
                  wgate_ref, bgate_ref, wbr_ref, wout_ref, wrt_ref, rb_ref,
                  x1_ref, h2_ref, comb_ref):
    x = x_ref[...]
    h = _norm_mod(x, nw_ref[...], sc_ref[0], sh_ref[0]).astype(BF16)
    gates = _sigmoid(_dot(h, wgate_ref[...]) + bgate_ref[...])
    mixed = None
    for i, o_ref in enumerate((oa_ref, ob_ref, oc_ref)):
        term = gates[:, i * D_MODEL:(i + 1) * D_MODEL] * _bdot(o_ref[...], wbr_ref[i])
        mixed = term if mixed is None else mixed + term
    x1 = x + gm_ref[0] * _bdot(mixed, wout_ref[...])
    x1_ref[...] = x1
    h2 = _norm_mod(x1, nw2_ref[...], sc2_ref[0], sh2_ref[0])
    h2_ref[...] = h2.astype(BF16)

    scores = _sigmoid(_dot_split(wrt_ref[...], h2, nt=True))
    sel = scores + rb_ref[...]
    tm = scores.shape[1]
    per = N_EXPERTS // N_GROUPS
    best_val, best = None, None
    for g in range(N_GROUPS):
        rows = [sel[g * per + i:g * per + i + 1, :] for i in range(per)]
        top2 = None
        for i in range(per):
            for i2 in range(i + 1, per):
                pair = rows[i] + rows[i2]
                top2 = pair if top2 is None else jnp.maximum(top2, pair)
        if g == 0:
            best_val, best = top2, jnp.zeros((1, tm), jnp.int32)
        else:
            better = top2 > best_val
            best = jnp.where(better, g, best)
            best_val = jnp.where(better, top2, best_val)
    eidx = _iota((N_EXPERTS, tm), 0)
    cand = jnp.where(eidx // per == best, sel, -jnp.inf)
    m1 = jnp.max(cand, axis=0, keepdims=True)
    i1 = jnp.min(jnp.where(cand == m1, eidx, N_EXPERTS), axis=0, keepdims=True)
    cand2 = jnp.where(eidx == i1, -jnp.inf, cand)
    m2 = jnp.max(cand2, axis=0, keepdims=True)
    i2 = jnp.min(jnp.where(cand2 == m2, eidx, N_EXPERTS), axis=0, keepdims=True)
    w1 = jnp.sum(jnp.where(eidx == i1, scores, 0.0), axis=0, keepdims=True)
    w2 = jnp.sum(jnp.where(eidx == i2, scores, 0.0), axis=0, keepdims=True)
    den = w1 + w2
    comb_ref[...] = jnp.where(eidx == i1, w1 / den, 0.0) + jnp.where(eidx == i2, w2 / den, 0.0)


def _merge(x, oa, ob, oc, mods, norm_w, norm2_w, w_gate, b_gate, w_branch, w_out, w_router_t, router_bias,
           tm, tiles_per_group):
    n = x.shape[0]
    row = lambda w: pl.BlockSpec((tm, w), lambda i: (i, 0))
    mod_specs = [_mod_spec(m, tm, tiles_per_group) for m in mods]
    return pl.pallas_call(
        _merge_kernel,
        grid=(n // tm,),
        in_specs=[row(D_MODEL), row(MIX), row(MIX), row(MIX)] + mod_specs
                 + [_const_spec((1, D_MODEL)), _const_spec((1, D_MODEL)), _const_spec(w_gate.shape),
                    _const_spec(b_gate.shape), _const_spec(w_branch.shape), _const_spec(w_out.shape),
                    _const_spec(w_router_t.shape), _const_spec(router_bias.shape)],
        out_specs=[row(D_MODEL), row(D_MODEL), pl.BlockSpec((N_EXPERTS, tm), lambda i: (0, i))],
        out_shape=[jax.ShapeDtypeStruct((n, D_MODEL), F32), jax.ShapeDtypeStruct((n, D_MODEL), BF16),
                   jax.ShapeDtypeStruct((N_EXPERTS, n), F32)],
        compiler_params=_params("arbitrary"),
        name="merge",
    )(x, oa, ob, oc, *mods, norm_w, norm2_w, w_gate, b_gate, w_branch, w_out, w_router_t, router_bias)


def _moe_kernel(x1_ref, h2_ref, comb_ref, gf_ref, wg_ref, wu_ref, wd_ref, o_ref):
    h2 = h2_ref[...]
    comb = jnp.transpose(comb_ref[...])
    per = N_EXPERTS // N_GROUPS
    acc = None
    for g in range(N_GROUPS):
        hid = [(_silu(_dot(h2, wg_ref[e])) * _dot(h2, wu_ref[e]) * comb[:, e:e + 1]).astype(BF16)
               for e in range(g * per, (g + 1) * per)]
        term = _dot(jnp.concatenate(hid, axis=1), wd_ref[g])
        acc = term if acc is None else acc + term
    o_ref[...] = x1_ref[...] + gf_ref[0] * acc


def _moe(x1, h2, comb, g_ffn, wg, wu, wd, tm, tiles_per_group):
    n = x1.shape[0]
    row = lambda w: pl.BlockSpec((tm, w), lambda i: (i, 0))
    return pl.pallas_call(
        _moe_kernel,
        grid=(n // tm,),
        in_specs=[row(D_MODEL), row(D_MODEL), pl.BlockSpec((N_EXPERTS, tm), lambda i: (0, i)),
                  _mod_spec(g_ffn, tm, tiles_per_group),
                  _const_spec(wg.shape), _const_spec(wu.shape), _const_spec(wd.shape)],
        out_specs=row(D_MODEL),
        out_shape=jax.ShapeDtypeStruct((n, D_MODEL), F32),
        compiler_params=pltpu.CompilerParams(dimension_semantics=("arbitrary",), vmem_limit_bytes=VMEM_LIMIT_BYTES,
                                             allow_input_fusion=[False, False, False, False, True, True, True]),
        name="moe",
    )(x1, h2, comb, g_ffn, wg, wu, wd)


def _pair_state(s):
    bsz = s.shape[0]
    s = s.reshape(bsz, N_PAIRS, 2, HEAD_DIM, HEAD_DIM)
    z = jnp.zeros_like(s[:, :, 0])
    top = jnp.concatenate([s[:, :, 0], z], axis=-1)
    bot = jnp.concatenate([z, s[:, :, 1]], axis=-1)
    return jnp.concatenate([top, bot], axis=-2)


def _unpair_state(s):
    bsz = s.shape[0]
    return jnp.stack([s[:, :, :HEAD_DIM, :HEAD_DIM], s[:, :, HEAD_DIM:, HEAD_DIM:]], axis=2).reshape(
        bsz, N_HEADS, HEAD_DIM, HEAD_DIM)


def _rel_bias(table, cq, wn):
    span = cq - 1 + wn
    offs = jnp.arange(-(cq - 1), wn + 1)
    diag = table[:, jnp.clip(BAND - offs, -MAX_REL, MAX_REL) + MAX_REL].astype(F32)
    skew = jnp.tile(diag, (1, cq))[:, :cq * span].reshape(-1, cq, span)
    return skew[:, :, cq - 1:]


def _layer(x, mod, state, wts, tm):
    bsz, t, _ = x.shape
    n = bsz * t
    c = min(CHUNK, t)
    shift0, wkv0, caches, conv0, s0 = state
    if t % tm == 0:
        tiles_per_group = t // tm
        mods = [m.reshape(bsz, 1, D_MODEL) for m in jnp.split(mod, 6, axis=-1)]
    else:
        tiles_per_group = 1
        mods = [jnp.repeat(m, t, axis=0).reshape(n // tm, tm, D_MODEL) for m in jnp.split(mod, 6, axis=-1)]
    sh_mix, sc_mix, g_mix, sh_ffn, sc_ffn, g_ffn = mods
    xf = x.reshape(n, D_MODEL)

    prompt = caches is None
    kv_tail_only = prompt and tm == BAND and t % tm == 0
    pa, q, k, v, kb, vb, pc, pab = _inproj(xf, sh_mix, sc_mix, wts["norm_mix_w"], wts["wa"], wts["wb"], wts["wc"],
                                           wts["wab"], wts["q_norm"], wts["k_norm"], tm, tiles_per_group,
                                           kv_tail_only)
    pa = pa.reshape(bsz, t, A_COLS)
    q, kb, vb = (z.reshape(bsz, t, MIX) for z in (q, kb, vb))
    k, v = (z.reshape(bsz, -1, MIX) for z in (k, v))
    pc = pc.reshape(bsz, t, C_MAIN)

    oa, wkv = _rwkv(pa, shift0.reshape(bsz, 1, A_COLS), _pair_state(jnp.swapaxes(wkv0, -1, -2)),
                    wts["mu"], wts["w0"], wts["a0"], wts["wwa"], wts["g_up"], wts["k_k"], wts["k_a"], wts["r_k"],
                    wts["gn_w"], wts["gn_b"], c)
    new_wkv = jnp.swapaxes(_unpair_state(wkv), -1, -2)

    if prompt:
        ob = _band_prompt(q, kb, vb, wts["rel_bias"])
        new_k, new_v = k[:, -BAND:], v[:, -BAND:]
    else:
        ob = _band_sample(q, k, v, caches, wts["rel_bias"])
        new_k, new_v = k, v

    conv_pad = jnp.pad(conv0, ((0, 0), (SUBLANES - (CONV_W - 1), 0), (0, 0)))
    oc, s_new = _gdn(pc, pab.reshape(bsz, t, LANES), conv_pad, _pair_state(s0), wts["conv_w"], wts["a_log"],
                     wts["dt_bias"], wts["gdn_norm_w"], wts["eab"], c)
    tail = min(t, CONV_W - 1)
    new_conv = jnp.concatenate([conv0, pc[:, t - tail:, :3 * MIX]], axis=1)[:, -(CONV_W - 1):]

    x1, h2, comb = _merge(xf, oa.reshape(n, MIX), ob.reshape(n, MIX), oc.reshape(n, MIX),
                          [sh_mix, sc_mix, g_mix, sh_ffn, sc_ffn], wts["norm_mix_w"], wts["norm_ffn_w"],
                          wts["w_gate"], wts["b_gate"], wts["w_branch"], wts["w_out"], wts["w_router_t"],
                          wts["router_bias"], tm, tiles_per_group)
    x2 = _moe(x1, h2, comb, g_ffn, wts["wg"], wts["wu"], wts["wd"], tm, tiles_per_group)
    heads = lambda z: z.reshape(bsz, -1, N_HEADS, HEAD_DIM)
    return x2.reshape(bsz, t, D_MODEL), (pa[:, -1], new_wkv, heads(new_k), heads(new_v), new_conv,
                                         _unpair_state(s_new))


def _prepare_layer(l, w_in, norm_mix_w, norm_ffn_w, rwkv_mu, rwkv_w0, rwkv_w_up, rwkv_a0, rwkv_a_up, rwkv_g_up,
                   rwkv_k_k, rwkv_k_a, rwkv_r_k, rwkv_gn_w, rwkv_gn_b, band_q_norm, band_k_norm, band_rel_bias,
                   gdn_conv_w, gdn_a_log, gdn_dt_bias, gdn_norm_w, w_branch, w_gate, b_gate, w_out,
                   w_router, router_bias, w_exp_gate, w_exp_up, w_exp_down):
    row = lambda z: z.reshape(1, -1).astype(F32)
    per_head = lambda z: jnp.repeat(z, HEAD_DIM).reshape(1, MIX)
    win = w_in[l].astype(BF16)
    c0 = A_COLS + B_COLS
    wab = jnp.pad(win[:, c0 + C_MAIN:], ((0, 0), (0, LANES - 2 * N_HEADS)))
    zeros = jnp.zeros((HEAD_DIM, MIX), F32)
    wwa = jnp.concatenate([jnp.concatenate([rwkv_w_up[l], zeros], axis=1),
                           jnp.concatenate([zeros, rwkv_a_up[l]], axis=1)], axis=0)
    head_of_lane = jnp.arange(MIX) // HEAD_DIM
    src = jnp.arange(LANES)[:, None]
    eab = jnp.concatenate([(src == head_of_lane[None, :]), (src == N_HEADS + head_of_lane[None, :])],
                          axis=1).astype(BF16)
    return dict(
        norm_mix_w=row(norm_mix_w[l]), norm_ffn_w=row(norm_ffn_w[l]),
        wa=win[:, :A_COLS], wb=win[:, A_COLS:c0], wc=win[:, c0:c0 + C_MAIN], wab=wab,
        q_norm=row(jnp.tile(band_q_norm[l], N_HEADS)), k_norm=row(jnp.tile(band_k_norm[l], N_HEADS)),
        mu=row(rwkv_mu[l]), w0=row(rwkv_w0[l]), a0=row(rwkv_a0[l]), wwa=wwa, g_up=rwkv_g_up[l],
        k_k=row(rwkv_k_k[l]), k_a=row(rwkv_k_a[l]), r_k=row(rwkv_r_k[l]), gn_w=row(rwkv_gn_w[l]),
        gn_b=row(rwkv_gn_b[l]), rel_bias=band_rel_bias[l],
        conv_w=gdn_conv_w[l], a_log=per_head(gdn_a_log[l]), dt_bias=per_head(gdn_dt_bias[l]),
        gdn_norm_w=row(jnp.tile(gdn_norm_w[l], N_HEADS)), eab=eab,
        w_gate=w_gate[l].astype(BF16), b_gate=row(b_gate[l]), w_branch=w_branch[l].astype(BF16),
        w_out=w_out[l].astype(BF16), w_router_t=jnp.transpose(w_router), router_bias=router_bias.reshape(-1, 1),
        wg=w_exp_gate[l].astype(BF16), wu=w_exp_up[l].astype(BF16),
        wd=w_exp_down[l].astype(BF16).reshape(N_GROUPS, -1, D_MODEL),
    )


def kernel(x_prompt, x_sample, c_prompt, c_sample, state_rwkv_shift, state_rwkv_wkv, cache_band_k, cache_band_v, state_gdn_conv, state_gdn_S, w_ada, b_ada, norm_mix_w, norm_ffn_w, w_in, rwkv_mu, rwkv_w0, rwkv_w_up, rwkv_a0, rwkv_a_up, rwkv_g_up, rwkv_k_k, rwkv_k_a, rwkv_r_k, rwkv_gn_w, rwkv_gn_b, band_q_norm, band_k_norm, band_rel_bias, gdn_conv_w, gdn_a_log, gdn_dt_bias, gdn_norm_w, w_branch, w_gate, b_gate, w_out, w_router, router_bias, w_exp_gate, w_exp_up, w_exp_down):
    depth = w_ada.shape[0]
    bsz = x_prompt.shape[0]
    mod = _adaln(jnp.concatenate([c_prompt, c_sample], axis=0), w_ada, b_ada)
    mod_p, mod_s = mod[:, :bsz], mod[:, bsz:]
    zero_state = (jnp.zeros((bsz, A_COLS), F32), jnp.zeros((bsz, N_HEADS, HEAD_DIM, HEAD_DIM), F32), None,
                  jnp.zeros((bsz, CONV_W - 1, 3 * MIX), F32), jnp.zeros((bsz, N_HEADS, HEAD_DIM, HEAD_DIM), F32))
    k_cache = cache_band_k.reshape(cache_band_k.shape[:3] + (MIX,))
    v_cache = cache_band_v.reshape(cache_band_v.shape[:3] + (MIX,))
    tm = TOKEN_TILE
    xp, xs = x_prompt, x_sample
    new_p, new_s = [], []
    for l in range(depth):
        wts = _prepare_layer(l, w_in, norm_mix_w, norm_ffn_w, rwkv_mu, rwkv_w0, rwkv_w_up, rwkv_a0, rwkv_a_up,
                             rwkv_g_up, rwkv_k_k, rwkv_k_a, rwkv_r_k, rwkv_gn_w, rwkv_gn_b, band_q_norm,
                             band_k_norm, band_rel_bias, gdn_conv_w, gdn_a_log, gdn_dt_bias, gdn_norm_w,
                             w_branch, w_gate, b_gate, w_out, w_router, router_bias, w_exp_gate, w_exp_up,
                             w_exp_down)
        xp, st_p = _layer(xp, mod_p[l], zero_state, wts, tm)
        xs, st_s = _layer(xs, mod_s[l], (state_rwkv_shift[l], state_rwkv_wkv[l], (k_cache, v_cache, l),
                                         state_gdn_conv[l], state_gdn_S[l]), wts, tm)
        new_p.append(st_p)
        new_s.append(st_s)
    p_out = [jnp.stack(z, axis=0) for z in zip(*new_p)]
    s_out = [jnp.stack(z, axis=0) for z in zip(*new_s)]
    return (xp, xs, *p_out, *s_out)
```
